```python
import jax, jax.numpy as jnp
from jax import lax
import numpy as np

D_MODEL = 2048
BATCH = 4
SEQ = 2048
DEPTH = 4

MLA_HEADS = 8
MLA_Q_LORA = 512
MLA_KV_LORA = 512
MLA_NOPE = 128
MLA_ROPE = 64
MLA_V = 128
MLA_QK = MLA_NOPE + MLA_ROPE
MLSTM_HEADS = 4
MLSTM_DK = 128
MLSTM_DV = 256
MLSTM_CHUNK = 128
CONV_WIDTH = 4
FORGET_BIAS = 3.0
DIL_HEADS = 16
DIL_HEAD_DIM = 128
DIL_PATTERNS = ((128, 1), (512, 4), (2048, 16))
DIL_BLOCK = 128
Q_BLOCK = 128
D_FF = 4 * D_MODEL
ROPE_THETA = 10000.0
NORM_EPS = 1e-6
N_EVEN = (DEPTH + 1) // 2
N_ODD = DEPTH // 2
EVEN_SPLIT_SIZES = (MLA_Q_LORA, MLA_KV_LORA, MLA_ROPE, 2 * MLSTM_HEADS * MLSTM_DK,
                    MLSTM_HEADS * MLSTM_DV, MLSTM_HEADS, MLSTM_HEADS, MLSTM_HEADS * MLSTM_DV)
EVEN_IN = sum(EVEN_SPLIT_SIZES)
EVEN_MIX = MLA_HEADS * MLA_V + MLSTM_HEADS * MLSTM_DV
ODD_MIX = DIL_HEADS * DIL_HEAD_DIM

kernel_name = 'hybrid_mla_mlstm_dilated_trunk'


def rms_norm(x, g):
    x32 = x.astype(jnp.float32)
    y = x32 * lax.rsqrt(jnp.mean(x32 * x32, axis=-1, keepdims=True) + NORM_EPS)
    return (y * g.astype(jnp.float32)).astype(x.dtype)


def rope(x, pos):
    d = x.shape[-1]
    half = d // 2
    inv = ROPE_THETA ** (-jnp.arange(half, dtype=jnp.float32) * 2.0 / d)
    ang = pos.astype(jnp.float32)[:, None] * inv[None, :]
    cos, sin = jnp.cos(ang), jnp.sin(ang)
    x32 = x.astype(jnp.float32)
    x1, x2 = x32[..., :half], x32[..., half:]
    return jnp.concatenate([x1 * cos - x2 * sin, x1 * sin + x2 * cos], axis=-1).astype(x.dtype)


def causal_attention(q, k, v, scale):
    B, H, S, dq = q.shape
    dv = v.shape[-1]
    nb = S // Q_BLOCK
    qb = q.reshape(B, H, nb, Q_BLOCK, dq).transpose(2, 0, 1, 3, 4)
    kpos = jnp.arange(S)

    def one_block(args):
        q_blk, i = args
        s = jnp.einsum('bhqd,bhkd->bhqk', q_blk, k).astype(jnp.float32) * scale
        qpos = i * Q_BLOCK + jnp.arange(Q_BLOCK)
        s = jnp.where(kpos[None, :] <= qpos[:, None], s, -jnp.inf)
        p = jax.nn.softmax(s, axis=-1)
        return jnp.einsum('bhqk,bhkd->bhqd', p.astype(v.dtype), v)

    out = lax.map(one_block, (qb, jnp.arange(nb)))
    return out.transpose(1, 2, 0, 3, 4).reshape(B, H, S, dv)


def causal_depthwise_conv(x, w, b):
    C = x.shape[-1]
    y = lax.conv_general_dilated(x, w[:, None, :], window_strides=(1,),
                                 padding=((CONV_WIDTH - 1, 0),),
                                 dimension_numbers=('NWC', 'WIO', 'NWC'),
                                 feature_group_count=C)
    return y + b


def mlstm_chunkwise(q, k, v, ig, lf):
    B, H, S, dk = q.shape
    dv = v.shape[-1]
    L = MLSTM_CHUNK
    nc = S // L

    def chunks(t):
        return jnp.moveaxis(t.reshape((B, H, nc, L) + t.shape[3:]), 2, 0)

    xs = (chunks(q), chunks(k), chunks(v), chunks(ig), chunks(lf))
    tril = jnp.tril(jnp.ones((L, L), dtype=bool))

    def step(carry, inp):
        C, n, m_prev = carry
        qc, kc, vc, ic, fc = inp
        b = jnp.cumsum(fc, axis=-1)
        D = b[..., :, None] - b[..., None, :] + ic[..., None, :]
        D = jnp.where(tril, D, -jnp.inf)
        m_inter = b + m_prev[..., None]
        m_row = jnp.maximum(m_inter, jnp.max(D, axis=-1))
        W = jnp.exp(D - m_row[..., None]) * jnp.einsum('bhtd,bhsd->bhts', qc, kc)
        inter = jnp.exp(m_inter - m_row)
        num = jnp.einsum('bhts,bhsv->bhtv', W, vc) + inter[..., None] * jnp.einsum('bhtd,bhdv->bhtv', qc, C)
        den = jnp.sum(W, axis=-1) + inter * jnp.einsum('bhtd,bhd->bht', qc, n)
        h = num / jnp.maximum(jnp.abs(den), jnp.exp(-m_row))[..., None]
        bL = b[..., -1]
        g = bL[..., None] - b + ic
        m_new = jnp.maximum(bL + m_prev, jnp.max(g, axis=-1))
        wk = jnp.exp(g - m_new[..., None])
        decay = jnp.exp(bL + m_prev - m_new)
        C_new = decay[..., None, None] * C + jnp.einsum('bhsd,bhsv->bhdv', kc * wk[..., None], vc)
        n_new = decay[..., None] * n + jnp.einsum('bhs,bhsd->bhd', wk, kc)
        return (C_new, n_new, m_new), h

    init = (jnp.zeros((B, H, dk, dv), jnp.float32), jnp.zeros((B, H, dk), jnp.float32),
            jnp.zeros((B, H), jnp.float32))
    _, h = lax.scan(step, init, xs)
    return jnp.moveaxis(h, 0, 2).reshape(B, H, S, dv)


def even_mixer(h, pos, w_in, q_norm, w_uq, kv_norm, w_ukv, conv_w, conv_b, b_i, b_f, w_out):
    B, S, _ = h.shape
    proj = h @ w_in
    cuts = np.cumsum(EVEN_SPLIT_SIZES)[:-1].tolist()
    c_q, c_kv, k_r, m_qk, m_v, m_i, m_f, m_o = jnp.split(proj, cuts, axis=-1)

    q = (rms_norm(c_q, q_norm) @ w_uq).reshape(B, S, MLA_HEADS, MLA_QK).transpose(0, 2, 1, 3)
    q = jnp.concatenate([q[..., :MLA_NOPE], rope(q[..., MLA_NOPE:], pos)], axis=-1)
    kv = (rms_norm(c_kv, kv_norm) @ w_ukv).reshape(B, S, MLA_HEADS, MLA_NOPE + MLA_V).transpose(0, 2, 1, 3)
    k_nope, v = kv[..., :MLA_NOPE], kv[..., MLA_NOPE:]
    k_rope = rope(k_r, pos)[:, None]
    k = jnp.concatenate([k_nope, jnp.broadcast_to(k_rope, (B, MLA_HEADS, S, MLA_ROPE))], axis=-1)
    a_out = causal_attention(q, k, v, MLA_QK ** -0.5)
    a_out = a_out.transpose(0, 2, 1, 3).reshape(B, S, MLA_HEADS * MLA_V)

    qk = jax.nn.silu(causal_depthwise_conv(m_qk, conv_w, conv_b))
    mq, mk = jnp.split(qk, 2, axis=-1)
    heads = lambda t, e: t.reshape(B, S, MLSTM_HEADS, e).transpose(0, 2, 1, 3).astype(jnp.float32)
    mq = heads(mq, MLSTM_DK)
    mk = heads(mk, MLSTM_DK) * (MLSTM_DK ** -0.5)
    mv = heads(m_v, MLSTM_DV)
    ig = (m_i + b_i).astype(jnp.float32).transpose(0, 2, 1)
    lf = jax.nn.log_sigmoid((m_f + b_f).astype(jnp.float32)).transpose(0, 2, 1)
    hm = mlstm_chunkwise(mq, mk, mv, ig, lf)
    hm = hm.transpose(0, 2, 1, 3).reshape(B, S, MLSTM_HEADS * MLSTM_DV)
    hm = (jax.nn.sigmoid(m_o.astype(jnp.float32)) * hm).astype(h.dtype)

    return jnp.concatenate([a_out, hm], axis=-1) @ w_out


def dilated_branch(q, k, v, window, dilation):
    B, H, S, dh = q.shape
    span = window // dilation
    L = S // dilation
    nb = -(-L // DIL_BLOCK)
    Lp = nb * DIL_BLOCK

    def residues(t):
        t = t.reshape(B, H, L, dilation, dh).transpose(0, 1, 3, 2, 4)
        t = jnp.pad(t, ((0, 0), (0, 0), (0, 0), (0, Lp - L), (0, 0)))
        return t.reshape(B, H, dilation, nb, DIL_BLOCK, dh)

    def band(t):
        prev = jnp.pad(t, ((0, 0), (0, 0), (0, 0), (1, 0), (0, 0), (0, 0)))[:, :, :, :nb]
        return jnp.concatenate([prev, t], axis=4)

    qb = residues(q)
    kw, vw = band(residues(k)), band(residues(v))
    s = jnp.einsum('bhrnqd,bhrnkd->bhrnqk', qb, kw).astype(jnp.float32) * (dh ** -0.5)
    qi = jnp.arange(DIL_BLOCK)[:, None]
    kj = jnp.arange(2 * DIL_BLOCK)[None, :]
    dist = DIL_BLOCK + qi - kj
    key_sub = (jnp.arange(nb)[:, None, None] - 1) * DIL_BLOCK + kj[None]
    mask = (dist >= 0) & (dist <= span) & (key_sub >= 0)
    s = jnp.where(mask, s, -jnp.inf)
    m = jnp.max(s, axis=-1, keepdims=True)
    p = jnp.exp(s - m)
    den = jnp.sum(p, axis=-1)
    o = jnp.einsum('bhrnqk,bhrnkd->bhrnqd', p.astype(vw.dtype), vw).astype(jnp.float32) / den[..., None]
    lse = m[..., 0] + jnp.log(den)

    def merge(t):
        e = t.shape[-1]
        t = t.reshape(B, H, dilation, Lp, e)[:, :, :, :L]
        return t.transpose(0, 1, 3, 2, 4).reshape(B, H, S, e)

    return merge(o), merge(lse[..., None])[..., 0]


def odd_mixer(h, pos, w_qkv, w_out):
    B, S, _ = h.shape
    qkv = (h @ w_qkv).reshape(B, S, 3, DIL_HEADS, DIL_HEAD_DIM).transpose(2, 0, 3, 1, 4)
    q, k, v = rope(qkv[0], pos), rope(qkv[1], pos), qkv[2]
    outs, lses = [], []
    for window, dilation in DIL_PATTERNS:
        o_g, lse_g = dilated_branch(q, k, v, window, dilation)
        outs.append(o_g)
        lses.append(lse_g)
    alpha = jax.nn.softmax(jnp.stack(lses), axis=0)
    o = jnp.einsum('gbhs,gbhsd->bhsd', alpha, jnp.stack(outs))
    o = o.transpose(0, 2, 1, 3).reshape(B, S, ODD_MIX).astype(h.dtype)
    return o @ w_out


def squared_relu_mlp(h, w1, w2):
    return jnp.square(jax.nn.relu(h @ w1)) @ w2


def setup_inputs(seed: int = 0) -> dict:
    key = jax.random.key(seed)
    ks = jax.random.split(key, 18)
    f32 = jnp.float32
    res = (2 * DEPTH) ** -0.5

    def dense(k, shape, fan_in, gain=1.0):
        return jax.random.normal(k, shape, f32) * (gain * fan_in ** -0.5)

    def norm_gain(k, shape):
        return 1.0 + 0.05 * jax.random.normal(k, shape, f32)

    return {
        'x': jax.random.normal(ks[0], (BATCH, SEQ, D_MODEL), f32),
        'norm_mix': norm_gain(ks[1], (DEPTH, D_MODEL)),
        'norm_mlp': norm_gain(ks[2], (DEPTH, D_MODEL)),
        'ev_w_in': dense(ks[3], (N_EVEN, D_MODEL, EVEN_IN), D_MODEL),
        'mla_q_norm': norm_gain(ks[4], (N_EVEN, MLA_Q_LORA)),
        'mla_w_uq': dense(ks[5], (N_EVEN, MLA_Q_LORA, MLA_HEADS * MLA_QK), MLA_Q_LORA),
        'mla_kv_norm': norm_gain(ks[6], (N_EVEN, MLA_KV_LORA)),
        'mla_w_ukv': dense(ks[7], (N_EVEN, MLA_KV_LORA, MLA_HEADS * (MLA_NOPE + MLA_V)), MLA_KV_LORA),
        'mlstm_conv_w': dense(ks[8], (N_EVEN, CONV_WIDTH, 2 * MLSTM_HEADS * MLSTM_DK), CONV_WIDTH),
        'mlstm_conv_b': 0.01 * jax.random.normal(ks[9], (N_EVEN, 2 * MLSTM_HEADS * MLSTM_DK), f32),
        'mlstm_b_i': 0.1 * jax.random.normal(ks[10], (N_EVEN, MLSTM_HEADS), f32),
        'mlstm_b_f': FORGET_BIAS + 0.1 * jax.random.normal(ks[11], (N_EVEN, MLSTM_HEADS), f32),
        'ev_w_out': dense(ks[12], (N_EVEN, EVEN_MIX, D_MODEL), EVEN_MIX, res),
        'od_w_qkv': dense(ks[13], (N_ODD, D_MODEL, 3 * ODD_MIX), D_MODEL),
        'od_w_out': dense(ks[14], (N_ODD, ODD_MIX, D_MODEL), ODD_MIX, res),
        'mlp_w1': dense(ks[15], (DEPTH, D_MODEL, D_FF), D_MODEL),
        'mlp_w2': dense(ks[16], (DEPTH, D_FF, D_MODEL), D_FF, res),
        'norm_final': norm_gain(ks[17], (D_MODEL,)),
    }


def reference(x, norm_mix, norm_mlp, ev_w_in, mla_q_norm, mla_w_uq, mla_kv_norm, mla_w_ukv,
              mlstm_conv_w, mlstm_conv_b, mlstm_b_i, mlstm_b_f, ev_w_out, od_w_qkv, od_w_out,
              mlp_w1, mlp_w2, norm_final):
    pos = jnp.arange(x.shape[1])
    for layer in range(DEPTH):
        i = layer // 2
        h = rms_norm(x, norm_mix[layer])
        if layer % 2 == 0:
            mix = even_mixer(h, pos, ev_w_in[i], mla_q_norm[i], mla_w_uq[i], mla_kv_norm[i],
                             mla_w_ukv[i], mlstm_conv_w[i], mlstm_conv_b[i], mlstm_b_i[i],
                             mlstm_b_f[i], ev_w_out[i])
        else:
            mix = odd_mixer(h, pos, od_w_qkv[i], od_w_out[i])
        x = x + mix
        x = x + squared_relu_mlp(rms_norm(x, norm_mlp[layer]), mlp_w1[layer], mlp_w2[layer])
    return rms_norm(x, norm_final)
```

```python
import functools

import jax
import jax.numpy as jnp
import numpy as np
from jax import lax
from jax.experimental import pallas as pl
from jax.experimental.pallas import tpu as pltpu

D_MODEL = 2048
BATCH = 4
SEQ = 2048
DEPTH = 4
MLA_HEADS = 8
MLA_Q_LORA = 512
MLA_KV_LORA = 512
MLA_NOPE = 128
MLA_ROPE = 64
MLA_V = 128
MLA_QK = MLA_NOPE + MLA_ROPE
MLSTM_HEADS = 4
MLSTM_DK = 128
MLSTM_DV = 256
MLSTM_CHUNK = 128
CONV_WIDTH = 4
DIL_HEADS = 16
DIL_HEAD_DIM = 128
DIL_PATTERNS = ((128, 1), (512, 4), (2048, 16))
DIL_BLOCK = 128
D_FF = 4 * D_MODEL
ROPE_THETA = 10000.0
NORM_EPS = 1e-6
ODD_MIX = DIL_HEADS * DIL_HEAD_DIM
TOKENS = BATCH * SEQ

LANES = 128
VMEM_LIMIT_BYTES = 56 * 1024 * 1024

EV_CQ, EV_CKV, EV_MQK, EV_MV, EV_MO, EV_TAIL = 0, 512, 1024, 2048, 3072, 4096
EV_PROJ = 4224
TAIL_I = MLA_ROPE
TAIL_F = MLA_ROPE + MLSTM_HEADS

F32 = jnp.float32
BF16 = jnp.bfloat16
NEG_INF = float("-inf")


def _params(semantics):
    return pltpu.CompilerParams(dimension_semantics=semantics, vmem_limit_bytes=VMEM_LIMIT_BYTES)


def _rms_normalize(x, g):
    ms = jnp.mean(x * x, axis=-1, keepdims=True)
    return x * lax.rsqrt(ms + NORM_EPS) * g


def _dot(a, b):
    return jnp.dot(a, b, preferred_element_type=F32)


def _dot_nt(a, b):
    return lax.dot_general(a, b, (((1,), (1,)), ((), ())), preferred_element_type=F32)


def _dot_tn(a, b):
    return lax.dot_general(a, b, (((0,), (0,)), ((), ())), preferred_element_type=F32)


def _sigmoid(x):
    return 1.0 / (1.0 + jnp.exp(-x))


def _log_sigmoid(x):
    return jnp.minimum(x, 0.0) - jnp.log1p(jnp.exp(-jnp.abs(x)))


def _rope_small(y, tab):
    return (y * tab[:, 0:LANES] + pltpu.roll(y, 96, 1) * tab[:, LANES:2 * LANES]
            + pltpu.roll(y, 32, 1) * tab[:, 2 * LANES:3 * LANES])


def _rope_full(y, tab):
    return y * tab[:, 0:LANES] + pltpu.roll(y, 64, 1) * tab[:, LANES:2 * LANES]


def _norm_to_scratch(x_ref, g_ref, xn_ref):
    @pl.when(pl.program_id(1) == 0)
    def _():
        xn_ref[...] = _rms_normalize(x_ref[...], g_ref[...]).astype(BF16)


def _rms_mm_plain_kernel(x_ref, g_ref, w_ref, o_ref, xn_ref):
    _norm_to_scratch(x_ref, g_ref, xn_ref)
    o_ref[...] = _dot(xn_ref[...], w_ref[...]).astype(o_ref.dtype)


def _rms_mm_mlaq_kernel(x_ref, g_ref, w_ref, tab_ref, o_ref, xn_ref, *, scale):
    _norm_to_scratch(x_ref, g_ref, xn_ref)
    y = _dot(xn_ref[...], w_ref[...])
    tab = tab_ref[...]
    for grp in range(y.shape[1] // LANES):
        yg = y[:, grp * LANES:(grp + 1) * LANES]
        og = yg * scale if grp % 2 == 0 else _rope_small(yg, tab)
        o_ref[:, grp * LANES:(grp + 1) * LANES] = og.astype(o_ref.dtype)


def _rms_mm_qkv_kernel(x_ref, g_ref, w_ref, tab_ref, o_ref, xn_ref, *, rope_tiles):
    _norm_to_scratch(x_ref, g_ref, xn_ref)
    y = _dot(xn_ref[...], w_ref[...])
    j = pl.program_id(1)

    @pl.when(j < rope_tiles)
    def _():
        tab = tab_ref[...]
        for grp in range(y.shape[1] // LANES):
            yg = y[:, grp * LANES:(grp + 1) * LANES]
            o_ref[:, grp * LANES:(grp + 1) * LANES] = _rope_full(yg, tab).astype(o_ref.dtype)

    @pl.when(j >= rope_tiles)
    def _():
        o_ref[...] = y.astype(o_ref.dtype)


def rms_matmul(x, x_col_block, k_dim, g, w, *, tm, tn, out_dtype, mode="plain", tab=None, scale=None):
    t = x.shape[0]
    n = w.shape[1]
    assert t % tm == 0 and n % tn == 0 and w.shape[0] == k_dim
    grid = (t // tm, n // tn)
    pos_tiles = SEQ // tm
    in_specs = [
        pl.BlockSpec((tm, k_dim), lambda i, j: (i, x_col_block)),
        pl.BlockSpec((1, k_dim), lambda i, j: (0, 0)),
        pl.BlockSpec((k_dim, tn), lambda i, j: (0, j)),
    ]
    args = [x, g.reshape(1, k_dim), w]
    if mode == "plain":
        body = _rms_mm_plain_kernel
    elif mode == "mlaq":
        body = functools.partial(_rms_mm_mlaq_kernel, scale=scale)
        in_specs.append(pl.BlockSpec((tm, 3 * LANES), lambda i, j: (i % pos_tiles, 0)))
        args.append(tab)
    elif mode == "qkv":
        q_tiles = ODD_MIX // tn
        body = functools.partial(_rms_mm_qkv_kernel, rope_tiles=2 * q_tiles)
        in_specs.append(pl.BlockSpec((None, tm, 2 * LANES),
                                     lambda i, j: (jnp.minimum(j // q_tiles, 1), i % pos_tiles, 0)))
        args.append(tab)
    else:
        raise ValueError(mode)
    return pl.pallas_call(
        body,
        out_shape=jax.ShapeDtypeStruct((t, n), out_dtype),
        grid=grid,
        in_specs=in_specs,
        out_specs=pl.BlockSpec((tm, tn), lambda i, j: (i, j)),
        scratch_shapes=[pltpu.VMEM((tm, k_dim), BF16)],
        compiler_params=_params(("parallel", "arbitrary")),
        name="rms_matmul_" + mode,
    )(*args)


def _krope_kernel(p_ref, tab_ref, o_ref):
    o_ref[...] = _rope_small(p_ref[...], tab_ref[...]).astype(o_ref.dtype)


def krope(proj, tab, *, tm):
    pos_tiles = SEQ // tm
    return pl.pallas_call(
        _krope_kernel,
        out_shape=jax.ShapeDtypeStruct((TOKENS, LANES), BF16),
        grid=(TOKENS // tm,),
        in_specs=[pl.BlockSpec((tm, LANES), lambda i: (i, EV_TAIL // LANES)),
                  pl.BlockSpec((tm, 3 * LANES), lambda i: (i % pos_tiles, 0))],
        out_specs=pl.BlockSpec((tm, LANES), lambda i: (i, 0)),
        compiler_params=_params(("parallel",)),
        name="krope",
    )(proj, tab)


def _mm_res2_kernel(a1_ref, w1_ref, a2_ref, w2_ref, r_ref, o_ref):
    o_ref[...] = r_ref[...] + _dot(a1_ref[...], w1_ref[...]) + _dot(a2_ref[...], w2_ref[...])


def _mm_res1_kernel(a1_ref, w1_ref, r_ref, o_ref):
    o_ref[...] = r_ref[...] + _dot(a1_ref[...], w1_ref[...])


def matmul_residual(res, pairs, *, tm, tn):
    t, n = res.shape
    grid = (t // tm, n // tn)
    in_specs, args = [], []
    for a, w in pairs:
        k = a.shape[1]
        in_specs += [pl.BlockSpec((tm, k), lambda i, j: (i, 0)), pl.BlockSpec((k, tn), lambda i, j: (0, j))]
        args += [a, w]
    in_specs.append(pl.BlockSpec((tm, tn), lambda i, j: (i, j)))
    args.append(res)
    body = _mm_res2_kernel if len(pairs) == 2 else _mm_res1_kernel
    return pl.pallas_call(
        body,
        out_shape=jax.ShapeDtypeStruct((t, n), F32),
        grid=grid,
        in_specs=in_specs,
        out_specs=pl.BlockSpec((tm, tn), lambda i, j: (i, j)),
        compiler_params=_params(("parallel", "arbitrary")),
        name="matmul_residual",
    )(*args)


def _mlp_kernel(x_ref, g_ref, w1_ref, w2_ref, gf_ref, o_ref, xn_ref, *, final_norm):
    f = pl.program_id(1)

    @pl.when(f == 0)
    def _():
        x = x_ref[...]
        xn_ref[...] = _rms_normalize(x, g_ref[...]).astype(BF16)
        o_ref[...] = x

    h = jnp.maximum(_dot(xn_ref[...], w1_ref[...]), 0.0)
    o_ref[...] += _dot((h * h).astype(BF16), w2_ref[...])

    if final_norm:
        @pl.when(f == pl.num_programs(1) - 1)
        def _():
            o_ref[...] = _rms_normalize(o_ref[...], gf_ref[...])


def mlp_block(x, g, w1, w2, gf, *, tm, tf, final_norm):
    t, d = x.shape
    dff = w1.shape[1]
    return pl.pallas_call(
        functools.partial(_mlp_kernel, final_norm=final_norm),
        out_shape=jax.ShapeDtypeStruct((t, d), F32),
        grid=(t // tm, dff // tf),
        in_specs=[
            pl.BlockSpec((tm, d), lambda i, f: (i, 0)),
            pl.BlockSpec((1, d), lambda i, f: (0, 0)),
            pl.BlockSpec((d, tf), lambda i, f: (0, f)),
            pl.BlockSpec((tf, d), lambda i, f: (f, 0)),
            pl.BlockSpec((1, d), lambda i, f: (0, 0)),
        ],
        out_specs=pl.BlockSpec((tm, d), lambda i, f: (i, 0)),
        scratch_shapes=[pltpu.VMEM((tm, d), BF16)],
        compiler_params=_params(("parallel", "arbitrary")),
        name="mlp_block",
    )(x, g.reshape(1, d), w1, w2, gf.reshape(1, d))


def _mla_attn_kernel(q_ref, kv_ref, kr_ref, o_ref, m_ref, l_ref, acc_ref, *, tq):
    qi = pl.program_id(2)
    q = q_ref[...]
    m_ref[...] = jnp.full(m_ref.shape, NEG_INF, F32)
    l_ref[...] = jnp.zeros(l_ref.shape, F32)
    acc_ref[...] = jnp.zeros(acc_ref.shape, F32)

    def block(ki, masked):
        start = pl.multiple_of(ki * tq, tq)
        kn = kv_ref[pl.ds(start, tq), 0:MLA_NOPE]
        v = kv_ref[pl.ds(start, tq), MLA_NOPE:MLA_NOPE + MLA_V]
        kr = kr_ref[pl.ds(start, tq), :]
        k = jnp.concatenate([kn, kr], axis=-1)
        s = _dot_nt(q, k)
        if masked:
            row = lax.broadcasted_iota(jnp.int32, s.shape, 0)
            col = lax.broadcasted_iota(jnp.int32, s.shape, 1)
            s = jnp.where(col <= row, s, NEG_INF)
        m_prev = m_ref[...]
        m_new = jnp.maximum(m_prev, jnp.max(s, axis=-1, keepdims=True))
        p = jnp.exp(s - m_new)
        alpha = jnp.exp(m_prev - m_new)
        l_ref[...] = alpha * l_ref[...] + jnp.sum(p, axis=-1, keepdims=True)
        acc_ref[...] = alpha * acc_ref[...] + _dot(p.astype(BF16), v)
        m_ref[...] = m_new

    def body(ki, carry):
        block(ki, False)
        return carry

    lax.fori_loop(0, qi, body, 0)
    block(qi, True)
    o_ref[...] = (acc_ref[...] / l_ref[...]).astype(o_ref.dtype)


def mla_attention(q, kv, kr, *, tq):
    nq = SEQ // tq
    hw = 2 * LANES
    return pl.pallas_call(
        functools.partial(_mla_attn_kernel, tq=tq),
        out_shape=jax.ShapeDtypeStruct((TOKENS, MLA_HEADS * MLA_V), BF16),
        grid=(BATCH, MLA_HEADS, nq),
        in_specs=[
            pl.BlockSpec((tq, hw), lambda b, h, i: (b * nq + i, h)),
            pl.BlockSpec((SEQ, hw), lambda b, h, i: (b, h)),
            pl.BlockSpec((SEQ, LANES), lambda b, h, i: (b, 0)),
        ],
        out_specs=pl.BlockSpec((tq, MLA_V), lambda b, h, i: (b * nq + i, h)),
        scratch_shapes=[pltpu.VMEM((tq, 1), F32), pltpu.VMEM((tq, 1), F32), pltpu.VMEM((tq, MLA_V), F32)],
        compiler_params=_params(("parallel", "parallel", "arbitrary")),
        name="mla_attention",
    )(q, kv, kr)


def _mlstm_kernel(qk_ref, v_ref, og_ref, gate_ref, cw_ref, cb_ref, gb_ref, out_ref,
                  hist_ref, c_ref, n_ref, m_ref):
    chunk = MLSTM_CHUNK
    c = pl.program_id(1)

    @pl.when(c == 0)
    def _():
        hist_ref[0:8, :] = jnp.zeros((8, hist_ref.shape[1]), F32)
        c_ref[...] = jnp.zeros(c_ref.shape, F32)
        n_ref[...] = jnp.zeros(n_ref.shape, F32)
        m_ref[...] = jnp.zeros(m_ref.shape, F32)

    hist_ref[8:8 + chunk, :] = qk_ref[...]
    y = jnp.broadcast_to(cb_ref[...], (chunk, hist_ref.shape[1]))
    for j in range(CONV_WIDTH):
        off = 8 - (CONV_WIDTH - 1) + j
        y = y + cw_ref[j:j + 1, :] * hist_ref[off:off + chunk, :]
    qk = y * _sigmoid(y)
    hist_ref[0:8, :] = hist_ref[chunk:chunk + 8, :]

    gates = gate_ref[...] + gb_ref[...]
    logf = _log_sigmoid(gates)
    row = lax.broadcasted_iota(jnp.int32, (chunk, chunk), 0)
    col = lax.broadcasted_iota(jnp.int32, (chunk, chunk), 1)
    tril = col <= row
    bcum = jnp.dot(tril.astype(F32), logf, preferred_element_type=F32,
                   precision=lax.Precision.HIGHEST)
    gates_t = gates.T
    bcum_t = bcum.T

    dk, dv = MLSTM_DK, MLSTM_DV
    for h in range(MLSTM_HEADS):
        q = qk[:, h * dk:(h + 1) * dk]
        k = qk[:, (MLSTM_HEADS + h) * dk:(MLSTM_HEADS + h + 1) * dk] * (dk ** -0.5)
        v = v_ref[:, h * dv:(h + 1) * dv].astype(BF16)
        b_c = bcum[:, TAIL_F + h:TAIL_F + h + 1]
        b_r = bcum_t[TAIL_F + h:TAIL_F + h + 1, :]
        i_c = gates[:, TAIL_I + h:TAIL_I + h + 1]
        i_r = gates_t[TAIL_I + h:TAIL_I + h + 1, :]
        b_last = bcum[chunk - 1:chunk, TAIL_F + h:TAIL_F + h + 1]
        m_prev = m_ref[h:h + 1, 0:1]
        n_prev = n_ref[h:h + 1, :]
        c_prev = c_ref[h]

        dmat = jnp.where(tril, b_c - b_r + i_r, NEG_INF)
        m_inter = b_c + m_prev
        m_row = jnp.maximum(m_inter, jnp.max(dmat, axis=-1, keepdims=True))
        q_b = q.astype(BF16)
        wmat = jnp.exp(dmat - m_row) * _dot_nt(q_b, k.astype(BF16))
        inter = jnp.exp(m_inter - m_row)
        num = _dot(wmat.astype(BF16), v) + inter * _dot(q_b, c_prev.astype(BF16))
        den = jnp.sum(wmat, axis=-1, keepdims=True) + inter * jnp.sum(q * n_prev, axis=-1, keepdims=True)
        hcell = num / jnp.maximum(jnp.abs(den), jnp.exp(-m_row))
        gate_o = _sigmoid(og_ref[:, h * dv:(h + 1) * dv])
        out_ref[:, h * dv:(h + 1) * dv] = (gate_o * hcell).astype(out_ref.dtype)

        g_c = b_last - b_c + i_c
        m_new = jnp.maximum(b_last + m_prev, jnp.max(g_c, axis=0, keepdims=True))
        wk = jnp.exp(g_c - m_new)
        decay = jnp.exp(b_last + m_prev - m_new)
        kw = k * wk
        c_ref[h] = decay * c_prev + _dot_tn(kw.astype(BF16), v)
        n_ref[h:h + 1, :] = decay * n_prev + jnp.sum(kw, axis=0, keepdims=True)
        m_ref[h:h + 1, :] = jnp.broadcast_to(m_new, (1, LANES))


def mlstm(proj, conv_w, conv_b, gate_bias):
    nc = SEQ // MLSTM_CHUNK
    chunk = MLSTM_CHUNK
    wqk = 2 * MLSTM_HEADS * MLSTM_DK
    wv = MLSTM_HEADS * MLSTM_DV
    row = lambda b, c: b * nc + c
    return pl.pallas_call(
        _mlstm_kernel,
        out_shape=jax.ShapeDtypeStruct((TOKENS, wv), BF16),
        grid=(BATCH, nc),
        in_specs=[
            pl.BlockSpec((chunk, wqk), lambda b, c: (row(b, c), EV_MQK // wqk)),
            pl.BlockSpec((chunk, wv), lambda b, c: (row(b, c), EV_MV // wv)),
            pl.BlockSpec((chunk, wv), lambda b, c: (row(b, c), EV_MO // wv)),
            pl.BlockSpec((chunk, LANES), lambda b, c: (row(b, c), EV_TAIL // LANES)),
            pl.BlockSpec((CONV_WIDTH, wqk), lambda b, c: (0, 0)),
            pl.BlockSpec((1, wqk), lambda b, c: (0, 0)),
            pl.BlockSpec((1, LANES), lambda b, c: (0, 0)),
        ],
        out_specs=pl.BlockSpec((chunk, wv), lambda b, c: (row(b, c), 0)),
        scratch_shapes=[
            pltpu.VMEM((chunk + 8, wqk), F32),
            pltpu.VMEM((MLSTM_HEADS, MLSTM_DK, MLSTM_DV), F32),
            pltpu.VMEM((8, MLSTM_DK), F32),
            pltpu.VMEM((8, LANES), F32),
        ],
        compiler_params=_params(("parallel", "arbitrary")),
        name="mlstm",
    )(proj, proj, proj, proj, conv_w, conv_b.reshape(1, wqk), gate_bias)


def _dilated_kernel(*refs, nblk, has_prev, span, heads):
    if has_prev:
        q_ref, kp_ref, kc_ref, vp_ref, vc_ref, o_ref, lse_ref = refs
    else:
        q_ref, kc_ref, vc_ref, o_ref, lse_ref = refs
        kp_ref = vp_ref = None
    n = pl.program_id(2)
    hg = pl.program_id(3)
    blk_sz = DIL_BLOCK
    dh = DIL_HEAD_DIM
    qi = lax.broadcasted_iota(jnp.int32, (blk_sz, blk_sz), 0)
    kj = lax.broadcasted_iota(jnp.int32, (blk_sz, blk_sz), 1)
    mask_cur = (qi - kj >= 0) & (qi - kj <= span)
    dist_prev = blk_sz + qi - kj
    lane = lax.broadcasted_iota(jnp.int32, (blk_sz, LANES), 1)

    @pl.when(hg == 0)
    def _():
        lse_ref[...] = jnp.zeros(lse_ref.shape, F32)

    for blk in range(nblk):
        rows = slice(blk * blk_sz, (blk + 1) * blk_sz)
        lse_tile = lse_ref[rows, :]
        for h in range(heads):
            cols = slice(h * dh, (h + 1) * dh)
            q = q_ref[rows, cols]
            k_cur = kc_ref[rows, cols]
            v_cur = vc_ref[rows, cols]
            s_cur = jnp.where(mask_cur, _dot_nt(q, k_cur), NEG_INF)
            m = jnp.max(s_cur, axis=-1, keepdims=True)
            if has_prev:
                if blk == 0:
                    k_prev = kp_ref[:, cols]
                    v_prev = vp_ref[:, cols]
                    mask_prev = dist_prev <= jnp.where(n > 0, span, 0)
                else:
                    prows = slice((blk - 1) * blk_sz, blk * blk_sz)
                    k_prev = kc_ref[prows, cols]
                    v_prev = vc_ref[prows, cols]
                    mask_prev = dist_prev <= span
                s_prev = jnp.where(mask_prev, _dot_nt(q, k_prev), NEG_INF)
                m = jnp.maximum(m, jnp.max(s_prev, axis=-1, keepdims=True))
            p_cur = jnp.exp(s_cur - m)
            den = jnp.sum(p_cur, axis=-1, keepdims=True)
            acc = _dot(p_cur.astype(BF16), v_cur)
            if has_prev:
                p_prev = jnp.exp(s_prev - m)
                den = den + jnp.sum(p_prev, axis=-1, keepdims=True)
                acc = acc + _dot(p_prev.astype(BF16), v_prev)
            o_ref[rows, cols] = acc / den
            lse = m + jnp.log(den)
            lse_tile = jnp.where(lane == hg * heads + h, lse, lse_tile)
        lse_ref[rows, :] = lse_tile


def dilated_pattern(qkv, window, dilation, *, hw):
    span = window // dilation
    length = SEQ // dilation
    tl = min(length, 4 * DIL_BLOCK)
    nblk = tl // DIL_BLOCK
    ntile = length // tl
    has_prev = length > DIL_BLOCK
    heads = hw // DIL_HEAD_DIM
    ngroups = ODD_MIX // hw
    view = qkv.reshape(BATCH, length, dilation * 3 * ODD_MIX)
    qkv_groups = 3 * ngroups

    def cur(which):
        return pl.BlockSpec((None, tl, hw), lambda b, r, n, g: (b, n, r * qkv_groups + which * ngroups + g))

    def prev(which):
        return pl.BlockSpec((None, DIL_BLOCK, hw),
                            lambda b, r, n, g: (b, jnp.maximum(n * nblk - 1, 0),
                                                r * qkv_groups + which * ngroups + g))

    if has_prev:
        in_specs = [cur(0), prev(1), cur(1), prev(2), cur(2)]
        args = [view] * 5
    else:
        in_specs = [cur(0), cur(1), cur(2)]
        args = [view] * 3
    o, lse = pl.pallas_call(
        functools.partial(_dilated_kernel, nblk=nblk, has_prev=has_prev, span=span, heads=heads),
        out_shape=(jax.ShapeDtypeStruct((BATCH, length, dilation * ODD_MIX), F32),
                   jax.ShapeDtypeStruct((BATCH, length, dilation * LANES), F32)),
        grid=(BATCH, dilation, ntile, ngroups),
        in_specs=in_specs,
        out_specs=(pl.BlockSpec((None, tl, hw), lambda b, r, n, g: (b, n, r * ngroups + g)),
                   pl.BlockSpec((None, tl, LANES), lambda b, r, n, g: (b, n, r))),
        compiler_params=_params(("parallel", "parallel", "arbitrary", "arbitrary")),
        name="dilated_d%d" % dilation,
    )(*args)
    return o.reshape(TOKENS, ODD_MIX), lse.reshape(TOKENS, LANES)


def _merge_kernel(o1_ref, o2_ref, o3_ref, l1_ref, l2_ref, l3_ref, out_ref):
    dh = DIL_HEAD_DIM
    l1, l2, l3 = l1_ref[...], l2_ref[...], l3_ref[...]
    mx = jnp.maximum(jnp.maximum(l1, l2), l3)
    e1, e2, e3 = jnp.exp(l1 - mx), jnp.exp(l2 - mx), jnp.exp(l3 - mx)
    tot = e1 + e2 + e3
    a1, a2, a3 = e1 / tot, e2 / tot, e3 / tot
    for h in range(DIL_HEADS):
        cols = slice(h * dh, (h + 1) * dh)
        o = (a1[:, h:h + 1] * o1_ref[:, cols] + a2[:, h:h + 1] * o2_ref[:, cols]
             + a3[:, h:h + 1] * o3_ref[:, cols])
        out_ref[:, cols] = o.astype(out_ref.dtype)


def merge_patterns(outs, lses, *, tm):
    o_spec = pl.BlockSpec((tm, ODD_MIX), lambda i: (i, 0))
    l_spec = pl.BlockSpec((tm, LANES), lambda i: (i, 0))
    return pl.pallas_call(
        _merge_kernel,
        out_shape=jax.ShapeDtypeStruct((TOKENS, ODD_MIX), BF16),
        grid=(TOKENS // tm,),
        in_specs=[o_spec] * 3 + [l_spec] * 3,
        out_specs=o_spec,
        compiler_params=_params(("parallel",)),
        name="merge_patterns",
    )(*outs, *lses)


def _rope_tables():
    pos = jnp.arange(SEQ, dtype=F32)[:, None]
    half = MLA_ROPE // 2
    inv = ROPE_THETA ** (-jnp.arange(half, dtype=F32) * 2.0 / MLA_ROPE)
    ang = pos * inv[None, :]
    c, s = jnp.cos(ang), jnp.sin(ang)
    z = jnp.zeros_like(c)
    small = jnp.concatenate([c, c, z, z, -s, z, z, z, z, s, z, z], axis=-1)
    half = DIL_HEAD_DIM // 2
    inv = ROPE_THETA ** (-jnp.arange(half, dtype=F32) * 2.0 / DIL_HEAD_DIM)
    ang = pos * inv[None, :]
    c, s = jnp.cos(ang), jnp.sin(ang)
    full = jnp.concatenate([c, c, -s, s], axis=-1)
    return small, full


def _even_weights(w_in, w_uq, w_ukv, b_i, b_f):
    cuts = np.cumsum((MLA_Q_LORA, MLA_KV_LORA, MLA_ROPE, 2 * MLSTM_HEADS * MLSTM_DK,
                      MLSTM_HEADS * MLSTM_DV, MLSTM_HEADS, MLSTM_HEADS))
    c_q, c_kv, k_r, m_qk, m_v, m_i, m_f, m_o = jnp.split(w_in, cuts.tolist(), axis=1)
    pad = jnp.zeros((D_MODEL, EV_PROJ - EV_TAIL - MLA_ROPE - 2 * MLSTM_HEADS), w_in.dtype)
    w_in_r = jnp.concatenate([c_q, c_kv, m_qk, m_v, m_o, k_r, m_i, m_f, pad], axis=1).astype(BF16)
    uq = w_uq.reshape(MLA_Q_LORA, MLA_HEADS, MLA_QK)
    uq = jnp.pad(uq, ((0, 0), (0, 0), (0, 2 * LANES - MLA_QK)))
    w_uq_r = uq.reshape(MLA_Q_LORA, MLA_HEADS * 2 * LANES).astype(BF16)
    gate_bias = jnp.concatenate([jnp.zeros((TAIL_I,), F32), b_i.astype(F32), b_f.astype(F32),
                                 jnp.zeros((LANES - TAIL_F - MLSTM_HEADS,), F32)]).reshape(1, LANES)
    return w_in_r, w_uq_r, w_ukv.astype(BF16), gate_bias


def _even_layer(x, g_mix, w_in, q_norm, w_uq, kv_norm, w_ukv, conv_w, conv_b, b_i, b_f, w_out, small_tab):
    w_in_r, w_uq_r, w_ukv_r, gate_bias = _even_weights(w_in, w_uq, w_ukv, b_i, b_f)
    proj = rms_matmul(x, 0, D_MODEL, g_mix, w_in_r, tm=512, tn=EV_PROJ // 3, out_dtype=F32)
    scale = MLA_QK ** -0.5
    q = rms_matmul(proj, EV_CQ // MLA_Q_LORA, MLA_Q_LORA, q_norm, w_uq_r, tm=512, tn=2048,
                   out_dtype=BF16, mode="mlaq", tab=small_tab * scale, scale=scale)
    kv = rms_matmul(proj, EV_CKV // MLA_KV_LORA, MLA_KV_LORA, kv_norm, w_ukv_r, tm=512, tn=2048,
                    out_dtype=BF16)
    kr = krope(proj, small_tab, tm=1024)
    a_out = mla_attention(q, kv, kr, tq=256)
    hm = mlstm(proj, conv_w, conv_b, gate_bias)
    w_out_b = w_out.astype(BF16)
    n_a = MLA_HEADS * MLA_V
    return matmul_residual(x, [(a_out, w_out_b[:n_a]), (hm, w_out_b[n_a:])], tm=512, tn=1024)


def _odd_layer(x, g_mix, w_qkv, w_out, full_tab):
    scale = DIL_HEAD_DIM ** -0.5
    tab = jnp.stack([full_tab * scale, full_tab])
    qkv = rms_matmul(x, 0, D_MODEL, g_mix, w_qkv.astype(BF16), tm=512, tn=1024, out_dtype=BF16,
                     mode="qkv", tab=tab)
    outs, lses = [], []
    for window, dilation in DIL_PATTERNS:
        o_g, lse_g = dilated_pattern(qkv, window, dilation, hw=512)
        outs.append(o_g)
        lses.append(lse_g)
    o = merge_patterns(outs, lses, tm=256)
    return matmul_residual(x, [(o, w_out.astype(BF16))], tm=512, tn=1024)


def kernel(x, norm_mix, norm_mlp, ev_w_in, mla_q_norm, mla_w_uq, mla_kv_norm, mla_w_ukv,
           mlstm_conv_w, mlstm_conv_b, mlstm_b_i, mlstm_b_f, ev_w_out, od_w_qkv, od_w_out,
           mlp_w1, mlp_w2, norm_final):
    assert x.shape == (BATCH, SEQ, D_MODEL) and x.dtype == F32
    small_tab, full_tab = _rope_tables()
    xt = x.reshape(TOKENS, D_MODEL)
    for layer in range(DEPTH):
        i = layer // 2
        if layer % 2 == 0:
            xt = _even_layer(xt, norm_mix[layer], ev_w_in[i], mla_q_norm[i], mla_w_uq[i], mla_kv_norm[i],
                             mla_w_ukv[i], mlstm_conv_w[i], mlstm_conv_b[i], mlstm_b_i[i], mlstm_b_f[i],
                             ev_w_out[i], small_tab)
        else:
            xt = _odd_layer(xt, norm_mix[layer], od_w_qkv[i], od_w_out[i], full_tab)
        xt = mlp_block(xt, norm_mlp[layer], mlp_w1[layer].astype(BF16), mlp_w2[layer].astype(BF16),
                       norm_final, tm=512, tf=1024, final_norm=(layer == DEPTH - 1))
    return xt.reshape(BATCH, SEQ, D_MODEL)
```

```python
import functools

import jax
import jax.numpy as jnp
import numpy as np
from jax import lax
from jax.experimental import pallas as pl
from jax.experimental.pallas import tpu as pltpu

D_MODEL = 2048
BATCH = 4
SEQ = 2048
DEPTH = 4
MLA_HEADS = 8
MLA_Q_LORA = 512
MLA_KV_LORA = 512
MLA_NOPE = 128
MLA_ROPE = 64
MLA_V = 128
MLA_QK = MLA_NOPE + MLA_ROPE
MLSTM_HEADS = 4
MLSTM_DK = 128
MLSTM_DV = 256
MLSTM_CHUNK = 128
CONV_WIDTH = 4
DIL_HEADS = 16
DIL_HEAD_DIM = 128
DIL_PATTERNS = ((128, 1), (512, 4), (2048, 16))
DIL_BLOCK = 128
D_FF = 4 * D_MODEL
ROPE_THETA = 10000.0
NORM_EPS = 1e-6
ODD_MIX = DIL_HEADS * DIL_HEAD_DIM
TOKENS = BATCH * SEQ

LANES = 128
VMEM_LIMIT_BYTES = 56 * 1024 * 1024

EV_CQ, EV_CKV, EV_MQK, EV_MV, EV_MO, EV_TAIL = 0, 512, 1024, 2048, 3072, 4096
EV_PROJ = 4224
TAIL_I = MLA_ROPE
TAIL_F = MLA_ROPE + MLSTM_HEADS

F32 = jnp.float32
BF16 = jnp.bfloat16
NEG_INF = float("-inf")


def _params(semantics):
    return pltpu.CompilerParams(dimension_semantics=semantics, vmem_limit_bytes=VMEM_LIMIT_BYTES)


def _rms_normalize(x, g):
    ms = jnp.mean(x * x, axis=-1, keepdims=True)
    return x * lax.rsqrt(ms + NORM_EPS) * g


def _dot(a, b):
    return jnp.dot(a, b, preferred_element_type=F32)


def _dot_nt(a, b):
    return lax.dot_general(a, b, (((1,), (1,)), ((), ())), preferred_element_type=F32)


def _dot_tn(a, b):
    return lax.dot_general(a, b, (((0,), (0,)), ((), ())), preferred_element_type=F32)


def _sigmoid(x):
    return 1.0 / (1.0 + jnp.exp(-x))


def _log_sigmoid(x):
    return jnp.minimum(x, 0.0) - jnp.log1p(jnp.exp(-jnp.abs(x)))


def _rope_small(y, tab):
    return (y * tab[:, 0:LANES] + pltpu.roll(y, 96, 1) * tab[:, LANES:2 * LANES]
            + pltpu.roll(y, 32, 1) * tab[:, 2 * LANES:3 * LANES])


def _rope_full(y, tab):
    return y * tab[:, 0:LANES] + pltpu.roll(y, 64, 1) * tab[:, LANES:2 * LANES]


def _norm_to_scratch(x_ref, g_ref, xn_ref):
    @pl.when(pl.program_id(1) == 0)
    def _():
        xn_ref[...] = _rms_normalize(x_ref[...], g_ref[...]).astype(BF16)


def _rms_mm_plain_kernel(x_ref, g_ref, w_ref, o_ref, xn_ref):
    _norm_to_scratch(x_ref, g_ref, xn_ref)
    o_ref[...] = _dot(xn_ref[...], w_ref[...]).astype(o_ref.dtype)


def _rms_mm_mlaq_kernel(x_ref, g_ref, w_ref, tab_ref, o_ref, xn_ref, *, scale):
    _norm_to_scratch(x_ref, g_ref, xn_ref)
    y = _dot(xn_ref[...], w_ref[...])
    tab = tab_ref[...]
    for grp in range(y.shape[1] // LANES):
        yg = y[:, grp * LANES:(grp + 1) * LANES]
        og = yg * scale if grp % 2 == 0 else _rope_small(yg, tab)
        o_ref[:, grp * LANES:(grp + 1) * LANES] = og.astype(o_ref.dtype)


def _qkv_proj_kernel(x_ref, g_ref, w_ref, tab_ref, *rest, dilations, tm, sub_cols):
    out_refs, (xn_ref, scr_ref) = rest[:len(dilations)], rest[len(dilations):]
    _norm_to_scratch(x_ref, g_ref, xn_ref)
    tab = tab_ref[...]
    nsub = w_ref.shape[1] // sub_cols

    def matmul(sub):
        return _dot(xn_ref[...], w_ref[:, sub * sub_cols:(sub + 1) * sub_cols])

    def epilogue(sub, y):
        for part in range(sub_cols // LANES):
            grp = sub * (sub_cols // LANES) + part
            cols = slice(grp * LANES, (grp + 1) * LANES)
            scr_ref[grp] = _rope_full(y[:, part * LANES:(part + 1) * LANES], tab)
            for d, o_ref in zip(dilations, out_refs):
                for r in range(d):
                    o_ref[r, :, cols] = scr_ref[grp, pl.ds(r, tm // d, stride=d), :].astype(o_ref.dtype)

    y = matmul(0)
    for sub in range(1, nsub):
        y_next = matmul(sub)
        epilogue(sub - 1, y)
        y = y_next
    epilogue(nsub - 1, y)


def qkv_projection(x, g, w, tab, dilations, *, tm, tn):
    t, k_dim = x.shape
    n = w.shape[1]
    tiles = SEQ // tm
    q_tiles = ODD_MIX // tn
    return pl.pallas_call(
        functools.partial(_qkv_proj_kernel, dilations=tuple(dilations), tm=tm, sub_cols=2 * LANES),
        out_shape=tuple(jax.ShapeDtypeStruct((BATCH, d, SEQ // d, n), BF16) for d in dilations),
        grid=(t // tm, n // tn),
        in_specs=[
            pl.BlockSpec((tm, k_dim), lambda i, j: (i, 0)),
            pl.BlockSpec((1, k_dim), lambda i, j: (0, 0)),
            pl.BlockSpec((k_dim, tn), lambda i, j: (0, j)),
            pl.BlockSpec((None, tm, 2 * LANES), lambda i, j: (j // q_tiles, i % tiles, 0)),
        ],
        out_specs=tuple(pl.BlockSpec((None, d, tm // d, tn), lambda i, j: (i // tiles, 0, i % tiles, j))
                        for d in dilations),
        scratch_shapes=[pltpu.VMEM((tm, k_dim), BF16), pltpu.VMEM((tn // LANES, tm, LANES), F32)],
        compiler_params=_params(("parallel", "arbitrary")),
        name="qkv_projection",
    )(x, g.reshape(1, k_dim), w, tab)


def rms_matmul(x, x_col_block, k_dim, g, w, *, tm, tn, out_dtype, mode="plain", tab=None, scale=None):
    t = x.shape[0]
    n = w.shape[1]
    assert t % tm == 0 and n % tn == 0 and w.shape[0] == k_dim
    grid = (t // tm, n // tn)
    pos_tiles = SEQ // tm
    in_specs = [
        pl.BlockSpec((tm, k_dim), lambda i, j: (i, x_col_block)),
        pl.BlockSpec((1, k_dim), lambda i, j: (0, 0)),
        pl.BlockSpec((k_dim, tn), lambda i, j: (0, j)),
    ]
    args = [x, g.reshape(1, k_dim), w]
    if mode == "plain":
        body = _rms_mm_plain_kernel
    elif mode == "mlaq":
        body = functools.partial(_rms_mm_mlaq_kernel, scale=scale)
        in_specs.append(pl.BlockSpec((tm, 3 * LANES), lambda i, j: (i % pos_tiles, 0)))
        args.append(tab)
    else:
        raise ValueError(mode)
    return pl.pallas_call(
        body,
        out_shape=jax.ShapeDtypeStruct((t, n), out_dtype),
        grid=grid,
        in_specs=in_specs,
        out_specs=pl.BlockSpec((tm, tn), lambda i, j: (i, j)),
        scratch_shapes=[pltpu.VMEM((tm, k_dim), BF16)],
        compiler_params=_params(("parallel", "arbitrary")),
        name="rms_matmul_" + mode,
    )(*args)


def _krope_kernel(p_ref, tab_ref, o_ref):
    o_ref[...] = _rope_small(p_ref[...], tab_ref[...]).astype(o_ref.dtype)


def krope(proj, tab, *, tm):
    pos_tiles = SEQ // tm
    return pl.pallas_call(
        _krope_kernel,
        out_shape=jax.ShapeDtypeStruct((TOKENS, LANES), BF16),
        grid=(TOKENS // tm,),
        in_specs=[pl.BlockSpec((tm, LANES), lambda i: (i, EV_TAIL // LANES)),
                  pl.BlockSpec((tm, 3 * LANES), lambda i: (i % pos_tiles, 0))],
        out_specs=pl.BlockSpec((tm, LANES), lambda i: (i, 0)),
        compiler_params=_params(("parallel",)),
        name="krope",
    )(proj, tab)


def _mm_res2_kernel(a1_ref, w1_ref, a2_ref, w2_ref, r_ref, o_ref):
    o_ref[...] = r_ref[...] + _dot(a1_ref[...], w1_ref[...]) + _dot(a2_ref[...], w2_ref[...])


def _mm_res1_kernel(a1_ref, w1_ref, r_ref, o_ref):
    o_ref[...] = r_ref[...] + _dot(a1_ref[...], w1_ref[...])


def matmul_residual(res, pairs, *, tm, tn):
    t, n = res.shape
    grid = (t // tm, n // tn)
    in_specs, args = [], []
    for a, w in pairs:
        k = a.shape[1]
        in_specs += [pl.BlockSpec((tm, k), lambda i, j: (i, 0)), pl.BlockSpec((k, tn), lambda i, j: (0, j))]
        args += [a, w]
    in_specs.append(pl.BlockSpec((tm, tn), lambda i, j: (i, j)))
    args.append(res)
    body = _mm_res2_kernel if len(pairs) == 2 else _mm_res1_kernel
    return pl.pallas_call(
        body,
        out_shape=jax.ShapeDtypeStruct((t, n), F32),
        grid=grid,
        in_specs=in_specs,
        out_specs=pl.BlockSpec((tm, tn), lambda i, j: (i, j)),
        compiler_params=_params(("parallel", "arbitrary")),
        name="matmul_residual",
    )(*args)


def _mlp_kernel(x_ref, g_ref, w1_ref, w2_ref, gf_ref, o_ref, xn_ref, *, final_norm):
    f = pl.program_id(1)

    @pl.when(f == 0)
    def _():
        x = x_ref[...]
        xn_ref[...] = _rms_normalize(x, g_ref[...]).astype(BF16)
        o_ref[...] = x

    h = jnp.maximum(_dot(xn_ref[...], w1_ref[...]), 0.0)
    o_ref[...] += _dot((h * h).astype(BF16), w2_ref[...])

    if final_norm:
        @pl.when(f == pl.num_programs(1) - 1)
        def _():
            o_ref[...] = _rms_normalize(o_ref[...], gf_ref[...])


def mlp_block(x, g, w1, w2, gf, *, tm, tf, final_norm):
    t, d = x.shape
    dff = w1.shape[1]
    return pl.pallas_call(
        functools.partial(_mlp_kernel, final_norm=final_norm),
        out_shape=jax.ShapeDtypeStruct((t, d), F32),
        grid=(t // tm, dff // tf),
        in_specs=[
            pl.BlockSpec((tm, d), lambda i, f: (i, 0)),
            pl.BlockSpec((1, d), lambda i, f: (0, 0)),
            pl.BlockSpec((d, tf), lambda i, f: (0, f)),
            pl.BlockSpec((tf, d), lambda i, f: (f, 0)),
            pl.BlockSpec((1, d), lambda i, f: (0, 0)),
        ],
        out_specs=pl.BlockSpec((tm, d), lambda i, f: (i, 0)),
        scratch_shapes=[pltpu.VMEM((tm, d), BF16)],
        compiler_params=_params(("parallel", "arbitrary")),
        name="mlp_block",
    )(x, g.reshape(1, d), w1, w2, gf.reshape(1, d))


def _mla_attn_kernel(q_ref, kv_ref, kr_ref, o_ref, m_ref, l_ref, acc_ref, *, tq, heads):
    qi = pl.program_id(2)
    hw = 2 * LANES
    m_ref[...] = jnp.full(m_ref.shape, NEG_INF, F32)
    l_ref[...] = jnp.zeros(l_ref.shape, F32)
    acc_ref[...] = jnp.zeros(acc_ref.shape, F32)

    def block(ki, masked):
        start = pl.multiple_of(ki * tq, tq)
        kr = kr_ref[pl.ds(start, tq), :]
        for h in range(heads):
            q = q_ref[:, h * hw:(h + 1) * hw]
            kn = kv_ref[pl.ds(start, tq), h * hw:h * hw + MLA_NOPE]
            v = kv_ref[pl.ds(start, tq), h * hw + MLA_NOPE:(h + 1) * hw]
            s = _dot_nt(q, jnp.concatenate([kn, kr], axis=-1))
            if masked:
                row = lax.broadcasted_iota(jnp.int32, s.shape, 0)
                col = lax.broadcasted_iota(jnp.int32, s.shape, 1)
                s = jnp.where(col <= row, s, NEG_INF)
            m_prev = m_ref[h]
            m_new = jnp.maximum(m_prev, jnp.max(s, axis=-1, keepdims=True))
            p = jnp.exp(s - pltpu.repeat(m_new, tq // LANES, axis=1))
            alpha = jnp.exp(m_prev - m_new)
            l_ref[h] = alpha * l_ref[h] + jnp.sum(p, axis=-1, keepdims=True)
            acc_ref[h] = alpha * acc_ref[h] + _dot(p.astype(BF16), v)
            m_ref[h] = m_new

    def body(ki, carry):
        block(ki, False)
        return carry

    lax.fori_loop(0, qi, body, 0)
    block(qi, True)
    for h in range(heads):
        o_ref[:, h * MLA_V:(h + 1) * MLA_V] = (acc_ref[h] / l_ref[h]).astype(o_ref.dtype)


def mla_attention(q, kv, kr, *, tq, heads):
    nq = SEQ // tq
    hw = 2 * LANES * heads
    return pl.pallas_call(
        functools.partial(_mla_attn_kernel, tq=tq, heads=heads),
        out_shape=jax.ShapeDtypeStruct((TOKENS, MLA_HEADS * MLA_V), BF16),
        grid=(BATCH, MLA_HEADS // heads, nq),
        in_specs=[
            pl.BlockSpec((tq, hw), lambda b, h, i: (b * nq + i, h)),
            pl.BlockSpec((SEQ, hw), lambda b, h, i: (b, h)),
            pl.BlockSpec((SEQ, LANES), lambda b, h, i: (b, 0)),
        ],
        out_specs=pl.BlockSpec((tq, heads * MLA_V), lambda b, h, i: (b * nq + i, h)),
        scratch_shapes=[pltpu.VMEM((heads, tq, LANES), F32), pltpu.VMEM((heads, tq, LANES), F32),
                        pltpu.VMEM((heads, tq, MLA_V), F32)],
        compiler_params=_params(("parallel", "parallel", "arbitrary")),
        name="mla_attention",
    )(q, kv, kr)


def _mlstm_kernel(qk_ref, v_ref, og_ref, gate_ref, cw_ref, cb_ref, gb_ref, out_ref,
                  hist_ref, c_ref, n_ref, m_ref):
    chunk = MLSTM_CHUNK
    c = pl.program_id(1)

    @pl.when(c == 0)
    def _():
        hist_ref[0:8, :] = jnp.zeros((8, hist_ref.shape[1]), F32)
        c_ref[...] = jnp.zeros(c_ref.shape, F32)
        n_ref[...] = jnp.zeros(n_ref.shape, F32)
        m_ref[...] = jnp.zeros(m_ref.shape, F32)

    hist_ref[8:8 + chunk, :] = qk_ref[...]
    y = jnp.broadcast_to(cb_ref[...], (chunk, hist_ref.shape[1]))
    for j in range(CONV_WIDTH):
        off = 8 - (CONV_WIDTH - 1) + j
        y = y + cw_ref[j:j + 1, :] * hist_ref[off:off + chunk, :]
    qk = y * _sigmoid(y)
    hist_ref[0:8, :] = hist_ref[chunk:chunk + 8, :]

    gates = gate_ref[...] + gb_ref[...]
    logf = _log_sigmoid(gates)
    row = lax.broadcasted_iota(jnp.int32, (chunk, chunk), 0)
    col = lax.broadcasted_iota(jnp.int32, (chunk, chunk), 1)
    tril = col <= row
    bcum = jnp.dot(tril.astype(F32), logf, preferred_element_type=F32,
                   precision=lax.Precision.HIGHEST)
    gates_t = gates.T
    bcum_t = bcum.T

    dk, dv = MLSTM_DK, MLSTM_DV
    for h in range(MLSTM_HEADS):
        q = qk[:, h * dk:(h + 1) * dk]
        k = qk[:, (MLSTM_HEADS + h) * dk:(MLSTM_HEADS + h + 1) * dk] * (dk ** -0.5)
        v = v_ref[:, h * dv:(h + 1) * dv].astype(BF16)
        b_c = bcum[:, TAIL_F + h:TAIL_F + h + 1]
        b_r = bcum_t[TAIL_F + h:TAIL_F + h + 1, :]
        i_c = gates[:, TAIL_I + h:TAIL_I + h + 1]
        i_r = gates_t[TAIL_I + h:TAIL_I + h + 1, :]
        b_last = bcum[chunk - 1:chunk, TAIL_F + h:TAIL_F + h + 1]
        m_prev = m_ref[h:h + 1, 0:1]
        n_prev = n_ref[h:h + 1, :]
        c_prev = c_ref[h]

        dmat = jnp.where(tril, b_c - b_r + i_r, NEG_INF)
        m_inter = b_c + m_prev
        m_row = jnp.maximum(m_inter, jnp.max(dmat, axis=-1, keepdims=True))
        q_b = q.astype(BF16)
        wmat = jnp.exp(dmat - m_row) * _dot_nt(q_b, k.astype(BF16))
        inter = jnp.exp(m_inter - m_row)
        num = _dot(wmat.astype(BF16), v) + inter * _dot(q_b, c_prev.astype(BF16))
        den = jnp.sum(wmat, axis=-1, keepdims=True) + inter * jnp.sum(q * n_prev, axis=-1, keepdims=True)
        hcell = num / jnp.maximum(jnp.abs(den), jnp.exp(-m_row))
        gate_o = _sigmoid(og_ref[:, h * dv:(h + 1) * dv])
        out_ref[:, h * dv:(h + 1) * dv] = (gate_o * hcell).astype(out_ref.dtype)

        g_c = b_last - b_c + i_c
        m_new = jnp.maximum(b_last + m_prev, jnp.max(g_c, axis=0, keepdims=True))
        wk = jnp.exp(g_c - m_new)
        decay = jnp.exp(b_last + m_prev - m_new)
        kw = k * wk
        c_ref[h] = decay * c_prev + _dot_tn(kw.astype(BF16), v)
        n_ref[h:h + 1, :] = decay * n_prev + jnp.sum(kw, axis=0, keepdims=True)
        m_ref[h:h + 1, :] = jnp.broadcast_to(m_new, (1, LANES))


def mlstm(proj, conv_w, conv_b, gate_bias):
    nc = SEQ // MLSTM_CHUNK
    chunk = MLSTM_CHUNK
    wqk = 2 * MLSTM_HEADS * MLSTM_DK
    wv = MLSTM_HEADS * MLSTM_DV
    row = lambda b, c: b * nc + c
    return pl.pallas_call(
        _mlstm_kernel,
        out_shape=jax.ShapeDtypeStruct((TOKENS, wv), BF16),
        grid=(BATCH, nc),
        in_specs=[
            pl.BlockSpec((chunk, wqk), lambda b, c: (row(b, c), EV_MQK // wqk)),
            pl.BlockSpec((chunk, wv), lambda b, c: (row(b, c), EV_MV // wv)),
            pl.BlockSpec((chunk, wv), lambda b, c: (row(b, c), EV_MO // wv)),
            pl.BlockSpec((chunk, LANES), lambda b, c: (row(b, c), EV_TAIL // LANES)),
            pl.BlockSpec((CONV_WIDTH, wqk), lambda b, c: (0, 0)),
            pl.BlockSpec((1, wqk), lambda b, c: (0, 0)),
            pl.BlockSpec((1, LANES), lambda b, c: (0, 0)),
        ],
        out_specs=pl.BlockSpec((chunk, wv), lambda b, c: (row(b, c), 0)),
        scratch_shapes=[
            pltpu.VMEM((chunk + 8, wqk), F32),
            pltpu.VMEM((MLSTM_HEADS, MLSTM_DK, MLSTM_DV), F32),
            pltpu.VMEM((8, MLSTM_DK), F32),
            pltpu.VMEM((8, LANES), F32),
        ],
        compiler_params=_params(("parallel", "arbitrary")),
        name="mlstm",
    )(proj, proj, proj, proj, conv_w, conv_b.reshape(1, wqk), gate_bias)


def _dilated_kernel(*refs, nblk, has_prev, span, heads):
    if has_prev:
        q_ref, kp_ref, kc_ref, vp_ref, vc_ref, o_ref, lse_ref = refs
    else:
        q_ref, kc_ref, vc_ref, o_ref, lse_ref = refs
        kp_ref = vp_ref = None
    n = pl.program_id(2)
    hg = pl.program_id(3)
    blk_sz = DIL_BLOCK
    dh = DIL_HEAD_DIM
    nkeys = 2 * blk_sz if has_prev else blk_sz
    qi = lax.broadcasted_iota(jnp.int32, (blk_sz, nkeys), 0)
    kj = lax.broadcasted_iota(jnp.int32, (blk_sz, nkeys), 1)
    dist = (nkeys - blk_sz) + qi - kj
    bias_full = jnp.where(dist >= 0, jnp.where(dist <= span, 0.0, NEG_INF), NEG_INF)
    bias_first = jnp.where(kj >= nkeys - blk_sz, bias_full, NEG_INF)
    lane = lax.broadcasted_iota(jnp.int32, (blk_sz, LANES), 1)

    @pl.when(hg == 0)
    def _():
        lse_ref[...] = jnp.zeros(lse_ref.shape, F32)

    def stack(ref, rows):
        return jnp.stack([ref[rows, h * dh:(h + 1) * dh] for h in range(heads)])

    for blk in range(nblk):
        rows = slice(blk * blk_sz, (blk + 1) * blk_sz)
        q3 = stack(q_ref, rows)
        k3 = stack(kc_ref, rows)
        v3 = stack(vc_ref, rows)
        bias = bias_full
        if has_prev:
            if blk == 0:
                first = slice(0, blk_sz)
                kp3, vp3 = stack(kp_ref, first), stack(vp_ref, first)
                bias = jnp.where(n > 0, bias_full, bias_first)
            else:
                prows = slice((blk - 1) * blk_sz, blk * blk_sz)
                kp3, vp3 = stack(kc_ref, prows), stack(vc_ref, prows)
            k3 = jnp.concatenate([kp3, k3], axis=1)
            v3 = jnp.concatenate([vp3, v3], axis=1)
        s = jnp.einsum("hqd,hkd->hqk", q3, k3, preferred_element_type=F32) + bias[None]
        m = jnp.max(s, axis=-1, keepdims=True)
        p = jnp.exp(s - m)
        den = jnp.sum(p, axis=-1, keepdims=True)
        o = jnp.einsum("hqk,hkd->hqd", p.astype(BF16), v3, preferred_element_type=F32) / den
        lse = m + jnp.log(den)
        lse_tile = lse_ref[rows, :]
        for h in range(heads):
            o_ref[rows, h * dh:(h + 1) * dh] = o[h]
            lse_tile = jnp.where(lane == hg * heads + h, lse[h], lse_tile)
        lse_ref[rows, :] = lse_tile


def dilated_pattern(qkv_perm, window, dilation, *, hw):
    span = window // dilation
    length = SEQ // dilation
    tl = min(length, 4 * DIL_BLOCK)
    nblk = tl // DIL_BLOCK
    ntile = length // tl
    has_prev = length > DIL_BLOCK
    heads = hw // DIL_HEAD_DIM
    ngroups = ODD_MIX // hw

    def cur(which):
        return pl.BlockSpec((None, None, tl, hw), lambda b, r, n, g: (b, r, n, which * ngroups + g))

    def prev(which):
        return pl.BlockSpec((None, None, DIL_BLOCK, hw),
                            lambda b, r, n, g: (b, r, jnp.maximum(n * nblk - 1, 0), which * ngroups + g))

    if has_prev:
        in_specs = [cur(0), prev(1), cur(1), prev(2), cur(2)]
    else:
        in_specs = [cur(0), cur(1), cur(2)]
    return pl.pallas_call(
        functools.partial(_dilated_kernel, nblk=nblk, has_prev=has_prev, span=span, heads=heads),
        out_shape=(jax.ShapeDtypeStruct((BATCH, dilation, length, ODD_MIX), F32),
                   jax.ShapeDtypeStruct((BATCH, dilation, length, LANES), F32)),
        grid=(BATCH, dilation, ntile, ngroups),
        in_specs=in_specs,
        out_specs=(pl.BlockSpec((None, None, tl, hw), lambda b, r, n, g: (b, r, n, g)),
                   pl.BlockSpec((None, None, tl, LANES), lambda b, r, n, g: (b, r, n, 0))),
        compiler_params=_params(("parallel", "parallel", "arbitrary", "arbitrary")),
        name="dilated_d%d" % dilation,
    )(*([qkv_perm] * len(in_specs)))


def _merge_kernel(*refs, dilations, tm):
    npat = len(dilations)
    o_refs, l_refs = refs[:npat], refs[npat:2 * npat]
    out_ref, lse_scr, w_scr, o_scr = refs[2 * npat:]
    dh = DIL_HEAD_DIM

    for g, d in enumerate(dilations):
        for r in range(d):
            lse_scr[g, pl.ds(r, tm // d, stride=d), :] = l_refs[g][r]
    lses = [lse_scr[g] for g in range(npat)]
    mx = functools.reduce(jnp.maximum, lses)
    es = [jnp.exp(l - mx) for l in lses]
    tot = functools.reduce(lambda a, b: a + b, es)
    for g in range(npat):
        w_scr[g] = es[g] / tot

    for h in range(DIL_HEADS):
        cols = slice(h * dh, (h + 1) * dh)
        acc = None
        for g, d in enumerate(dilations):
            rows = tm // d
            for r in range(d):
                w = w_scr[g, pl.ds(r, rows, stride=d), :][:, h:h + 1]
                o_scr[pl.ds(r, rows, stride=d), :] = w * o_refs[g][r, :, cols]
            term = o_scr[...]
            acc = term if acc is None else acc + term
        out_ref[:, cols] = acc.astype(out_ref.dtype)


def merge_patterns(outs, lses, dilations, *, tm):
    tiles = SEQ // tm

    def spec(d, width):
        return pl.BlockSpec((None, d, tm // d, width), lambda i: (i // tiles, 0, i % tiles, 0))

    return pl.pallas_call(
        functools.partial(_merge_kernel, dilations=tuple(dilations), tm=tm),
        out_shape=jax.ShapeDtypeStruct((TOKENS, ODD_MIX), BF16),
        grid=(TOKENS // tm,),
        in_specs=[spec(d, ODD_MIX) for d in dilations] + [spec(d, LANES) for d in dilations],
        out_specs=pl.BlockSpec((tm, ODD_MIX), lambda i: (i, 0)),
        scratch_shapes=[pltpu.VMEM((len(dilations), tm, LANES), F32),
                        pltpu.VMEM((len(dilations), tm, LANES), F32),
                        pltpu.VMEM((tm, DIL_HEAD_DIM), F32)],
        compiler_params=_params(("parallel",)),
        name="merge_patterns",
    )(*outs, *lses)


def _rope_tables():
    pos = jnp.arange(SEQ, dtype=F32)[:, None]
    half = MLA_ROPE // 2
    inv = ROPE_THETA ** (-jnp.arange(half, dtype=F32) * 2.0 / MLA_ROPE)
    ang = pos * inv[None, :]
    c, s = jnp.cos(ang), jnp.sin(ang)
    z = jnp.zeros_like(c)
    small = jnp.concatenate([c, c, z, z, -s, z, z, z, z, s, z, z], axis=-1)
    half = DIL_HEAD_DIM // 2
    inv = ROPE_THETA ** (-jnp.arange(half, dtype=F32) * 2.0 / DIL_HEAD_DIM)
    ang = pos * inv[None, :]
    c, s = jnp.cos(ang), jnp.sin(ang)
    full = jnp.concatenate([c, c, -s, s], axis=-1)
    return small, full


def _even_weights(w_in, w_uq, w_ukv, b_i, b_f):
    cuts = np.cumsum((MLA_Q_LORA, MLA_KV_LORA, MLA_ROPE, 2 * MLSTM_HEADS * MLSTM_DK,
                      MLSTM_HEADS * MLSTM_DV, MLSTM_HEADS, MLSTM_HEADS))
    c_q, c_kv, k_r, m_qk, m_v, m_i, m_f, m_o = jnp.split(w_in, cuts.tolist(), axis=1)
    pad = jnp.zeros((D_MODEL, EV_PROJ - EV_TAIL - MLA_ROPE - 2 * MLSTM_HEADS), w_in.dtype)
    w_in_r = jnp.concatenate([c_q, c_kv, m_qk, m_v, m_o, k_r, m_i, m_f, pad], axis=1).astype(BF16)
    uq = w_uq.reshape(MLA_Q_LORA, MLA_HEADS, MLA_QK)
    uq = jnp.pad(uq, ((0, 0), (0, 0), (0, 2 * LANES - MLA_QK)))
    w_uq_r = uq.reshape(MLA_Q_LORA, MLA_HEADS * 2 * LANES).astype(BF16)
    gate_bias = jnp.concatenate([jnp.zeros((TAIL_I,), F32), b_i.astype(F32), b_f.astype(F32),
                                 jnp.zeros((LANES - TAIL_F - MLSTM_HEADS,), F32)]).reshape(1, LANES)
    return w_in_r, w_uq_r, w_ukv.astype(BF16), gate_bias


def _even_layer(x, g_mix, w_in, q_norm, w_uq, kv_norm, w_ukv, conv_w, conv_b, b_i, b_f, w_out, small_tab):
    w_in_r, w_uq_r, w_ukv_r, gate_bias = _even_weights(w_in, w_uq, w_ukv, b_i, b_f)
    proj = rms_matmul(x, 0, D_MODEL, g_mix, w_in_r, tm=512, tn=EV_PROJ // 3, out_dtype=F32)
    scale = MLA_QK ** -0.5
    q = rms_matmul(proj, EV_CQ // MLA_Q_LORA, MLA_Q_LORA, q_norm, w_uq_r, tm=512, tn=2048,
                   out_dtype=BF16, mode="mlaq", tab=small_tab * scale, scale=scale)
    kv = rms_matmul(proj, EV_CKV // MLA_KV_LORA, MLA_KV_LORA, kv_norm, w_ukv_r, tm=512, tn=2048,
                    out_dtype=BF16)
    kr = krope(proj, small_tab, tm=1024)
    a_out = mla_attention(q, kv, kr, tq=512, heads=2)
    hm = mlstm(proj, conv_w, conv_b, gate_bias)
    w_out_b = w_out.astype(BF16)
    n_a = MLA_HEADS * MLA_V
    return matmul_residual(x, [(a_out, w_out_b[:n_a]), (hm, w_out_b[n_a:])], tm=512, tn=1024)


def _odd_layer(x, g_mix, w_qkv, w_out, full_tab):
    scale = DIL_HEAD_DIM ** -0.5
    identity = jnp.concatenate([jnp.ones((SEQ, LANES), F32), jnp.zeros((SEQ, LANES), F32)], axis=-1)
    tab = jnp.stack([full_tab * scale, full_tab, identity])
    dilations = [d for _, d in DIL_PATTERNS]
    qkv_perms = qkv_projection(x, g_mix, w_qkv.astype(BF16), tab, dilations, tm=512, tn=1024)
    outs, lses = [], []
    for (window, dilation), qkv_perm in zip(DIL_PATTERNS, qkv_perms):
        o_g, lse_g = dilated_pattern(qkv_perm, window, dilation, hw=1024)
        outs.append(o_g)
        lses.append(lse_g)
    o = merge_patterns(outs, lses, dilations, tm=512)
    return matmul_residual(x, [(o, w_out.astype(BF16))], tm=512, tn=1024)


def kernel(x, norm_mix, norm_mlp, ev_w_in, mla_q_norm, mla_w_uq, mla_kv_norm, mla_w_ukv,
           mlstm_conv_w, mlstm_conv_b, mlstm_b_i, mlstm_b_f, ev_w_out, od_w_qkv, od_w_out,
           mlp_w1, mlp_w2, norm_final):
    assert x.shape == (BATCH, SEQ, D_MODEL) and x.dtype == F32
    small_tab, full_tab = _rope_tables()
    xt = x.reshape(TOKENS, D_MODEL)
    for layer in range(DEPTH):
        i = layer // 2
        if layer % 2 == 0:
            xt = _even_layer(xt, norm_mix[layer], ev_w_in[i], mla_q_norm[i], mla_w_uq[i], mla_kv_norm[i],
                             mla_w_ukv[i], mlstm_conv_w[i], mlstm_conv_b[i], mlstm_b_i[i], mlstm_b_f[i],
                             ev_w_out[i], small_tab)
        else:
            xt = _odd_layer(xt, norm_mix[layer], od_w_qkv[i], od_w_out[i], full_tab)
        xt = mlp_block(xt, norm_mlp[layer], mlp_w1[layer].astype(BF16), mlp_w2[layer].astype(BF16),
                       norm_final, tm=512, tf=1024, final_norm=(layer == DEPTH - 1))
    return xt.reshape(BATCH, SEQ, D_MODEL)
```

```python
import functools

import jax
import jax.numpy as jnp
import numpy as np
from jax import lax
from jax.experimental import pallas as pl
from jax.experimental.pallas import tpu as pltpu

D_MODEL = 2048
BATCH = 4
SEQ = 2048
DEPTH = 4
MLA_HEADS = 8
MLA_Q_LORA = 512
MLA_KV_LORA = 512
MLA_NOPE = 128
MLA_ROPE = 64
MLA_V = 128
MLA_QK = MLA_NOPE + MLA_ROPE
MLSTM_HEADS = 4
MLSTM_DK = 128
MLSTM_DV = 256
MLSTM_CHUNK = 128
CONV_WIDTH = 4
DIL_HEADS = 16
DIL_HEAD_DIM = 128
DIL_PATTERNS = ((128, 1), (512, 4), (2048, 16))
DIL_BLOCK = 128
D_FF = 4 * D_MODEL
ROPE_THETA = 10000.0
NORM_EPS = 1e-6
ODD_MIX = DIL_HEADS * DIL_HEAD_DIM
TOKENS = BATCH * SEQ

LANES = 128
VMEM_LIMIT_BYTES = 56 * 1024 * 1024

EV_CQ, EV_CKV, EV_MQK, EV_MV, EV_MO, EV_TAIL = 0, 512, 1024, 2048, 3072, 4096
EV_PROJ = 4224
TAIL_I = MLA_ROPE
TAIL_F = MLA_ROPE + MLSTM_HEADS

F32 = jnp.float32
BF16 = jnp.bfloat16
NEG_INF = float("-inf")


def _params(semantics):
    return pltpu.CompilerParams(dimension_semantics=semantics, vmem_limit_bytes=VMEM_LIMIT_BYTES)


def _rms_normalize(x, g):
    ms = jnp.mean(x * x, axis=-1, keepdims=True)
    return x * lax.rsqrt(ms + NORM_EPS) * g


def _dot(a, b):
    return jnp.dot(a, b, preferred_element_type=F32)


def _dot_nt(a, b):
    return lax.dot_general(a, b, (((1,), (1,)), ((), ())), preferred_element_type=F32)


def _dot_tn(a, b):
    return lax.dot_general(a, b, (((0,), (0,)), ((), ())), preferred_element_type=F32)


def _sigmoid(x):
    return 1.0 / (1.0 + jnp.exp(-x))


def _log_sigmoid(x):
    return jnp.minimum(x, 0.0) - jnp.log1p(jnp.exp(-jnp.abs(x)))


def _rope_small(y, tab):
    return (y * tab[:, 0:LANES] + pltpu.roll(y, 96, 1) * tab[:, LANES:2 * LANES]
            + pltpu.roll(y, 32, 1) * tab[:, 2 * LANES:3 * LANES])


def _rope_full(y, tab):
    return y * tab[:, 0:LANES] + pltpu.roll(y, 64, 1) * tab[:, LANES:2 * LANES]


def _norm_to_scratch(x_ref, g_ref, xn_ref):
    @pl.when(pl.program_id(1) == 0)
    def _():
        xn_ref[...] = _rms_normalize(x_ref[...], g_ref[...]).astype(BF16)


def _rms_mm_plain_kernel(x_ref, g_ref, w_ref, o_ref, xn_ref):
    _norm_to_scratch(x_ref, g_ref, xn_ref)
    o_ref[...] = _dot(xn_ref[...], w_ref[...]).astype(o_ref.dtype)


def _rms_mm_mlaq_kernel(x_ref, g_ref, w_ref, tab_ref, o_ref, xn_ref, *, scale):
    _norm_to_scratch(x_ref, g_ref, xn_ref)
    y = _dot(xn_ref[...], w_ref[...])
    tab = tab_ref[...]
    for grp in range(y.shape[1] // LANES):
        yg = y[:, grp * LANES:(grp + 1) * LANES]
        og = yg * scale if grp % 2 == 0 else _rope_small(yg, tab)
        o_ref[:, grp * LANES:(grp + 1) * LANES] = og.astype(o_ref.dtype)


def _qkv_proj_kernel(xn_ref, w_ref, tab_ref, *rest, dilations, tm, sub_cols):
    out_refs, scr_ref = rest[:len(dilations)], rest[-1]
    tab = tab_ref[...]
    for sub in range(w_ref.shape[1] // sub_cols):
        y = _dot(xn_ref[...], w_ref[:, sub * sub_cols:(sub + 1) * sub_cols])
        for part in range(sub_cols // LANES):
            grp = sub * (sub_cols // LANES) + part
            cols = slice(grp * LANES, (grp + 1) * LANES)
            scr_ref[grp] = _rope_full(y[:, part * LANES:(part + 1) * LANES], tab)
            for d, o_ref in zip(dilations, out_refs):
                for r in range(d):
                    o_ref[r, :, cols] = scr_ref[grp, pl.ds(r, tm // d, stride=d), :].astype(o_ref.dtype)


def qkv_projection(xn, w, tab, dilations, *, tm, tn):
    t, k_dim = xn.shape
    n = w.shape[1]
    tiles = SEQ // tm
    q_tiles = ODD_MIX // tn
    return pl.pallas_call(
        functools.partial(_qkv_proj_kernel, dilations=tuple(dilations), tm=tm, sub_cols=2 * LANES),
        out_shape=tuple(jax.ShapeDtypeStruct((BATCH, d, SEQ // d, n), BF16) for d in dilations),
        grid=(n // tn, t // tm),
        in_specs=[
            pl.BlockSpec((tm, k_dim), lambda j, i: (i, 0)),
            pl.BlockSpec((k_dim, tn), lambda j, i: (0, j)),
            pl.BlockSpec((None, tm, 2 * LANES), lambda j, i: (j // q_tiles, i % tiles, 0)),
        ],
        out_specs=tuple(pl.BlockSpec((None, d, tm // d, tn), lambda j, i: (i // tiles, 0, i % tiles, j))
                        for d in dilations),
        scratch_shapes=[pltpu.VMEM((tn // LANES, tm, LANES), F32)],
        compiler_params=_params(("parallel", "arbitrary")),
        name="qkv_projection",
    )(xn, w, tab)


def _norm_kernel(x_ref, g_ref, o_ref):
    o_ref[...] = _rms_normalize(x_ref[...], g_ref[...]).astype(o_ref.dtype)


def rms_norm_bf16(x, g, *, tm):
    t, d = x.shape
    return pl.pallas_call(
        _norm_kernel,
        out_shape=jax.ShapeDtypeStruct((t, d), BF16),
        grid=(t // tm,),
        in_specs=[pl.BlockSpec((tm, d), lambda i: (i, 0)), pl.BlockSpec((1, d), lambda i: (0, 0))],
        out_specs=pl.BlockSpec((tm, d), lambda i: (i, 0)),
        compiler_params=_params(("parallel",)),
        name="rms_norm_bf16",
    )(x, g.reshape(1, d))


def _mm_kernel(a_ref, w_ref, o_ref):
    o_ref[...] = _dot(a_ref[...], w_ref[...]).astype(o_ref.dtype)


def matmul_weight_stationary(a, w, *, tm, tn, out_dtype):
    t, k = a.shape
    n = w.shape[1]
    return pl.pallas_call(
        _mm_kernel,
        out_shape=jax.ShapeDtypeStruct((t, n), out_dtype),
        grid=(n // tn, t // tm),
        in_specs=[pl.BlockSpec((tm, k), lambda j, i: (i, 0)), pl.BlockSpec((k, tn), lambda j, i: (0, j))],
        out_specs=pl.BlockSpec((tm, tn), lambda j, i: (i, j)),
        compiler_params=_params(("parallel", "arbitrary")),
        name="matmul_weight_stationary",
    )(a, w)


def rms_matmul(x, x_col_block, k_dim, g, w, *, tm, tn, out_dtype, mode="plain", tab=None, scale=None):
    t = x.shape[0]
    n = w.shape[1]
    assert t % tm == 0 and n % tn == 0 and w.shape[0] == k_dim
    grid = (t // tm, n // tn)
    pos_tiles = SEQ // tm
    in_specs = [
        pl.BlockSpec((tm, k_dim), lambda i, j: (i, x_col_block)),
        pl.BlockSpec((1, k_dim), lambda i, j: (0, 0)),
        pl.BlockSpec((k_dim, tn), lambda i, j: (0, j)),
    ]
    args = [x, g.reshape(1, k_dim), w]
    if mode == "plain":
        body = _rms_mm_plain_kernel
    elif mode == "mlaq":
        body = functools.partial(_rms_mm_mlaq_kernel, scale=scale)
        in_specs.append(pl.BlockSpec((tm, 3 * LANES), lambda i, j: (i % pos_tiles, 0)))
        args.append(tab)
    else:
        raise ValueError(mode)
    return pl.pallas_call(
        body,
        out_shape=jax.ShapeDtypeStruct((t, n), out_dtype),
        grid=grid,
        in_specs=in_specs,
        out_specs=pl.BlockSpec((tm, tn), lambda i, j: (i, j)),
        scratch_shapes=[pltpu.VMEM((tm, k_dim), BF16)],
        compiler_params=_params(("parallel", "arbitrary")),
        name="rms_matmul_" + mode,
    )(*args)


def _krope_kernel(p_ref, tab_ref, o_ref):
    o_ref[...] = _rope_small(p_ref[...], tab_ref[...]).astype(o_ref.dtype)


def krope(proj, tab, *, tm):
    pos_tiles = SEQ // tm
    return pl.pallas_call(
        _krope_kernel,
        out_shape=jax.ShapeDtypeStruct((TOKENS, LANES), BF16),
        grid=(TOKENS // tm,),
        in_specs=[pl.BlockSpec((tm, LANES), lambda i: (i, EV_TAIL // LANES)),
                  pl.BlockSpec((tm, 3 * LANES), lambda i: (i % pos_tiles, 0))],
        out_specs=pl.BlockSpec((tm, LANES), lambda i: (i, 0)),
        compiler_params=_params(("parallel",)),
        name="krope",
    )(proj, tab)


def _mm_res_kernel(*refs):
    r_ref, o_ref = refs[-2], refs[-1]
    acc = r_ref[...]
    for k in range((len(refs) - 2) // 2):
        acc = acc + _dot(refs[2 * k][...], refs[2 * k + 1][...])
    o_ref[...] = acc


def matmul_residual(res, pairs, *, tm):
    t, n = res.shape
    in_specs, args = [], []
    for a, w in pairs:
        k = a.shape[1]
        in_specs += [pl.BlockSpec((tm, k), lambda i: (i, 0)), pl.BlockSpec((k, n), lambda i: (0, 0))]
        args += [a, w]
    in_specs.append(pl.BlockSpec((tm, n), lambda i: (i, 0)))
    args.append(res)
    return pl.pallas_call(
        _mm_res_kernel,
        out_shape=jax.ShapeDtypeStruct((t, n), F32),
        grid=(t // tm,),
        in_specs=in_specs,
        out_specs=pl.BlockSpec((tm, n), lambda i: (i, 0)),
        compiler_params=_params(("parallel",)),
        name="matmul_residual",
    )(*args)


def _mlp_kernel(x_ref, g_ref, w1_ref, w2_ref, gn_ref, *rest, tail):
    o_ref, xn_ref = rest[0], rest[-1]
    f = pl.program_id(1)

    @pl.when(f == 0)
    def _():
        x = x_ref[...]
        xn_ref[...] = _rms_normalize(x, g_ref[...]).astype(BF16)
        o_ref[...] = x

    h = jnp.maximum(_dot(xn_ref[...], w1_ref[...]), 0.0)
    o_ref[...] += _dot((h * h).astype(BF16), w2_ref[...])

    @pl.when(f == pl.num_programs(1) - 1)
    def _():
        normed = _rms_normalize(o_ref[...], gn_ref[...])
        if tail == "final":
            o_ref[...] = normed
        else:
            rest[1][...] = normed.astype(BF16)


def mlp_block(x, g, w1_all, w2_all, layer, gn, *, tm, tf, tail):
    t, d = x.shape
    dff = w1_all.shape[2]
    out_shape = [jax.ShapeDtypeStruct((t, d), F32)]
    out_specs = [pl.BlockSpec((tm, d), lambda i, f: (i, 0))]
    if tail == "next":
        out_shape.append(jax.ShapeDtypeStruct((t, d), BF16))
        out_specs.append(pl.BlockSpec((tm, d), lambda i, f: (i, 0)))
    return pl.pallas_call(
        functools.partial(_mlp_kernel, tail=tail),
        out_shape=tuple(out_shape),
        grid=(t // tm, dff // tf),
        in_specs=[
            pl.BlockSpec((tm, d), lambda i, f: (i, 0)),
            pl.BlockSpec((1, d), lambda i, f: (0, 0)),
            pl.BlockSpec((None, d, tf), lambda i, f: (layer, 0, f)),
            pl.BlockSpec((None, tf, d), lambda i, f: (layer, f, 0)),
            pl.BlockSpec((1, d), lambda i, f: (0, 0)),
        ],
        out_specs=tuple(out_specs),
        scratch_shapes=[pltpu.VMEM((tm, d), BF16)],
        compiler_params=_params(("parallel", "arbitrary")),
        name="mlp_block",
    )(x, g.reshape(1, d), w1_all, w2_all, gn.reshape(1, d))


def _mla_attn_kernel(q_ref, kv_ref, kr_ref, o_ref, m_ref, l_ref, acc_ref, *, tq, heads):
    qi = pl.program_id(2)
    hw = 2 * LANES
    m_ref[...] = jnp.full(m_ref.shape, NEG_INF, F32)
    l_ref[...] = jnp.zeros(l_ref.shape, F32)
    acc_ref[...] = jnp.zeros(acc_ref.shape, F32)

    def block(ki, masked):
        start = pl.multiple_of(ki * tq, tq)
        kr = kr_ref[pl.ds(start, tq), :]
        for h in range(heads):
            q = q_ref[:, h * hw:(h + 1) * hw]
            kn = kv_ref[pl.ds(start, tq), h * hw:h * hw + MLA_NOPE]
            v = kv_ref[pl.ds(start, tq), h * hw + MLA_NOPE:(h + 1) * hw]
            s = _dot_nt(q, jnp.concatenate([kn, kr], axis=-1))
            if masked:
                row = lax.broadcasted_iota(jnp.int32, s.shape, 0)
                col = lax.broadcasted_iota(jnp.int32, s.shape, 1)
                s = jnp.where(col <= row, s, NEG_INF)
            m_prev = m_ref[h]
            m_new = jnp.maximum(m_prev, jnp.max(s, axis=-1, keepdims=True))
            p = jnp.exp(s - pltpu.repeat(m_new, tq // LANES, axis=1))
            alpha = jnp.exp(m_prev - m_new)
            l_ref[h] = alpha * l_ref[h] + jnp.sum(p, axis=-1, keepdims=True)
            acc_ref[h] = alpha * acc_ref[h] + _dot(p.astype(BF16), v)
            m_ref[h] = m_new

    def body(ki, carry):
        block(ki, False)
        return carry

    lax.fori_loop(0, qi, body, 0)
    block(qi, True)
    for h in range(heads):
        o_ref[:, h * MLA_V:(h + 1) * MLA_V] = (acc_ref[h] / l_ref[h]).astype(o_ref.dtype)


def mla_attention(q, kv, kr, *, tq, heads):
    nq = SEQ // tq
    hw = 2 * LANES * heads
    return pl.pallas_call(
        functools.partial(_mla_attn_kernel, tq=tq, heads=heads),
        out_shape=jax.ShapeDtypeStruct((TOKENS, MLA_HEADS * MLA_V), BF16),
        grid=(BATCH, MLA_HEADS // heads, nq),
        in_specs=[
            pl.BlockSpec((tq, hw), lambda b, h, i: (b * nq + i, h)),
            pl.BlockSpec((SEQ, hw), lambda b, h, i: (b, h)),
            pl.BlockSpec((SEQ, LANES), lambda b, h, i: (b, 0)),
        ],
        out_specs=pl.BlockSpec((tq, heads * MLA_V), lambda b, h, i: (b * nq + i, h)),
        scratch_shapes=[pltpu.VMEM((heads, tq, LANES), F32), pltpu.VMEM((heads, tq, LANES), F32),
                        pltpu.VMEM((heads, tq, MLA_V), F32)],
        compiler_params=_params(("parallel", "parallel", "arbitrary")),
        name="mla_attention",
    )(q, kv, kr)


def _mlstm_kernel(qk_ref, v_ref, og_ref, gate_ref, cw_ref, cb_ref, gb_ref, out_ref,
                  hist_ref, c_ref, n_ref, m_ref):
    chunk = MLSTM_CHUNK
    c = pl.program_id(1)

    @pl.when(c == 0)
    def _():
        hist_ref[0:8, :] = jnp.zeros((8, hist_ref.shape[1]), F32)
        c_ref[...] = jnp.zeros(c_ref.shape, F32)
        n_ref[...] = jnp.zeros(n_ref.shape, F32)
        m_ref[...] = jnp.zeros(m_ref.shape, F32)

    hist_ref[8:8 + chunk, :] = qk_ref[...]
    y = jnp.broadcast_to(cb_ref[...], (chunk, hist_ref.shape[1]))
    for j in range(CONV_WIDTH):
        off = 8 - (CONV_WIDTH - 1) + j
        y = y + cw_ref[j:j + 1, :] * hist_ref[off:off + chunk, :]
    qk = y * _sigmoid(y)
    hist_ref[0:8, :] = hist_ref[chunk:chunk + 8, :]

    gates = gate_ref[...] + gb_ref[...]
    logf = _log_sigmoid(gates)
    row = lax.broadcasted_iota(jnp.int32, (chunk, chunk), 0)
    col = lax.broadcasted_iota(jnp.int32, (chunk, chunk), 1)
    tril = col <= row
    bcum = jnp.dot(tril.astype(F32), logf, preferred_element_type=F32,
                   precision=lax.Precision.HIGHEST)
    gates_t = gates.T
    bcum_t = bcum.T

    dk, dv = MLSTM_DK, MLSTM_DV
    for h in range(MLSTM_HEADS):
        q = qk[:, h * dk:(h + 1) * dk]
        k = qk[:, (MLSTM_HEADS + h) * dk:(MLSTM_HEADS + h + 1) * dk] * (dk ** -0.5)
        v = v_ref[:, h * dv:(h + 1) * dv].astype(BF16)
        b_c = bcum[:, TAIL_F + h:TAIL_F + h + 1]
        b_r = bcum_t[TAIL_F + h:TAIL_F + h + 1, :]
        i_c = gates[:, TAIL_I + h:TAIL_I + h + 1]
        i_r = gates_t[TAIL_I + h:TAIL_I + h + 1, :]
        b_last = bcum[chunk - 1:chunk, TAIL_F + h:TAIL_F + h + 1]
        m_prev = m_ref[h:h + 1, 0:1]
        n_prev = n_ref[h:h + 1, :]
        c_prev = c_ref[h]

        dmat = jnp.where(tril, b_c - b_r + i_r, NEG_INF)
        m_inter = b_c + m_prev
        m_row = jnp.maximum(m_inter, jnp.max(dmat, axis=-1, keepdims=True))
        q_b = q.astype(BF16)
        wmat = jnp.exp(dmat - m_row) * _dot_nt(q_b, k.astype(BF16))
        inter = jnp.exp(m_inter - m_row)
        num = _dot(wmat.astype(BF16), v) + inter * _dot(q_b, c_prev.astype(BF16))
        den = jnp.sum(wmat, axis=-1, keepdims=True) + inter * jnp.sum(q * n_prev, axis=-1, keepdims=True)
        hcell = num / jnp.maximum(jnp.abs(den), jnp.exp(-m_row))
        gate_o = _sigmoid(og_ref[:, h * dv:(h + 1) * dv])
        out_ref[:, h * dv:(h + 1) * dv] = (gate_o * hcell).astype(out_ref.dtype)

        g_c = b_last - b_c + i_c
        m_new = jnp.maximum(b_last + m_prev, jnp.max(g_c, axis=0, keepdims=True))
        wk = jnp.exp(g_c - m_new)
        decay = jnp.exp(b_last + m_prev - m_new)
        kw = k * wk
        c_ref[h] = decay * c_prev + _dot_tn(kw.astype(BF16), v)
        n_ref[h:h + 1, :] = decay * n_prev + jnp.sum(kw, axis=0, keepdims=True)
        m_ref[h:h + 1, :] = jnp.broadcast_to(m_new, (1, LANES))


def mlstm(proj, conv_w, conv_b, gate_bias):
    nc = SEQ // MLSTM_CHUNK
    chunk = MLSTM_CHUNK
    wqk = 2 * MLSTM_HEADS * MLSTM_DK
    wv = MLSTM_HEADS * MLSTM_DV
    row = lambda b, c: b * nc + c
    return pl.pallas_call(
        _mlstm_kernel,
        out_shape=jax.ShapeDtypeStruct((TOKENS, wv), BF16),
        grid=(BATCH, nc),
        in_specs=[
            pl.BlockSpec((chunk, wqk), lambda b, c: (row(b, c), EV_MQK // wqk)),
            pl.BlockSpec((chunk, wv), lambda b, c: (row(b, c), EV_MV // wv)),
            pl.BlockSpec((chunk, wv), lambda b, c: (row(b, c), EV_MO // wv)),
            pl.BlockSpec((chunk, LANES), lambda b, c: (row(b, c), EV_TAIL // LANES)),
            pl.BlockSpec((CONV_WIDTH, wqk), lambda b, c: (0, 0)),
            pl.BlockSpec((1, wqk), lambda b, c: (0, 0)),
            pl.BlockSpec((1, LANES), lambda b, c: (0, 0)),
        ],
        out_specs=pl.BlockSpec((chunk, wv), lambda b, c: (row(b, c), 0)),
        scratch_shapes=[
            pltpu.VMEM((chunk + 8, wqk), F32),
            pltpu.VMEM((MLSTM_HEADS, MLSTM_DK, MLSTM_DV), F32),
            pltpu.VMEM((8, MLSTM_DK), F32),
            pltpu.VMEM((8, LANES), F32),
        ],
        compiler_params=_params(("parallel", "arbitrary")),
        name="mlstm",
    )(proj, proj, proj, proj, conv_w, conv_b.reshape(1, wqk), gate_bias)


def _dilated_kernel(*refs, nres, nblk, has_prev, span, heads):
    if has_prev:
        q_ref, kp_ref, kc_ref, vp_ref, vc_ref, o_ref, lse_ref = refs
    else:
        q_ref, kc_ref, vc_ref, o_ref, lse_ref = refs
        kp_ref = vp_ref = None
    n = pl.program_id(2)
    hg = pl.program_id(3)
    blk_sz = DIL_BLOCK
    dh = DIL_HEAD_DIM
    nkeys = 2 * blk_sz if has_prev else blk_sz
    qi = lax.broadcasted_iota(jnp.int32, (blk_sz, nkeys), 0)
    kj = lax.broadcasted_iota(jnp.int32, (blk_sz, nkeys), 1)
    dist = (nkeys - blk_sz) + qi - kj
    bias_full = jnp.where(dist >= 0, jnp.where(dist <= span, 0.0, NEG_INF), NEG_INF)
    bias_first = jnp.where(kj >= nkeys - blk_sz, bias_full, NEG_INF)
    lane = lax.broadcasted_iota(jnp.int32, (blk_sz, LANES), 1)

    @pl.when(hg == 0)
    def _():
        lse_ref[...] = jnp.zeros(lse_ref.shape, F32)

    def stack(ref, res, rows):
        return jnp.stack([ref[res, rows, h * dh:(h + 1) * dh] for h in range(heads)])

    for res in range(nres):
        for blk in range(nblk):
            rows = slice(blk * blk_sz, (blk + 1) * blk_sz)
            q3 = stack(q_ref, res, rows)
            k3 = stack(kc_ref, res, rows)
            v3 = stack(vc_ref, res, rows)
            bias = bias_full
            if has_prev:
                if blk == 0:
                    first = slice(0, blk_sz)
                    kp3, vp3 = stack(kp_ref, res, first), stack(vp_ref, res, first)
                    bias = jnp.where(n > 0, bias_full, bias_first)
                else:
                    prows = slice((blk - 1) * blk_sz, blk * blk_sz)
                    kp3, vp3 = stack(kc_ref, res, prows), stack(vc_ref, res, prows)
                k3 = jnp.concatenate([kp3, k3], axis=1)
                v3 = jnp.concatenate([vp3, v3], axis=1)
            s = jnp.einsum("hqd,hkd->hqk", q3, k3, preferred_element_type=F32) + bias[None]
            m = jnp.max(s, axis=-1, keepdims=True)
            p = jnp.exp(s - m)
            den = jnp.sum(p, axis=-1, keepdims=True)
            o = jnp.einsum("hqk,hkd->hqd", p.astype(BF16), v3, preferred_element_type=F32) / den
            lse = m + jnp.log(den)
            lse_tile = lse_ref[res, rows, :]
            for h in range(heads):
                o_ref[res, rows, h * dh:(h + 1) * dh] = o[h]
                lse_tile = jnp.where(lane == hg * heads + h, lse[h], lse_tile)
            lse_ref[res, rows, :] = lse_tile


def dilated_pattern(qkv_perm, window, dilation, *, hw, units):
    span = window // dilation
    length = SEQ // dilation
    nblk = min(length // DIL_BLOCK, units)
    nres = min(dilation, units // nblk)
    tl = nblk * DIL_BLOCK
    ntile = length // tl
    has_prev = length > DIL_BLOCK
    heads = hw // DIL_HEAD_DIM
    ngroups = ODD_MIX // hw

    def cur(which):
        return pl.BlockSpec((None, nres, tl, hw), lambda b, r, n, g: (b, r, n, which * ngroups + g))

    def prev(which):
        return pl.BlockSpec((None, nres, DIL_BLOCK, hw),
                            lambda b, r, n, g: (b, r, jnp.maximum(n * nblk - 1, 0), which * ngroups + g))

    if has_prev:
        in_specs = [cur(0), prev(1), cur(1), prev(2), cur(2)]
    else:
        in_specs = [cur(0), cur(1), cur(2)]
    return pl.pallas_call(
        functools.partial(_dilated_kernel, nres=nres, nblk=nblk, has_prev=has_prev, span=span, heads=heads),
        out_shape=(jax.ShapeDtypeStruct((BATCH, dilation, length, ODD_MIX), F32),
                   jax.ShapeDtypeStruct((BATCH, dilation, length, LANES), F32)),
        grid=(BATCH, dilation // nres, ntile, ngroups),
        in_specs=in_specs,
        out_specs=(pl.BlockSpec((None, nres, tl, hw), lambda b, r, n, g: (b, r, n, g)),
                   pl.BlockSpec((None, nres, tl, LANES), lambda b, r, n, g: (b, r, n, 0))),
        compiler_params=_params(("parallel", "parallel", "arbitrary", "arbitrary")),
        name="dilated_d%d" % dilation,
    )(*([qkv_perm] * len(in_specs)))


def _merge_kernel(*refs, dilations, tm):
    npat = len(dilations)
    o_refs, l_refs = refs[:npat], refs[npat:2 * npat]
    out_ref, lse_scr, w_scr, o_scr = refs[2 * npat:]
    dh = DIL_HEAD_DIM

    for g, d in enumerate(dilations):
        for r in range(d):
            lse_scr[g, pl.ds(r, tm // d, stride=d), :] = l_refs[g][r]
    lses = [lse_scr[g] for g in range(npat)]
    mx = functools.reduce(jnp.maximum, lses)
    es = [jnp.exp(l - mx) for l in lses]
    tot = functools.reduce(lambda a, b: a + b, es)
    for g in range(npat):
        w_scr[g] = es[g] / tot

    for h in range(DIL_HEADS):
        cols = slice(h * dh, (h + 1) * dh)
        acc = None
        for g, d in enumerate(dilations):
            rows = tm // d
            for r in range(d):
                w = w_scr[g, pl.ds(r, rows, stride=d), :][:, h:h + 1]
                o_scr[pl.ds(r, rows, stride=d), :] = w * o_refs[g][r, :, cols]
            term = o_scr[...]
            acc = term if acc is None else acc + term
        out_ref[:, cols] = acc.astype(out_ref.dtype)


def merge_patterns(outs, lses, dilations, *, tm):
    tiles = SEQ // tm

    def spec(d, width):
        return pl.BlockSpec((None, d, tm // d, width), lambda i: (i // tiles, 0, i % tiles, 0))

    return pl.pallas_call(
        functools.partial(_merge_kernel, dilations=tuple(dilations), tm=tm),
        out_shape=jax.ShapeDtypeStruct((TOKENS, ODD_MIX), BF16),
        grid=(TOKENS // tm,),
        in_specs=[spec(d, ODD_MIX) for d in dilations] + [spec(d, LANES) for d in dilations],
        out_specs=pl.BlockSpec((tm, ODD_MIX), lambda i: (i, 0)),
        scratch_shapes=[pltpu.VMEM((len(dilations), tm, LANES), F32),
                        pltpu.VMEM((len(dilations), tm, LANES), F32),
                        pltpu.VMEM((tm, DIL_HEAD_DIM), F32)],
        compiler_params=_params(("parallel",)),
        name="merge_patterns",
    )(*outs, *lses)


def _rope_tables():
    pos = jnp.arange(SEQ, dtype=F32)[:, None]
    half = MLA_ROPE // 2
    inv = ROPE_THETA ** (-jnp.arange(half, dtype=F32) * 2.0 / MLA_ROPE)
    ang = pos * inv[None, :]
    c, s = jnp.cos(ang), jnp.sin(ang)
    z = jnp.zeros_like(c)
    small = jnp.concatenate([c, c, z, z, -s, z, z, z, z, s, z, z], axis=-1)
    half = DIL_HEAD_DIM // 2
    inv = ROPE_THETA ** (-jnp.arange(half, dtype=F32) * 2.0 / DIL_HEAD_DIM)
    ang = pos * inv[None, :]
    c, s = jnp.cos(ang), jnp.sin(ang)
    full = jnp.concatenate([c, c, -s, s], axis=-1)
    return small, full


def _even_weights(w_in, w_uq, w_ukv, b_i, b_f):
    cuts = np.cumsum((MLA_Q_LORA, MLA_KV_LORA, MLA_ROPE, 2 * MLSTM_HEADS * MLSTM_DK,
                      MLSTM_HEADS * MLSTM_DV, MLSTM_HEADS, MLSTM_HEADS))
    c_q, c_kv, k_r, m_qk, m_v, m_i, m_f, m_o = jnp.split(w_in, cuts.tolist(), axis=1)
    pad = jnp.zeros((D_MODEL, EV_PROJ - EV_TAIL - MLA_ROPE - 2 * MLSTM_HEADS), w_in.dtype)
    w_in_r = jnp.concatenate([c_q, c_kv, m_qk, m_v, m_o, k_r, m_i, m_f, pad], axis=1).astype(BF16)
    uq = w_uq.reshape(MLA_Q_LORA, MLA_HEADS, MLA_QK)
    uq = jnp.pad(uq, ((0, 0), (0, 0), (0, 2 * LANES - MLA_QK)))
    w_uq_r = uq.reshape(MLA_Q_LORA, MLA_HEADS * 2 * LANES).astype(BF16)
    gate_bias = jnp.concatenate([jnp.zeros((TAIL_I,), F32), b_i.astype(F32), b_f.astype(F32),
                                 jnp.zeros((LANES - TAIL_F - MLSTM_HEADS,), F32)]).reshape(1, LANES)
    return w_in_r, w_uq_r, w_ukv.astype(BF16), gate_bias


def _even_layer(x, xn, w_in, q_norm, w_uq, kv_norm, w_ukv, conv_w, conv_b, b_i, b_f, w_out, small_tab):
    w_in_r, w_uq_r, w_ukv_r, gate_bias = _even_weights(w_in, w_uq, w_ukv, b_i, b_f)
    proj = matmul_weight_stationary(xn, w_in_r, tm=512, tn=EV_PROJ // 3, out_dtype=F32)
    scale = MLA_QK ** -0.5
    q = rms_matmul(proj, EV_CQ // MLA_Q_LORA, MLA_Q_LORA, q_norm, w_uq_r, tm=512, tn=2048,
                   out_dtype=BF16, mode="mlaq", tab=small_tab * scale, scale=scale)
    kv = rms_matmul(proj, EV_CKV // MLA_KV_LORA, MLA_KV_LORA, kv_norm, w_ukv_r, tm=512, tn=2048,
                    out_dtype=BF16)
    kr = krope(proj, small_tab, tm=1024)
    a_out = mla_attention(q, kv, kr, tq=512, heads=2)
    hm = mlstm(proj, conv_w, conv_b, gate_bias)
    w_out_b = w_out.astype(BF16)
    n_a = MLA_HEADS * MLA_V
    return matmul_residual(x, [(a_out, w_out_b[:n_a]), (hm, w_out_b[n_a:])], tm=512)


def _odd_layer(x, xn, w_qkv, w_out, full_tab):
    scale = DIL_HEAD_DIM ** -0.5
    identity = jnp.concatenate([jnp.ones((SEQ, LANES), F32), jnp.zeros((SEQ, LANES), F32)], axis=-1)
    tab = jnp.stack([full_tab * scale, full_tab, identity])
    dilations = [d for _, d in DIL_PATTERNS]
    qkv_perms = qkv_projection(xn, w_qkv.astype(BF16), tab, dilations, tm=512, tn=1024)
    outs, lses = [], []
    for (window, dilation), qkv_perm in zip(DIL_PATTERNS, qkv_perms):
        o_g, lse_g = dilated_pattern(qkv_perm, window, dilation, hw=1024, units=4)
        outs.append(o_g)
        lses.append(lse_g)
    o = merge_patterns(outs, lses, dilations, tm=512)
    return matmul_residual(x, [(o, w_out.astype(BF16))], tm=512)


def kernel(x, norm_mix, norm_mlp, ev_w_in, mla_q_norm, mla_w_uq, mla_kv_norm, mla_w_ukv,
           mlstm_conv_w, mlstm_conv_b, mlstm_b_i, mlstm_b_f, ev_w_out, od_w_qkv, od_w_out,
           mlp_w1, mlp_w2, norm_final):
    assert x.shape == (BATCH, SEQ, D_MODEL) and x.dtype == F32
    small_tab, full_tab = _rope_tables()
    w1_all, w2_all = mlp_w1.astype(BF16), mlp_w2.astype(BF16)
    xt = x.reshape(TOKENS, D_MODEL)
    xn = rms_norm_bf16(xt, norm_mix[0], tm=512)
    for layer in range(DEPTH):
        i = layer // 2
        if layer % 2 == 0:
            xt = _even_layer(xt, xn, ev_w_in[i], mla_q_norm[i], mla_w_uq[i], mla_kv_norm[i],
                             mla_w_ukv[i], mlstm_conv_w[i], mlstm_conv_b[i], mlstm_b_i[i], mlstm_b_f[i],
                             ev_w_out[i], small_tab)
        else:
            xt = _odd_layer(xt, xn, od_w_qkv[i], od_w_out[i], full_tab)
        if layer == DEPTH - 1:
            (xt,) = mlp_block(xt, norm_mlp[layer], w1_all, w2_all, layer, norm_final, tm=512, tf=1024,
                              tail="final")
        else:
            xt, xn = mlp_block(xt, norm_mlp[layer], w1_all, w2_all, layer, norm_mix[layer + 1], tm=512,
                               tf=1024, tail="next")
    return xt.reshape(BATCH, SEQ, D_MODEL)
```

```python
import functools

import jax
import jax.numpy as jnp
import numpy as np
from jax import lax
from jax.experimental import pallas as pl
from jax.experimental.pallas import tpu as pltpu

D_MODEL = 2048
BATCH = 4
SEQ = 2048
DEPTH = 4
MLA_HEADS = 8
MLA_Q_LORA = 512
MLA_KV_LORA = 512
MLA_NOPE = 128
MLA_ROPE = 64
MLA_V = 128
MLA_QK = MLA_NOPE + MLA_ROPE
MLSTM_HEADS = 4
MLSTM_DK = 128
MLSTM_DV = 256
MLSTM_CHUNK = 128
CONV_WIDTH = 4
DIL_HEADS = 16
DIL_HEAD_DIM = 128
DIL_PATTERNS = ((128, 1), (512, 4), (2048, 16))
DIL_BLOCK = 128
D_FF = 4 * D_MODEL
ROPE_THETA = 10000.0
NORM_EPS = 1e-6
ODD_MIX = DIL_HEADS * DIL_HEAD_DIM
TOKENS = BATCH * SEQ

LANES = 128
VMEM_LIMIT_BYTES = 56 * 1024 * 1024

EV_CQ, EV_CKV, EV_MQK, EV_MV, EV_MO, EV_TAIL = 0, 512, 1024, 2048, 3072, 4096
EV_PROJ = 4224
TAIL_I = MLA_ROPE
TAIL_F = MLA_ROPE + MLSTM_HEADS

F32 = jnp.float32
BF16 = jnp.bfloat16
NEG_INF = float("-inf")
LOG2E = float(np.log2(np.e))
LN2 = float(np.log(2.0))


def _params(semantics):
    return pltpu.CompilerParams(dimension_semantics=semantics, vmem_limit_bytes=VMEM_LIMIT_BYTES)


def _rms_normalize(x, g):
    ms = jnp.mean(x * x, axis=-1, keepdims=True)
    return x * lax.rsqrt(ms + NORM_EPS) * g


def _dot(a, b):
    return jnp.dot(a, b, preferred_element_type=F32)


def _dot_nt(a, b):
    return lax.dot_general(a, b, (((1,), (1,)), ((), ())), preferred_element_type=F32)


def _dot_tn(a, b):
    return lax.dot_general(a, b, (((0,), (0,)), ((), ())), preferred_element_type=F32)


def _sigmoid(x):
    return 1.0 / (1.0 + jnp.exp(-x))


def _log_sigmoid(x):
    return jnp.minimum(x, 0.0) - jnp.log1p(jnp.exp(-jnp.abs(x)))


def _rope_small(y, tab):
    return (y * tab[:, 0:LANES] + pltpu.roll(y, 96, 1) * tab[:, LANES:2 * LANES]
            + pltpu.roll(y, 32, 1) * tab[:, 2 * LANES:3 * LANES])


def _rope_full(y, tab):
    return y * tab[:, 0:LANES] + pltpu.roll(y, 64, 1) * tab[:, LANES:2 * LANES]


def _norm_to_scratch(x_ref, g_ref, xn_ref):
    @pl.when(pl.program_id(1) == 0)
    def _():
        xn_ref[...] = _rms_normalize(x_ref[...], g_ref[...]).astype(BF16)


def _rms_mm_plain_kernel(x_ref, g_ref, w_ref, o_ref, xn_ref):
    _norm_to_scratch(x_ref, g_ref, xn_ref)
    o_ref[...] = _dot(xn_ref[...], w_ref[...]).astype(o_ref.dtype)


def _rms_mm_mlaq_kernel(x_ref, g_ref, w_ref, tab_ref, o_ref, xn_ref, *, scale):
    _norm_to_scratch(x_ref, g_ref, xn_ref)
    y = _dot(xn_ref[...], w_ref[...])
    tab = tab_ref[...]
    for grp in range(y.shape[1] // LANES):
        yg = y[:, grp * LANES:(grp + 1) * LANES]
        og = yg * scale if grp % 2 == 0 else _rope_small(yg, tab)
        o_ref[:, grp * LANES:(grp + 1) * LANES] = og.astype(o_ref.dtype)


def _qkv_proj_kernel(xn_ref, w_ref, tab_ref, *rest, dilations, tm, sub_cols):
    out_refs, (scr_ref, scr4_ref) = rest[:len(dilations)], rest[len(dilations):]
    out_by_d = dict(zip(dilations, out_refs))
    tab = tab_ref[...]
    for sub in range(w_ref.shape[1] // sub_cols):
        y = _dot(xn_ref[...], w_ref[:, sub * sub_cols:(sub + 1) * sub_cols])
        for part in range(sub_cols // LANES):
            grp = sub * (sub_cols // LANES) + part
            cols = slice(grp * LANES, (grp + 1) * LANES)
            yg = _rope_full(y[:, part * LANES:(part + 1) * LANES], tab)
            out_by_d[1][0, :, cols] = yg.astype(BF16)
            scr_ref[grp] = yg
            for r4 in range(4):
                v = scr_ref[grp, pl.ds(r4, tm // 4, stride=4), :]
                out_by_d[4][r4, :, cols] = v.astype(BF16)
                scr4_ref[r4] = v
            for r4 in range(4):
                for a in range(4):
                    w = scr4_ref[r4, pl.ds(a, tm // 16, stride=4), :]
                    out_by_d[16][r4 + 4 * a, :, cols] = w.astype(BF16)


def qkv_projection(xn, w, tab, dilations, *, tm, tn):
    assert tuple(dilations) == (1, 4, 16), "the de-interleave is written as two stride-4 passes"
    t, k_dim = xn.shape
    n = w.shape[1]
    tiles = SEQ // tm
    q_tiles = ODD_MIX // tn
    return pl.pallas_call(
        functools.partial(_qkv_proj_kernel, dilations=tuple(dilations), tm=tm, sub_cols=2 * LANES),
        out_shape=tuple(jax.ShapeDtypeStruct((BATCH, d, SEQ // d, n), BF16) for d in dilations),
        grid=(n // tn, t // tm),
        in_specs=[
            pl.BlockSpec((tm, k_dim), lambda j, i: (i, 0)),
            pl.BlockSpec((k_dim, tn), lambda j, i: (0, j)),
            pl.BlockSpec((None, tm, 2 * LANES), lambda j, i: (j // q_tiles, i % tiles, 0)),
        ],
        out_specs=tuple(pl.BlockSpec((None, d, tm // d, tn), lambda j, i: (i // tiles, 0, i % tiles, j))
                        for d in dilations),
        scratch_shapes=[pltpu.VMEM((tn // LANES, tm, LANES), F32), pltpu.VMEM((4, tm // 4, LANES), F32)],
        compiler_params=_params(("parallel", "arbitrary")),
        name="qkv_projection",
    )(xn, w, tab)


def _norm_kernel(x_ref, g_ref, o_ref):
    o_ref[...] = _rms_normalize(x_ref[...], g_ref[...]).astype(o_ref.dtype)


def rms_norm_bf16(x, g, *, tm):
    t, d = x.shape
    return pl.pallas_call(
        _norm_kernel,
        out_shape=jax.ShapeDtypeStruct((t, d), BF16),
        grid=(t // tm,),
        in_specs=[pl.BlockSpec((tm, d), lambda i: (i, 0)), pl.BlockSpec((1, d), lambda i: (0, 0))],
        out_specs=pl.BlockSpec((tm, d), lambda i: (i, 0)),
        compiler_params=_params(("parallel",)),
        name="rms_norm_bf16",
    )(x, g.reshape(1, d))


def _mm_kernel(a_ref, w_ref, o_ref):
    o_ref[...] = _dot(a_ref[...], w_ref[...]).astype(o_ref.dtype)


def matmul_weight_stationary(a, w, *, tm, tn, out_dtype):
    t, k = a.shape
    n = w.shape[1]
    return pl.pallas_call(
        _mm_kernel,
        out_shape=jax.ShapeDtypeStruct((t, n), out_dtype),
        grid=(n // tn, t // tm),
        in_specs=[pl.BlockSpec((tm, k), lambda j, i: (i, 0)), pl.BlockSpec((k, tn), lambda j, i: (0, j))],
        out_specs=pl.BlockSpec((tm, tn), lambda j, i: (i, j)),
        compiler_params=_params(("parallel", "arbitrary")),
        name="matmul_weight_stationary",
    )(a, w)


def rms_matmul(x, x_col_block, k_dim, g, w, *, tm, tn, out_dtype, mode="plain", tab=None, scale=None):
    t = x.shape[0]
    n = w.shape[1]
    assert t % tm == 0 and n % tn == 0 and w.shape[0] == k_dim
    grid = (t // tm, n // tn)
    pos_tiles = SEQ // tm
    in_specs = [
        pl.BlockSpec((tm, k_dim), lambda i, j: (i, x_col_block)),
        pl.BlockSpec((1, k_dim), lambda i, j: (0, 0)),
        pl.BlockSpec((k_dim, tn), lambda i, j: (0, j)),
    ]
    args = [x, g.reshape(1, k_dim), w]
    if mode == "plain":
        body = _rms_mm_plain_kernel
    elif mode == "mlaq":
        body = functools.partial(_rms_mm_mlaq_kernel, scale=scale)
        in_specs.append(pl.BlockSpec((tm, 3 * LANES), lambda i, j: (i % pos_tiles, 0)))
        args.append(tab)
    else:
        raise ValueError(mode)
    return pl.pallas_call(
        body,
        out_shape=jax.ShapeDtypeStruct((t, n), out_dtype),
        grid=grid,
        in_specs=in_specs,
        out_specs=pl.BlockSpec((tm, tn), lambda i, j: (i, j)),
        scratch_shapes=[pltpu.VMEM((tm, k_dim), BF16)],
        compiler_params=_params(("parallel", "arbitrary")),
        name="rms_matmul_" + mode,
    )(*args)


def _krope_kernel(p_ref, tab_ref, o_ref):
    o_ref[...] = _rope_small(p_ref[...], tab_ref[...]).astype(o_ref.dtype)


def krope(proj, tab, *, tm):
    pos_tiles = SEQ // tm
    return pl.pallas_call(
        _krope_kernel,
        out_shape=jax.ShapeDtypeStruct((TOKENS, LANES), BF16),
        grid=(TOKENS // tm,),
        in_specs=[pl.BlockSpec((tm, LANES), lambda i: (i, EV_TAIL // LANES)),
                  pl.BlockSpec((tm, 3 * LANES), lambda i: (i % pos_tiles, 0))],
        out_specs=pl.BlockSpec((tm, LANES), lambda i: (i, 0)),
        compiler_params=_params(("parallel",)),
        name="krope",
    )(proj, tab)


def _mm_res_kernel(*refs):
    r_ref, o_ref = refs[-2], refs[-1]
    acc = r_ref[...]
    for k in range((len(refs) - 2) // 2):
        acc = acc + _dot(refs[2 * k][...], refs[2 * k + 1][...])
    o_ref[...] = acc


def matmul_residual(res, pairs, *, tm):
    t, n = res.shape
    in_specs, args = [], []
    for a, w in pairs:
        k = a.shape[1]
        in_specs += [pl.BlockSpec((tm, k), lambda i: (i, 0)), pl.BlockSpec((k, n), lambda i: (0, 0))]
        args += [a, w]
    in_specs.append(pl.BlockSpec((tm, n), lambda i: (i, 0)))
    args.append(res)
    return pl.pallas_call(
        _mm_res_kernel,
        out_shape=jax.ShapeDtypeStruct((t, n), F32),
        grid=(t // tm,),
        in_specs=in_specs,
        out_specs=pl.BlockSpec((tm, n), lambda i: (i, 0)),
        compiler_params=_params(("parallel",)),
        name="matmul_residual",
    )(*args)


def _mlp_kernel(x_ref, g_ref, w1_ref, w2_ref, gn_ref, *rest, tail):
    o_ref, xn_ref = rest[0], rest[-1]
    f = pl.program_id(1)

    @pl.when(f == 0)
    def _():
        x = x_ref[...]
        xn_ref[...] = _rms_normalize(x, g_ref[...]).astype(BF16)
        o_ref[...] = x

    h = jnp.maximum(_dot(xn_ref[...], w1_ref[...]), 0.0)
    o_ref[...] += _dot((h * h).astype(BF16), w2_ref[...])

    @pl.when(f == pl.num_programs(1) - 1)
    def _():
        normed = _rms_normalize(o_ref[...], gn_ref[...])
        if tail == "final":
            o_ref[...] = normed
        else:
            rest[1][...] = normed.astype(BF16)


def mlp_block(x, g, w1_all, w2_all, layer, gn, *, tm, tf, tail):
    t, d = x.shape
    dff = w1_all.shape[2]
    out_shape = [jax.ShapeDtypeStruct((t, d), F32)]
    out_specs = [pl.BlockSpec((tm, d), lambda i, f: (i, 0))]
    if tail == "next":
        out_shape.append(jax.ShapeDtypeStruct((t, d), BF16))
        out_specs.append(pl.BlockSpec((tm, d), lambda i, f: (i, 0)))
    return pl.pallas_call(
        functools.partial(_mlp_kernel, tail=tail),
        out_shape=tuple(out_shape),
        grid=(t // tm, dff // tf),
        in_specs=[
            pl.BlockSpec((tm, d), lambda i, f: (i, 0)),
            pl.BlockSpec((1, d), lambda i, f: (0, 0)),
            pl.BlockSpec((None, d, tf), lambda i, f: (layer, 0, f)),
            pl.BlockSpec((None, tf, d), lambda i, f: (layer, f, 0)),
            pl.BlockSpec((1, d), lambda i, f: (0, 0)),
        ],
        out_specs=tuple(out_specs),
        scratch_shapes=[pltpu.VMEM((tm, d), BF16)],
        compiler_params=_params(("parallel", "arbitrary")),
        name="mlp_block",
    )(x, g.reshape(1, d), w1_all, w2_all, gn.reshape(1, d))


def _mla_attn_kernel(q_ref, kv_ref, kr_ref, o_ref, m_ref, l_ref, acc_ref, *, tq, heads):
    qi = pl.program_id(2)
    hw = 2 * LANES
    m_ref[...] = jnp.full(m_ref.shape, NEG_INF, F32)
    l_ref[...] = jnp.zeros(l_ref.shape, F32)
    acc_ref[...] = jnp.zeros(acc_ref.shape, F32)

    def block(ki, masked):
        start = pl.multiple_of(ki * tq, tq)
        kr = kr_ref[pl.ds(start, tq), :]
        for h in range(heads):
            q = q_ref[:, h * hw:(h + 1) * hw]
            kn = kv_ref[pl.ds(start, tq), h * hw:h * hw + MLA_NOPE]
            v = kv_ref[pl.ds(start, tq), h * hw + MLA_NOPE:(h + 1) * hw]
            s = _dot_nt(q, jnp.concatenate([kn, kr], axis=-1))
            if masked:
                row = lax.broadcasted_iota(jnp.int32, s.shape, 0)
                col = lax.broadcasted_iota(jnp.int32, s.shape, 1)
                s = jnp.where(col <= row, s, NEG_INF)
            m_prev = m_ref[h]
            m_new = jnp.maximum(m_prev, jnp.max(s, axis=-1, keepdims=True))
            p = jnp.exp2(s - pltpu.repeat(m_new, tq // LANES, axis=1))
            alpha = jnp.exp2(m_prev - m_new)
            l_ref[h] = alpha * l_ref[h] + jnp.sum(p, axis=-1, keepdims=True)
            acc_ref[h] = alpha * acc_ref[h] + _dot(p.astype(BF16), v)
            m_ref[h] = m_new

    def body(ki, carry):
        block(ki, False)
        return carry

    lax.fori_loop(0, qi, body, 0)
    block(qi, True)
    for h in range(heads):
        o_ref[:, h * MLA_V:(h + 1) * MLA_V] = (acc_ref[h] / l_ref[h]).astype(o_ref.dtype)


def mla_attention(q, kv, kr, *, tq, heads):
    nq = SEQ // tq
    hw = 2 * LANES * heads
    return pl.pallas_call(
        functools.partial(_mla_attn_kernel, tq=tq, heads=heads),
        out_shape=jax.ShapeDtypeStruct((TOKENS, MLA_HEADS * MLA_V), BF16),
        grid=(BATCH, MLA_HEADS // heads, nq),
        in_specs=[
            pl.BlockSpec((tq, hw), lambda b, h, i: (b * nq + i, h)),
            pl.BlockSpec((SEQ, hw), lambda b, h, i: (b, h)),
            pl.BlockSpec((SEQ, LANES), lambda b, h, i: (b, 0)),
        ],
        out_specs=pl.BlockSpec((tq, heads * MLA_V), lambda b, h, i: (b * nq + i, h)),
        scratch_shapes=[pltpu.VMEM((heads, tq, LANES), F32), pltpu.VMEM((heads, tq, LANES), F32),
                        pltpu.VMEM((heads, tq, MLA_V), F32)],
        compiler_params=_params(("parallel", "parallel", "arbitrary")),
        name="mla_attention",
    )(q, kv, kr)


def _mlstm_kernel(qk_ref, v_ref, og_ref, gate_ref, cw_ref, cb_ref, gb_ref, out_ref,
                  hist_ref, c_ref, n_ref, m_ref):
    chunk = MLSTM_CHUNK
    c = pl.program_id(1)

    @pl.when(c == 0)
    def _():
        hist_ref[0:8, :] = jnp.zeros((8, hist_ref.shape[1]), F32)
        c_ref[...] = jnp.zeros(c_ref.shape, F32)
        n_ref[...] = jnp.zeros(n_ref.shape, F32)
        m_ref[...] = jnp.zeros(m_ref.shape, F32)

    hist_ref[8:8 + chunk, :] = qk_ref[...]
    y = jnp.broadcast_to(cb_ref[...], (chunk, hist_ref.shape[1]))
    for j in range(CONV_WIDTH):
        off = 8 - (CONV_WIDTH - 1) + j
        y = y + cw_ref[j:j + 1, :] * hist_ref[off:off + chunk, :]
    qk = y * _sigmoid(y)
    hist_ref[0:8, :] = hist_ref[chunk:chunk + 8, :]

    gates = gate_ref[...] + gb_ref[...]
    logf = _log_sigmoid(gates)
    row = lax.broadcasted_iota(jnp.int32, (chunk, chunk), 0)
    col = lax.broadcasted_iota(jnp.int32, (chunk, chunk), 1)
    tril = col <= row
    bcum = jnp.dot(tril.astype(F32), logf, preferred_element_type=F32,
                   precision=lax.Precision.HIGHEST)
    gates_t = gates.T
    bcum_t = bcum.T

    dk, dv = MLSTM_DK, MLSTM_DV
    for h in range(MLSTM_HEADS):
        q = qk[:, h * dk:(h + 1) * dk]
        k = qk[:, (MLSTM_HEADS + h) * dk:(MLSTM_HEADS + h + 1) * dk] * (dk ** -0.5)
        v = v_ref[:, h * dv:(h + 1) * dv].astype(BF16)
        b_c = bcum[:, TAIL_F + h:TAIL_F + h + 1]
        b_r = bcum_t[TAIL_F + h:TAIL_F + h + 1, :]
        i_c = gates[:, TAIL_I + h:TAIL_I + h + 1]
        i_r = gates_t[TAIL_I + h:TAIL_I + h + 1, :]
        b_last = bcum[chunk - 1:chunk, TAIL_F + h:TAIL_F + h + 1]
        m_prev = m_ref[h:h + 1, 0:1]
        n_prev = n_ref[h:h + 1, :]
        c_prev = c_ref[h]

        dmat = jnp.where(tril, b_c - b_r + i_r, NEG_INF)
        m_inter = b_c + m_prev
        m_row = jnp.maximum(m_inter, jnp.max(dmat, axis=-1, keepdims=True))
        q_b = q.astype(BF16)
        wmat = jnp.exp(dmat - m_row) * _dot_nt(q_b, k.astype(BF16))
        inter = jnp.exp(m_inter - m_row)
        num = _dot(wmat.astype(BF16), v) + inter * _dot(q_b, c_prev.astype(BF16))
        den = jnp.sum(wmat, axis=-1, keepdims=True) + inter * jnp.sum(q * n_prev, axis=-1, keepdims=True)
        hcell = num / jnp.maximum(jnp.abs(den), jnp.exp(-m_row))
        gate_o = _sigmoid(og_ref[:, h * dv:(h + 1) * dv])
        out_ref[:, h * dv:(h + 1) * dv] = (gate_o * hcell).astype(out_ref.dtype)

        g_c = b_last - b_c + i_c
        m_new = jnp.maximum(b_last + m_prev, jnp.max(g_c, axis=0, keepdims=True))
        wk = jnp.exp(g_c - m_new)
        decay = jnp.exp(b_last + m_prev - m_new)
        kw = k * wk
        c_ref[h] = decay * c_prev + _dot_tn(kw.astype(BF16), v)
        n_ref[h:h + 1, :] = decay * n_prev + jnp.sum(kw, axis=0, keepdims=True)
        m_ref[h:h + 1, :] = jnp.broadcast_to(m_new, (1, LANES))


def mlstm(proj, conv_w, conv_b, gate_bias):
    nc = SEQ // MLSTM_CHUNK
    chunk = MLSTM_CHUNK
    wqk = 2 * MLSTM_HEADS * MLSTM_DK
    wv = MLSTM_HEADS * MLSTM_DV
    row = lambda b, c: b * nc + c
    return pl.pallas_call(
        _mlstm_kernel,
        out_shape=jax.ShapeDtypeStruct((TOKENS, wv), BF16),
        grid=(BATCH, nc),
        in_specs=[
            pl.BlockSpec((chunk, wqk), lambda b, c: (row(b, c), EV_MQK // wqk)),
            pl.BlockSpec((chunk, wv), lambda b, c: (row(b, c), EV_MV // wv)),
            pl.BlockSpec((chunk, wv), lambda b, c: (row(b, c), EV_MO // wv)),
            pl.BlockSpec((chunk, LANES), lambda b, c: (row(b, c), EV_TAIL // LANES)),
            pl.BlockSpec((CONV_WIDTH, wqk), lambda b, c: (0, 0)),
            pl.BlockSpec((1, wqk), lambda b, c: (0, 0)),
            pl.BlockSpec((1, LANES), lambda b, c: (0, 0)),
        ],
        out_specs=pl.BlockSpec((chunk, wv), lambda b, c: (row(b, c), 0)),
        scratch_shapes=[
            pltpu.VMEM((chunk + 8, wqk), F32),
            pltpu.VMEM((MLSTM_HEADS, MLSTM_DK, MLSTM_DV), F32),
            pltpu.VMEM((8, MLSTM_DK), F32),
            pltpu.VMEM((8, LANES), F32),
        ],
        compiler_params=_params(("parallel", "arbitrary")),
        name="mlstm",
    )(proj, proj, proj, proj, conv_w, conv_b.reshape(1, wqk), gate_bias)


def _dilated_kernel(*refs, nres, nblk, has_prev, span, heads):
    if has_prev:
        q_ref, kp_ref, kc_ref, vp_ref, vc_ref, o_ref, lse_ref = refs
    else:
        q_ref, kc_ref, vc_ref, o_ref, lse_ref = refs
        kp_ref = vp_ref = None
    n = pl.program_id(2)
    hg = pl.program_id(3)
    blk_sz = DIL_BLOCK
    dh = DIL_HEAD_DIM
    nkeys = 2 * blk_sz if has_prev else blk_sz
    qi = lax.broadcasted_iota(jnp.int32, (blk_sz, nkeys), 0)
    kj = lax.broadcasted_iota(jnp.int32, (blk_sz, nkeys), 1)
    dist = (nkeys - blk_sz) + qi - kj
    bias_full = jnp.where(dist >= 0, jnp.where(dist <= span, 0.0, NEG_INF), NEG_INF)
    bias_first = jnp.where(kj >= nkeys - blk_sz, bias_full, NEG_INF)
    lane = lax.broadcasted_iota(jnp.int32, (blk_sz, LANES), 1)

    @pl.when(hg == 0)
    def _():
        lse_ref[...] = jnp.zeros(lse_ref.shape, F32)

    def stack(ref, res, rows):
        return jnp.stack([ref[res, rows, h * dh:(h + 1) * dh] for h in range(heads)])

    for res in range(nres):
        for blk in range(nblk):
            rows = slice(blk * blk_sz, (blk + 1) * blk_sz)
            q3 = stack(q_ref, res, rows)
            k3 = stack(kc_ref, res, rows)
            v3 = stack(vc_ref, res, rows)
            bias = bias_full
            if has_prev:
                if blk == 0:
                    first = slice(0, blk_sz)
                    kp3, vp3 = stack(kp_ref, res, first), stack(vp_ref, res, first)
                    bias = jnp.where(n > 0, bias_full, bias_first)
                else:
                    prows = slice((blk - 1) * blk_sz, blk * blk_sz)
                    kp3, vp3 = stack(kc_ref, res, prows), stack(vc_ref, res, prows)
                k3 = jnp.concatenate([kp3, k3], axis=1)
                v3 = jnp.concatenate([vp3, v3], axis=1)
            s = jnp.einsum("hqd,hkd->hqk", q3, k3, preferred_element_type=F32) + bias[None]
            m = jnp.max(s, axis=-1, keepdims=True)
            p = jnp.exp2(s - m)
            den = jnp.sum(p, axis=-1, keepdims=True)
            o = jnp.einsum("hqk,hkd->hqd", p.astype(BF16), v3, preferred_element_type=F32) / den
            lse = m * LN2 + jnp.log(den)
            lse_tile = lse_ref[res, rows, :]
            for h in range(heads):
                o_ref[res, rows, h * dh:(h + 1) * dh] = o[h].astype(o_ref.dtype)
                lse_tile = jnp.where(lane == hg * heads + h, lse[h], lse_tile)
            lse_ref[res, rows, :] = lse_tile


def dilated_pattern(qkv_perm, window, dilation, *, hw, units):
    span = window // dilation
    length = SEQ // dilation
    nblk = min(length // DIL_BLOCK, units)
    nres = min(dilation, units // nblk)
    tl = nblk * DIL_BLOCK
    ntile = length // tl
    has_prev = length > DIL_BLOCK
    heads = hw // DIL_HEAD_DIM
    ngroups = ODD_MIX // hw

    def cur(which):
        return pl.BlockSpec((None, nres, tl, hw), lambda b, r, n, g: (b, r, n, which * ngroups + g))

    def prev(which):
        return pl.BlockSpec((None, nres, DIL_BLOCK, hw),
                            lambda b, r, n, g: (b, r, jnp.maximum(n * nblk - 1, 0), which * ngroups + g))

    if has_prev:
        in_specs = [cur(0), prev(1), cur(1), prev(2), cur(2)]
    else:
        in_specs = [cur(0), cur(1), cur(2)]
    return pl.pallas_call(
        functools.partial(_dilated_kernel, nres=nres, nblk=nblk, has_prev=has_prev, span=span, heads=heads),
        out_shape=(jax.ShapeDtypeStruct((BATCH, dilation, length, ODD_MIX), BF16),
                   jax.ShapeDtypeStruct((BATCH, dilation, length, LANES), F32)),
        grid=(BATCH, dilation // nres, ntile, ngroups),
        in_specs=in_specs,
        out_specs=(pl.BlockSpec((None, nres, tl, hw), lambda b, r, n, g: (b, r, n, g)),
                   pl.BlockSpec((None, nres, tl, LANES), lambda b, r, n, g: (b, r, n, 0))),
        compiler_params=_params(("parallel", "parallel", "arbitrary", "arbitrary")),
        name="dilated_d%d" % dilation,
    )(*([qkv_perm] * len(in_specs)))


def _merge_kernel(o1_ref, o4_ref, o16_ref, l1_ref, l4_ref, l16_ref, out_ref,
                  nat_scr, p4_scr, o4_scr, o_scr, *, tm):
    dh = DIL_HEAD_DIM
    q4, q16 = tm // 4, tm // 16

    for r4 in range(4):
        for a in range(4):
            p4_scr[r4, pl.ds(a, q16, stride=4), :] = l16_ref[r4 + 4 * a]
    for r4 in range(4):
        nat_scr[0, pl.ds(r4, q4, stride=4), :] = l4_ref[r4]
        nat_scr[1, pl.ds(r4, q4, stride=4), :] = p4_scr[r4]
    l1, l4, l16 = l1_ref[0], nat_scr[0], nat_scr[1]
    mx = jnp.maximum(jnp.maximum(l1, l4), l16)
    e1, e4, e16 = jnp.exp(l1 - mx), jnp.exp(l4 - mx), jnp.exp(l16 - mx)
    tot = e1 + e4 + e16
    w1 = e1 / tot
    nat_scr[0] = e4 / tot
    nat_scr[1] = e16 / tot
    w4 = jnp.concatenate([nat_scr[0, pl.ds(r4, q4, stride=4), :] for r4 in range(4)], axis=0)
    for r4 in range(4):
        p4_scr[r4] = nat_scr[1, pl.ds(r4, q4, stride=4), :]
    w16_parts = [None] * 16
    for r4 in range(4):
        for a in range(4):
            w16_parts[r4 + 4 * a] = p4_scr[r4, pl.ds(a, q16, stride=4), :]
    w16 = jnp.concatenate(w16_parts, axis=0)

    head_of_col = lax.broadcasted_iota(jnp.int32, (2 * LANES, ODD_MIX), 1) // dh
    lane_of_row = lax.broadcasted_iota(jnp.int32, (2 * LANES, ODD_MIX), 0) % LANES
    expand = jnp.where(head_of_col == lane_of_row, 1.0, 0.0).astype(BF16)

    def two_terms(w):
        hi = w.astype(BF16)
        return jnp.concatenate([hi, (w - hi.astype(F32)).astype(BF16)], axis=1)

    w1, w4, w16 = two_terms(w1), two_terms(w4), two_terms(w16)

    pair = 2
    for hp in range(DIL_HEADS // pair):
        spread = expand[:, hp * pair * dh:(hp + 1) * pair * dh]
        w1p, w4p, w16p = _dot(w1, spread), _dot(w4, spread), _dot(w16, spread)
        for hh in range(pair):
            cols = slice((hp * pair + hh) * dh, (hp * pair + hh + 1) * dh)
            part = slice(hh * dh, (hh + 1) * dh)
            for r4 in range(4):
                for a in range(4):
                    r = r4 + 4 * a
                    o4_scr[r4, pl.ds(a, q16, stride=4), :] = (w16p[r * q16:(r + 1) * q16, part]
                                                             * o16_ref[r, :, cols])
            for r4 in range(4):
                acc4 = o4_scr[r4] + w4p[r4 * q4:(r4 + 1) * q4, part] * o4_ref[r4, :, cols]
                o_scr[pl.ds(r4, q4, stride=4), :] = acc4
            out_ref[:, cols] = (o_scr[...] + w1p[:, part] * o1_ref[0, :, cols]).astype(out_ref.dtype)


def merge_patterns(outs, lses, dilations, *, tm):
    assert tuple(dilations) == (1, 4, 16), "the re-interleave is written as two stride-4 passes"
    tiles = SEQ // tm

    def spec(d, width):
        return pl.BlockSpec((None, d, tm // d, width), lambda i: (i // tiles, 0, i % tiles, 0))

    return pl.pallas_call(
        functools.partial(_merge_kernel, tm=tm),
        out_shape=jax.ShapeDtypeStruct((TOKENS, ODD_MIX), BF16),
        grid=(TOKENS // tm,),
        in_specs=[spec(d, ODD_MIX) for d in dilations] + [spec(d, LANES) for d in dilations],
        out_specs=pl.BlockSpec((tm, ODD_MIX), lambda i: (i, 0)),
        scratch_shapes=[pltpu.VMEM((2, tm, LANES), F32),
                        pltpu.VMEM((4, tm // 4, LANES), F32),
                        pltpu.VMEM((4, tm // 4, DIL_HEAD_DIM), F32),
                        pltpu.VMEM((tm, DIL_HEAD_DIM), F32)],
        compiler_params=_params(("parallel",)),
        name="merge_patterns",
    )(*outs, *lses)


def _rope_tables():
    pos = jnp.arange(SEQ, dtype=F32)[:, None]
    half = MLA_ROPE // 2
    inv = ROPE_THETA ** (-jnp.arange(half, dtype=F32) * 2.0 / MLA_ROPE)
    ang = pos * inv[None, :]
    c, s = jnp.cos(ang), jnp.sin(ang)
    z = jnp.zeros_like(c)
    small = jnp.concatenate([c, c, z, z, -s, z, z, z, z, s, z, z], axis=-1)
    half = DIL_HEAD_DIM // 2
    inv = ROPE_THETA ** (-jnp.arange(half, dtype=F32) * 2.0 / DIL_HEAD_DIM)
    ang = pos * inv[None, :]
    c, s = jnp.cos(ang), jnp.sin(ang)
    full = jnp.concatenate([c, c, -s, s], axis=-1)
    return small, full


def _even_weights(w_in, w_uq, w_ukv, b_i, b_f):
    cuts = np.cumsum((MLA_Q_LORA, MLA_KV_LORA, MLA_ROPE, 2 * MLSTM_HEADS * MLSTM_DK,
                      MLSTM_HEADS * MLSTM_DV, MLSTM_HEADS, MLSTM_HEADS))
    c_q, c_kv, k_r, m_qk, m_v, m_i, m_f, m_o = jnp.split(w_in, cuts.tolist(), axis=1)
    pad = jnp.zeros((D_MODEL, EV_PROJ - EV_TAIL - MLA_ROPE - 2 * MLSTM_HEADS), w_in.dtype)
    w_in_r = jnp.concatenate([c_q, c_kv, m_qk, m_v, m_o, k_r, m_i, m_f, pad], axis=1).astype(BF16)
    uq = w_uq.reshape(MLA_Q_LORA, MLA_HEADS, MLA_QK)
    uq = jnp.pad(uq, ((0, 0), (0, 0), (0, 2 * LANES - MLA_QK)))
    w_uq_r = uq.reshape(MLA_Q_LORA, MLA_HEADS * 2 * LANES).astype(BF16)
    gate_bias = jnp.concatenate([jnp.zeros((TAIL_I,), F32), b_i.astype(F32), b_f.astype(F32),
                                 jnp.zeros((LANES - TAIL_F - MLSTM_HEADS,), F32)]).reshape(1, LANES)
    return w_in_r, w_uq_r, w_ukv.astype(BF16), gate_bias


def _even_layer(x, xn, w_in, q_norm, w_uq, kv_norm, w_ukv, conv_w, conv_b, b_i, b_f, w_out, small_tab):
    w_in_r, w_uq_r, w_ukv_r, gate_bias = _even_weights(w_in, w_uq, w_ukv, b_i, b_f)
    proj = matmul_weight_stationary(xn, w_in_r, tm=512, tn=EV_PROJ // 3, out_dtype=F32)
    scale = MLA_QK ** -0.5 * LOG2E
    q = rms_matmul(proj, EV_CQ // MLA_Q_LORA, MLA_Q_LORA, q_norm, w_uq_r, tm=512, tn=2048,
                   out_dtype=BF16, mode="mlaq", tab=small_tab * scale, scale=scale)
    kv = rms_matmul(proj, EV_CKV // MLA_KV_LORA, MLA_KV_LORA, kv_norm, w_ukv_r, tm=512, tn=2048,
                    out_dtype=BF16)
    kr = krope(proj, small_tab, tm=1024)
    a_out = mla_attention(q, kv, kr, tq=512, heads=2)
    hm = mlstm(proj, conv_w, conv_b, gate_bias)
    w_out_b = w_out.astype(BF16)
    n_a = MLA_HEADS * MLA_V
    return matmul_residual(x, [(a_out, w_out_b[:n_a]), (hm, w_out_b[n_a:])], tm=512)


def _odd_layer(x, xn, w_qkv, w_out, full_tab):
    scale = DIL_HEAD_DIM ** -0.5 * LOG2E
    identity = jnp.concatenate([jnp.ones((SEQ, LANES), F32), jnp.zeros((SEQ, LANES), F32)], axis=-1)
    tab = jnp.stack([full_tab * scale, full_tab, identity])
    dilations = [d for _, d in DIL_PATTERNS]
    qkv_perms = qkv_projection(xn, w_qkv.astype(BF16), tab, dilations, tm=512, tn=1024)
    outs, lses = [], []
    for (window, dilation), qkv_perm in zip(DIL_PATTERNS, qkv_perms):
        o_g, lse_g = dilated_pattern(qkv_perm, window, dilation, hw=1024, units=4)
        outs.append(o_g)
        lses.append(lse_g)
    o = merge_patterns(outs, lses, dilations, tm=512)
    return matmul_residual(x, [(o, w_out.astype(BF16))], tm=512)


def kernel(x, norm_mix, norm_mlp, ev_w_in, mla_q_norm, mla_w_uq, mla_kv_norm, mla_w_ukv,
           mlstm_conv_w, mlstm_conv_b, mlstm_b_i, mlstm_b_f, ev_w_out, od_w_qkv, od_w_out,
           mlp_w1, mlp_w2, norm_final):
    assert x.shape == (BATCH, SEQ, D_MODEL) and x.dtype == F32
    small_tab, full_tab = _rope_tables()
    w1_all, w2_all = mlp_w1.astype(BF16), mlp_w2.astype(BF16)
    xt = x.reshape(TOKENS, D_MODEL)
    xn = rms_norm_bf16(xt, norm_mix[0], tm=512)
    for layer in range(DEPTH):
        i = layer // 2
        if layer % 2 == 0:
            xt = _even_layer(xt, xn, ev_w_in[i], mla_q_norm[i], mla_w_uq[i], mla_kv_norm[i],
                             mla_w_ukv[i], mlstm_conv_w[i], mlstm_conv_b[i], mlstm_b_i[i], mlstm_b_f[i],
                             ev_w_out[i], small_tab)
        else:
            xt = _odd_layer(xt, xn, od_w_qkv[i], od_w_out[i], full_tab)
        if layer == DEPTH - 1:
            (xt,) = mlp_block(xt, norm_mlp[layer], w1_all, w2_all, layer, norm_final, tm=512, tf=1024,
                              tail="final")
        else:
            xt, xn = mlp_block(xt, norm_mlp[layer], w1_all, w2_all, layer, norm_mix[layer + 1], tm=512,
                               tf=1024, tail="next")
    return xt.reshape(BATCH, SEQ, D_MODEL)
```

```python
import functools

import jax
import jax.numpy as jnp
import numpy as np
from jax import lax
from jax.experimental import pallas as pl
from jax.experimental.pallas import tpu as pltpu

D_MODEL = 2048
BATCH = 4
SEQ = 2048
DEPTH = 4
MLA_HEADS = 8
MLA_Q_LORA = 512
MLA_KV_LORA = 512
MLA_NOPE = 128
MLA_ROPE = 64
MLA_V = 128
MLA_QK = MLA_NOPE + MLA_ROPE
MLSTM_HEADS = 4
MLSTM_DK = 128
MLSTM_DV = 256
MLSTM_CHUNK = 128
CONV_WIDTH = 4
DIL_HEADS = 16
DIL_HEAD_DIM = 128
DIL_PATTERNS = ((128, 1), (512, 4), (2048, 16))
DIL_BLOCK = 128
D_FF = 4 * D_MODEL
ROPE_THETA = 10000.0
NORM_EPS = 1e-6
ODD_MIX = DIL_HEADS * DIL_HEAD_DIM
TOKENS = BATCH * SEQ

LANES = 128
VMEM_LIMIT_BYTES = 56 * 1024 * 1024

EV_CQ, EV_CKV, EV_MQK, EV_MV, EV_MO, EV_TAIL = 0, 512, 1024, 2048, 3072, 4096
EV_PROJ = 4224
TAIL_I = MLA_ROPE
TAIL_F = MLA_ROPE + MLSTM_HEADS

F32 = jnp.float32
BF16 = jnp.bfloat16
NEG_INF = float("-inf")
LOG2E = float(np.log2(np.e))
LN2 = float(np.log(2.0))


def _params(semantics):
    return pltpu.CompilerParams(dimension_semantics=semantics, vmem_limit_bytes=VMEM_LIMIT_BYTES)


def _rms_normalize(x, g):
    ms = jnp.mean(x * x, axis=-1, keepdims=True)
    return x * lax.rsqrt(ms + NORM_EPS) * g


def _dot(a, b):
    return jnp.dot(a, b, preferred_element_type=F32)


def _dot_nt(a, b):
    return lax.dot_general(a, b, (((1,), (1,)), ((), ())), preferred_element_type=F32)


def _dot_tn(a, b):
    return lax.dot_general(a, b, (((0,), (0,)), ((), ())), preferred_element_type=F32)


def _sigmoid(x):
    return 1.0 / (1.0 + jnp.exp(-x))


def _log_sigmoid(x):
    return jnp.minimum(x, 0.0) - jnp.log1p(jnp.exp(-jnp.abs(x)))


def _rope_small(y, tab):
    return (y * tab[:, 0:LANES] + pltpu.roll(y, 96, 1) * tab[:, LANES:2 * LANES]
            + pltpu.roll(y, 32, 1) * tab[:, 2 * LANES:3 * LANES])


def _rope_full(y, tab):
    return y * tab[:, 0:LANES] + pltpu.roll(y, 64, 1) * tab[:, LANES:2 * LANES]


def _norm_to_scratch(x_ref, g_ref, xn_ref):
    @pl.when(pl.program_id(1) == 0)
    def _():
        xn_ref[...] = _rms_normalize(x_ref[...], g_ref[...]).astype(BF16)


def _rms_mm_plain_kernel(x_ref, g_ref, w_ref, o_ref, xn_ref):
    _norm_to_scratch(x_ref, g_ref, xn_ref)
    o_ref[...] = _dot(xn_ref[...], w_ref[...]).astype(o_ref.dtype)


def _rms_mm_mlaq_kernel(x_ref, g_ref, w_ref, tab_ref, o_ref, xn_ref, *, scale):
    _norm_to_scratch(x_ref, g_ref, xn_ref)
    y = _dot(xn_ref[...], w_ref[...])
    tab = tab_ref[...]
    for grp in range(y.shape[1] // LANES):
        yg = y[:, grp * LANES:(grp + 1) * LANES]
        og = yg * scale if grp % 2 == 0 else _rope_small(yg, tab)
        o_ref[:, grp * LANES:(grp + 1) * LANES] = og.astype(o_ref.dtype)


def _qkv_proj_kernel(xn_ref, w_ref, tab_ref, *rest, dilations, tm, sub_cols):
    out_refs, (scr_ref, scr4_ref) = rest[:len(dilations)], rest[len(dilations):]
    out_by_d = dict(zip(dilations, out_refs))
    tab = tab_ref[...]
    for sub in range(w_ref.shape[1] // sub_cols):
        y = _dot(xn_ref[...], w_ref[:, sub * sub_cols:(sub + 1) * sub_cols])
        for part in range(sub_cols // LANES):
            grp = sub * (sub_cols // LANES) + part
            cols = slice(grp * LANES, (grp + 1) * LANES)
            yg = _rope_full(y[:, part * LANES:(part + 1) * LANES], tab)
            out_by_d[1][0, :, cols] = yg.astype(BF16)
            scr_ref[grp] = yg
            for r4 in range(4):
                v = scr_ref[grp, pl.ds(r4, tm // 4, stride=4), :]
                out_by_d[4][r4, :, cols] = v.astype(BF16)
                scr4_ref[r4] = v
            for r4 in range(4):
                for a in range(4):
                    w = scr4_ref[r4, pl.ds(a, tm // 16, stride=4), :]
                    out_by_d[16][r4 + 4 * a, :, cols] = w.astype(BF16)


def qkv_projection(xn, w, tab, dilations, *, tm, tn):
    assert tuple(dilations) == (1, 4, 16), "the de-interleave is written as two stride-4 passes"
    t, k_dim = xn.shape
    n = w.shape[1]
    tiles = SEQ // tm
    q_tiles = ODD_MIX // tn
    return pl.pallas_call(
        functools.partial(_qkv_proj_kernel, dilations=tuple(dilations), tm=tm, sub_cols=2 * LANES),
        out_shape=tuple(jax.ShapeDtypeStruct((BATCH, d, SEQ // d, n), BF16) for d in dilations),
        grid=(n // tn, t // tm),
        in_specs=[
            pl.BlockSpec((tm, k_dim), lambda j, i: (i, 0)),
            pl.BlockSpec((k_dim, tn), lambda j, i: (0, j)),
            pl.BlockSpec((None, tm, 2 * LANES), lambda j, i: (j // q_tiles, i % tiles, 0)),
        ],
        out_specs=tuple(pl.BlockSpec((None, d, tm // d, tn), lambda j, i: (i // tiles, 0, i % tiles, j))
                        for d in dilations),
        scratch_shapes=[pltpu.VMEM((tn // LANES, tm, LANES), F32), pltpu.VMEM((4, tm // 4, LANES), F32)],
        compiler_params=_params(("parallel", "arbitrary")),
        name="qkv_projection",
    )(xn, w, tab)


def _norm_kernel(x_ref, g_ref, o_ref):
    o_ref[...] = _rms_normalize(x_ref[...], g_ref[...]).astype(o_ref.dtype)


def rms_norm_bf16(x, g, *, tm):
    t, d = x.shape
    return pl.pallas_call(
        _norm_kernel,
        out_shape=jax.ShapeDtypeStruct((t, d), BF16),
        grid=(t // tm,),
        in_specs=[pl.BlockSpec((tm, d), lambda i: (i, 0)), pl.BlockSpec((1, d), lambda i: (0, 0))],
        out_specs=pl.BlockSpec((tm, d), lambda i: (i, 0)),
        compiler_params=_params(("parallel",)),
        name="rms_norm_bf16",
    )(x, g.reshape(1, d))


def _mm_kernel(a_ref, w_ref, o_ref):
    o_ref[...] = _dot(a_ref[...], w_ref[...]).astype(o_ref.dtype)


def matmul_weight_stationary(a, w, *, tm, tn, out_dtype):
    t, k = a.shape
    n = w.shape[1]
    return pl.pallas_call(
        _mm_kernel,
        out_shape=jax.ShapeDtypeStruct((t, n), out_dtype),
        grid=(n // tn, t // tm),
        in_specs=[pl.BlockSpec((tm, k), lambda j, i: (i, 0)), pl.BlockSpec((k, tn), lambda j, i: (0, j))],
        out_specs=pl.BlockSpec((tm, tn), lambda j, i: (i, j)),
        compiler_params=_params(("parallel", "arbitrary")),
        name="matmul_weight_stationary",
    )(a, w)


def rms_matmul(x, x_col_block, k_dim, g, w, *, tm, tn, out_dtype, mode="plain", tab=None, scale=None):
    t = x.shape[0]
    n = w.shape[1]
    assert t % tm == 0 and n % tn == 0 and w.shape[0] == k_dim
    grid = (t // tm, n // tn)
    pos_tiles = SEQ // tm
    in_specs = [
        pl.BlockSpec((tm, k_dim), lambda i, j: (i, x_col_block)),
        pl.BlockSpec((1, k_dim), lambda i, j: (0, 0)),
        pl.BlockSpec((k_dim, tn), lambda i, j: (0, j)),
    ]
    args = [x, g.reshape(1, k_dim), w]
    if mode == "plain":
        body = _rms_mm_plain_kernel
    elif mode == "mlaq":
        body = functools.partial(_rms_mm_mlaq_kernel, scale=scale)
        in_specs.append(pl.BlockSpec((tm, 3 * LANES), lambda i, j: (i % pos_tiles, 0)))
        args.append(tab)
    else:
        raise ValueError(mode)
    return pl.pallas_call(
        body,
        out_shape=jax.ShapeDtypeStruct((t, n), out_dtype),
        grid=grid,
        in_specs=in_specs,
        out_specs=pl.BlockSpec((tm, tn), lambda i, j: (i, j)),
        scratch_shapes=[pltpu.VMEM((tm, k_dim), BF16)],
        compiler_params=_params(("parallel", "arbitrary")),
        name="rms_matmul_" + mode,
    )(*args)


def _krope_kernel(p_ref, tab_ref, o_ref):
    o_ref[...] = _rope_small(p_ref[...], tab_ref[...]).astype(o_ref.dtype)


def krope(proj, tab, *, tm):
    pos_tiles = SEQ // tm
    return pl.pallas_call(
        _krope_kernel,
        out_shape=jax.ShapeDtypeStruct((TOKENS, LANES), BF16),
        grid=(TOKENS // tm,),
        in_specs=[pl.BlockSpec((tm, LANES), lambda i: (i, EV_TAIL // LANES)),
                  pl.BlockSpec((tm, 3 * LANES), lambda i: (i % pos_tiles, 0))],
        out_specs=pl.BlockSpec((tm, LANES), lambda i: (i, 0)),
        compiler_params=_params(("parallel",)),
        name="krope",
    )(proj, tab)


def _mm_res_kernel(*refs):
    r_ref, o_ref = refs[-2], refs[-1]
    acc = r_ref[...]
    for k in range((len(refs) - 2) // 2):
        acc = acc + _dot(refs[2 * k][...], refs[2 * k + 1][...])
    o_ref[...] = acc


def matmul_residual(res, pairs, *, tm):
    t, n = res.shape
    in_specs, args = [], []
    for a, w in pairs:
        k = a.shape[1]
        in_specs += [pl.BlockSpec((tm, k), lambda i: (i, 0)), pl.BlockSpec((k, n), lambda i: (0, 0))]
        args += [a, w]
    in_specs.append(pl.BlockSpec((tm, n), lambda i: (i, 0)))
    args.append(res)
    return pl.pallas_call(
        _mm_res_kernel,
        out_shape=jax.ShapeDtypeStruct((t, n), F32),
        grid=(t // tm,),
        in_specs=in_specs,
        out_specs=pl.BlockSpec((tm, n), lambda i: (i, 0)),
        compiler_params=_params(("parallel",)),
        name="matmul_residual",
    )(*args)


def _mlp_kernel(x_ref, g_ref, w1_ref, w2_ref, gn_ref, *rest, tail):
    o_ref, xn_ref = rest[0], rest[-1]
    f = pl.program_id(1)

    @pl.when(f == 0)
    def _():
        x = x_ref[...]
        xn_ref[...] = _rms_normalize(x, g_ref[...]).astype(BF16)
        o_ref[...] = x

    h = jnp.maximum(_dot(xn_ref[...], w1_ref[...]), 0.0)
    o_ref[...] += _dot((h * h).astype(BF16), w2_ref[...])

    @pl.when(f == pl.num_programs(1) - 1)
    def _():
        normed = _rms_normalize(o_ref[...], gn_ref[...])
        if tail == "final":
            o_ref[...] = normed
        else:
            rest[1][...] = normed.astype(BF16)


def mlp_block(x, g, w1_all, w2_all, layer, gn, *, tm, tf, tail):
    t, d = x.shape
    dff = w1_all.shape[2]
    out_shape = [jax.ShapeDtypeStruct((t, d), F32)]
    out_specs = [pl.BlockSpec((tm, d), lambda i, f: (i, 0))]
    if tail == "next":
        out_shape.append(jax.ShapeDtypeStruct((t, d), BF16))
        out_specs.append(pl.BlockSpec((tm, d), lambda i, f: (i, 0)))
    return pl.pallas_call(
        functools.partial(_mlp_kernel, tail=tail),
        out_shape=tuple(out_shape),
        grid=(t // tm, dff // tf),
        in_specs=[
            pl.BlockSpec((tm, d), lambda i, f: (i, 0)),
            pl.BlockSpec((1, d), lambda i, f: (0, 0)),
            pl.BlockSpec((None, d, tf), lambda i, f: (layer, 0, f)),
            pl.BlockSpec((None, tf, d), lambda i, f: (layer, f, 0)),
            pl.BlockSpec((1, d), lambda i, f: (0, 0)),
        ],
        out_specs=tuple(out_specs),
        scratch_shapes=[pltpu.VMEM((tm, d), BF16)],
        compiler_params=_params(("parallel", "arbitrary")),
        name="mlp_block",
    )(x, g.reshape(1, d), w1_all, w2_all, gn.reshape(1, d))


def _mla_attn_kernel(q_ref, kv_ref, kr_ref, *rest, tq, heads, ncast):
    cast_in, o_ref, cast_out = rest[:ncast], rest[ncast], rest[ncast + 1:2 * ncast + 1]
    m_ref, l_ref, acc_ref = rest[2 * ncast + 1:]
    for src, dst in zip(cast_in, cast_out):
        dst[...] = src[...].astype(dst.dtype)
    qi = pl.program_id(2)
    hw = 2 * LANES
    m_ref[...] = jnp.full(m_ref.shape, NEG_INF, F32)
    l_ref[...] = jnp.zeros(l_ref.shape, F32)
    acc_ref[...] = jnp.zeros(acc_ref.shape, F32)

    def block(ki, masked):
        start = pl.multiple_of(ki * tq, tq)
        kr = kr_ref[pl.ds(start, tq), :]
        for h in range(heads):
            q = q_ref[:, h * hw:(h + 1) * hw]
            kn = kv_ref[pl.ds(start, tq), h * hw:h * hw + MLA_NOPE]
            v = kv_ref[pl.ds(start, tq), h * hw + MLA_NOPE:(h + 1) * hw]
            s = _dot_nt(q, jnp.concatenate([kn, kr], axis=-1))
            if masked:
                row = lax.broadcasted_iota(jnp.int32, s.shape, 0)
                col = lax.broadcasted_iota(jnp.int32, s.shape, 1)
                s = jnp.where(col <= row, s, NEG_INF)
            m_prev = m_ref[h]
            m_new = jnp.maximum(m_prev, jnp.max(s, axis=-1, keepdims=True))
            p = jnp.exp2(s - jnp.tile(m_new, (1, tq // LANES)))
            alpha = jnp.exp2(m_prev - m_new)
            l_ref[h] = alpha * l_ref[h] + jnp.sum(p, axis=-1, keepdims=True)
            acc_ref[h] = alpha * acc_ref[h] + _dot(p.astype(BF16), v)
            m_ref[h] = m_new

    def body(ki, carry):
        block(ki, False)
        return carry

    lax.fori_loop(0, qi, body, 0)
    block(qi, True)
    for h in range(heads):
        o_ref[:, h * MLA_V:(h + 1) * MLA_V] = (acc_ref[h] / l_ref[h]).astype(o_ref.dtype)


def mla_attention(q, kv, kr, casts, *, tq, heads):
    nq = SEQ // tq
    hw = 2 * LANES * heads
    groups = MLA_HEADS // heads
    steps = BATCH * groups * nq

    def step(b, h, i):
        return (b * groups + h) * nq + i

    cast_in_specs, cast_out_specs, cast_shapes = [], [], []
    for w, first, count in casts:
        _, rows, cols = w.shape
        slabs = steps // count
        assert steps % count == 0 and rows % slabs == 0 and (rows // slabs) % 16 == 0
        block = (None, rows // slabs, cols)
        cast_in_specs.append(pl.BlockSpec(
            block, lambda b, h, i, first=first, slabs=slabs: (first + step(b, h, i) // slabs,
                                                               step(b, h, i) % slabs, 0)))
        cast_out_specs.append(pl.BlockSpec(
            block, lambda b, h, i, slabs=slabs: (step(b, h, i) // slabs, step(b, h, i) % slabs, 0)))
        cast_shapes.append(jax.ShapeDtypeStruct((count, rows, cols), BF16))

    outs = pl.pallas_call(
        functools.partial(_mla_attn_kernel, tq=tq, heads=heads, ncast=len(casts)),
        out_shape=(jax.ShapeDtypeStruct((TOKENS, MLA_HEADS * MLA_V), BF16), *cast_shapes),
        grid=(BATCH, groups, nq),
        in_specs=[
            pl.BlockSpec((tq, hw), lambda b, h, i: (b * nq + i, h)),
            pl.BlockSpec((SEQ, hw), lambda b, h, i: (b, h)),
            pl.BlockSpec((SEQ, LANES), lambda b, h, i: (b, 0)),
            *cast_in_specs,
        ],
        out_specs=(pl.BlockSpec((tq, heads * MLA_V), lambda b, h, i: (b * nq + i, h)), *cast_out_specs),
        scratch_shapes=[pltpu.VMEM((heads, tq, LANES), F32), pltpu.VMEM((heads, tq, LANES), F32),
                        pltpu.VMEM((heads, tq, MLA_V), F32)],
        compiler_params=_params(("parallel", "parallel", "arbitrary")),
        name="mla_attention",
    )(q, kv, kr, *[w for w, _, _ in casts])
    return outs[0], outs[1:]


def _mlstm_kernel(qk_ref, v_ref, og_ref, gate_ref, cw_ref, cb_ref, gb_ref, out_ref,
                  hist_ref, c_ref, n_ref, m_ref):
    chunk = MLSTM_CHUNK
    c = pl.program_id(1)

    @pl.when(c == 0)
    def _():
        hist_ref[0:8, :] = jnp.zeros((8, hist_ref.shape[1]), F32)
        c_ref[...] = jnp.zeros(c_ref.shape, F32)
        n_ref[...] = jnp.zeros(n_ref.shape, F32)
        m_ref[...] = jnp.zeros(m_ref.shape, F32)

    hist_ref[8:8 + chunk, :] = qk_ref[...]
    y = jnp.broadcast_to(cb_ref[...], (chunk, hist_ref.shape[1]))
    for j in range(CONV_WIDTH):
        off = 8 - (CONV_WIDTH - 1) + j
        y = y + cw_ref[j:j + 1, :] * hist_ref[off:off + chunk, :]
    qk = y * _sigmoid(y)
    hist_ref[0:8, :] = hist_ref[chunk:chunk + 8, :]

    gates = gate_ref[...] + gb_ref[...]
    logf = _log_sigmoid(gates)
    row = lax.broadcasted_iota(jnp.int32, (chunk, chunk), 0)
    col = lax.broadcasted_iota(jnp.int32, (chunk, chunk), 1)
    tril = col <= row
    bcum = jnp.dot(tril.astype(F32), logf, preferred_element_type=F32,
                   precision=lax.Precision.HIGHEST)
    gates_t = gates.T
    bcum_t = bcum.T

    dk, dv = MLSTM_DK, MLSTM_DV
    for h in range(MLSTM_HEADS):
        q = qk[:, h * dk:(h + 1) * dk]
        k = qk[:, (MLSTM_HEADS + h) * dk:(MLSTM_HEADS + h + 1) * dk] * (dk ** -0.5)
        v = v_ref[:, h * dv:(h + 1) * dv].astype(BF16)
        b_c = bcum[:, TAIL_F + h:TAIL_F + h + 1]
        b_r = bcum_t[TAIL_F + h:TAIL_F + h + 1, :]
        i_c = gates[:, TAIL_I + h:TAIL_I + h + 1]
        i_r = gates_t[TAIL_I + h:TAIL_I + h + 1, :]
        b_last = bcum[chunk - 1:chunk, TAIL_F + h:TAIL_F + h + 1]
        m_prev = m_ref[h:h + 1, 0:1]
        n_prev = n_ref[h:h + 1, :]
        c_prev = c_ref[h]

        dmat = jnp.where(tril, b_c - b_r + i_r, NEG_INF)
        m_inter = b_c + m_prev
        m_row = jnp.maximum(m_inter, jnp.max(dmat, axis=-1, keepdims=True))
        q_b = q.astype(BF16)
        wmat = jnp.exp(dmat - m_row) * _dot_nt(q_b, k.astype(BF16))
        inter = jnp.exp(m_inter - m_row)
        num = _dot(wmat.astype(BF16), v) + inter * _dot(q_b, c_prev.astype(BF16))
        den = jnp.sum(wmat, axis=-1, keepdims=True) + inter * jnp.sum(q * n_prev, axis=-1, keepdims=True)
        hcell = num / jnp.maximum(jnp.abs(den), jnp.exp(-m_row))
        gate_o = _sigmoid(og_ref[:, h * dv:(h + 1) * dv])
        out_ref[:, h * dv:(h + 1) * dv] = (gate_o * hcell).astype(out_ref.dtype)

        g_c = b_last - b_c + i_c
        m_new = jnp.maximum(b_last + m_prev, jnp.max(g_c, axis=0, keepdims=True))
        wk = jnp.exp(g_c - m_new)
        decay = jnp.exp(b_last + m_prev - m_new)
        kw = k * wk
        c_ref[h] = decay * c_prev + _dot_tn(kw.astype(BF16), v)
        n_ref[h:h + 1, :] = decay * n_prev + jnp.sum(kw, axis=0, keepdims=True)
        m_ref[h:h + 1, :] = jnp.broadcast_to(m_new, (1, LANES))


def mlstm(proj, conv_w, conv_b, gate_bias):
    nc = SEQ // MLSTM_CHUNK
    chunk = MLSTM_CHUNK
    wqk = 2 * MLSTM_HEADS * MLSTM_DK
    wv = MLSTM_HEADS * MLSTM_DV
    row = lambda b, c: b * nc + c
    return pl.pallas_call(
        _mlstm_kernel,
        out_shape=jax.ShapeDtypeStruct((TOKENS, wv), BF16),
        grid=(BATCH, nc),
        in_specs=[
            pl.BlockSpec((chunk, wqk), lambda b, c: (row(b, c), EV_MQK // wqk)),
            pl.BlockSpec((chunk, wv), lambda b, c: (row(b, c), EV_MV // wv)),
            pl.BlockSpec((chunk, wv), lambda b, c: (row(b, c), EV_MO // wv)),
            pl.BlockSpec((chunk, LANES), lambda b, c: (row(b, c), EV_TAIL // LANES)),
            pl.BlockSpec((CONV_WIDTH, wqk), lambda b, c: (0, 0)),
            pl.BlockSpec((1, wqk), lambda b, c: (0, 0)),
            pl.BlockSpec((1, LANES), lambda b, c: (0, 0)),
        ],
        out_specs=pl.BlockSpec((chunk, wv), lambda b, c: (row(b, c), 0)),
        scratch_shapes=[
            pltpu.VMEM((chunk + 8, wqk), F32),
            pltpu.VMEM((MLSTM_HEADS, MLSTM_DK, MLSTM_DV), F32),
            pltpu.VMEM((8, MLSTM_DK), F32),
            pltpu.VMEM((8, LANES), F32),
        ],
        compiler_params=_params(("parallel", "arbitrary")),
        name="mlstm",
    )(proj, proj, proj, proj, conv_w, conv_b.reshape(1, wqk), gate_bias)


def _dilated_kernel(*refs, nres, nblk, has_prev, span, heads):
    if has_prev:
        q_ref, kp_ref, kc_ref, vp_ref, vc_ref, o_ref, lse_ref = refs
    else:
        q_ref, kc_ref, vc_ref, o_ref, lse_ref = refs
        kp_ref = vp_ref = None
    n = pl.program_id(2)
    hg = pl.program_id(3)
    blk_sz = DIL_BLOCK
    dh = DIL_HEAD_DIM
    nkeys = 2 * blk_sz if has_prev else blk_sz
    qi = lax.broadcasted_iota(jnp.int32, (blk_sz, nkeys), 0)
    kj = lax.broadcasted_iota(jnp.int32, (blk_sz, nkeys), 1)
    dist = (nkeys - blk_sz) + qi - kj
    bias_full = jnp.where(dist >= 0, jnp.where(dist <= span, 0.0, NEG_INF), NEG_INF)
    bias_first = jnp.where(kj >= nkeys - blk_sz, bias_full, NEG_INF)
    lane = lax.broadcasted_iota(jnp.int32, (blk_sz, LANES), 1)

    @pl.when(hg == 0)
    def _():
        lse_ref[...] = jnp.zeros(lse_ref.shape, F32)

    def stack(ref, res, rows):
        return jnp.stack([ref[res, rows, h * dh:(h + 1) * dh] for h in range(heads)])

    for res in range(nres):
        for blk in range(nblk):
            rows = slice(blk * blk_sz, (blk + 1) * blk_sz)
            q3 = stack(q_ref, res, rows)
            k3 = stack(kc_ref, res, rows)
            v3 = stack(vc_ref, res, rows)
            bias = bias_full
            if has_prev:
                if blk == 0:
                    first = slice(0, blk_sz)
                    kp3, vp3 = stack(kp_ref, res, first), stack(vp_ref, res, first)
                    bias = jnp.where(n > 0, bias_full, bias_first)
                else:
                    prows = slice((blk - 1) * blk_sz, blk * blk_sz)
                    kp3, vp3 = stack(kc_ref, res, prows), stack(vc_ref, res, prows)
                k3 = jnp.concatenate([kp3, k3], axis=1)
                v3 = jnp.concatenate([vp3, v3], axis=1)
            s = jnp.einsum("hqd,hkd->hqk", q3, k3, preferred_element_type=F32) + bias[None]
            m = jnp.max(s, axis=-1, keepdims=True)
            p = jnp.exp2(s - m)
            den = jnp.sum(p, axis=-1, keepdims=True)
            o = jnp.einsum("hqk,hkd->hqd", p.astype(BF16), v3, preferred_element_type=F32) / den
            lse = m * LN2 + jnp.log(den)
            lse_tile = lse_ref[res, rows, :]
            for h in range(heads):
                o_ref[res, rows, h * dh:(h + 1) * dh] = o[h].astype(o_ref.dtype)
                lse_tile = jnp.where(lane == hg * heads + h, lse[h], lse_tile)
            lse_ref[res, rows, :] = lse_tile


def dilated_pattern(qkv_perm, window, dilation, *, hw, units):
    span = window // dilation
    length = SEQ // dilation
    nblk = min(length // DIL_BLOCK, units)
    nres = min(dilation, units // nblk)
    tl = nblk * DIL_BLOCK
    ntile = length // tl
    has_prev = length > DIL_BLOCK
    heads = hw // DIL_HEAD_DIM
    ngroups = ODD_MIX // hw

    def cur(which):
        return pl.BlockSpec((None, nres, tl, hw), lambda b, r, n, g: (b, r, n, which * ngroups + g))

    def prev(which):
        return pl.BlockSpec((None, nres, DIL_BLOCK, hw),
                            lambda b, r, n, g: (b, r, jnp.maximum(n * nblk - 1, 0), which * ngroups + g))

    if has_prev:
        in_specs = [cur(0), prev(1), cur(1), prev(2), cur(2)]
    else:
        in_specs = [cur(0), cur(1), cur(2)]
    return pl.pallas_call(
        functools.partial(_dilated_kernel, nres=nres, nblk=nblk, has_prev=has_prev, span=span, heads=heads),
        out_shape=(jax.ShapeDtypeStruct((BATCH, dilation, length, ODD_MIX), BF16),
                   jax.ShapeDtypeStruct((BATCH, dilation, length, LANES), F32)),
        grid=(BATCH, dilation // nres, ntile, ngroups),
        in_specs=in_specs,
        out_specs=(pl.BlockSpec((None, nres, tl, hw), lambda b, r, n, g: (b, r, n, g)),
                   pl.BlockSpec((None, nres, tl, LANES), lambda b, r, n, g: (b, r, n, 0))),
        compiler_params=_params(("parallel", "parallel", "arbitrary", "arbitrary")),
        name="dilated_d%d" % dilation,
    )(*([qkv_perm] * len(in_specs)))


def _merge_kernel(o1_ref, o4_ref, o16_ref, l1_ref, l4_ref, l16_ref, out_ref,
                  nat_scr, p4_scr, o4_scr, o_scr, *, tm):
    dh = DIL_HEAD_DIM
    q4, q16 = tm // 4, tm // 16

    for r4 in range(4):
        for a in range(4):
            p4_scr[r4, pl.ds(a, q16, stride=4), :] = l16_ref[r4 + 4 * a]
    for r4 in range(4):
        nat_scr[0, pl.ds(r4, q4, stride=4), :] = l4_ref[r4]
        nat_scr[1, pl.ds(r4, q4, stride=4), :] = p4_scr[r4]
    l1, l4, l16 = l1_ref[0], nat_scr[0], nat_scr[1]
    mx = jnp.maximum(jnp.maximum(l1, l4), l16)
    e1, e4, e16 = jnp.exp(l1 - mx), jnp.exp(l4 - mx), jnp.exp(l16 - mx)
    tot = e1 + e4 + e16
    w1 = e1 / tot
    nat_scr[0] = e4 / tot
    nat_scr[1] = e16 / tot
    w4 = jnp.concatenate([nat_scr[0, pl.ds(r4, q4, stride=4), :] for r4 in range(4)], axis=0)
    for r4 in range(4):
        p4_scr[r4] = nat_scr[1, pl.ds(r4, q4, stride=4), :]
    w16_parts = [None] * 16
    for r4 in range(4):
        for a in range(4):
            w16_parts[r4 + 4 * a] = p4_scr[r4, pl.ds(a, q16, stride=4), :]
    w16 = jnp.concatenate(w16_parts, axis=0)

    head_of_col = lax.broadcasted_iota(jnp.int32, (2 * LANES, ODD_MIX), 1) // dh
    lane_of_row = lax.broadcasted_iota(jnp.int32, (2 * LANES, ODD_MIX), 0) % LANES
    expand = jnp.where(head_of_col == lane_of_row, 1.0, 0.0).astype(BF16)

    def two_terms(w):
        hi = w.astype(BF16)
        return jnp.concatenate([hi, (w - hi.astype(F32)).astype(BF16)], axis=1)

    w1, w4, w16 = two_terms(w1), two_terms(w4), two_terms(w16)

    pair = 2
    for hp in range(DIL_HEADS // pair):
        spread = expand[:, hp * pair * dh:(hp + 1) * pair * dh]
        w1p, w4p, w16p = _dot(w1, spread), _dot(w4, spread), _dot(w16, spread)
        for hh in range(pair):
            cols = slice((hp * pair + hh) * dh, (hp * pair + hh + 1) * dh)
            part = slice(hh * dh, (hh + 1) * dh)
            for r4 in range(4):
                for a in range(4):
                    r = r4 + 4 * a
                    o4_scr[r4, pl.ds(a, q16, stride=4), :] = (w16p[r * q16:(r + 1) * q16, part]
                                                             * o16_ref[r, :, cols])
            for r4 in range(4):
                acc4 = o4_scr[r4] + w4p[r4 * q4:(r4 + 1) * q4, part] * o4_ref[r4, :, cols]
                o_scr[pl.ds(r4, q4, stride=4), :] = acc4
            out_ref[:, cols] = (o_scr[...] + w1p[:, part] * o1_ref[0, :, cols]).astype(out_ref.dtype)


def merge_patterns(outs, lses, dilations, *, tm):
    assert tuple(dilations) == (1, 4, 16), "the re-interleave is written as two stride-4 passes"
    tiles = SEQ // tm

    def spec(d, width):
        return pl.BlockSpec((None, d, tm // d, width), lambda i: (i // tiles, 0, i % tiles, 0))

    return pl.pallas_call(
        functools.partial(_merge_kernel, tm=tm),
        out_shape=jax.ShapeDtypeStruct((TOKENS, ODD_MIX), BF16),
        grid=(TOKENS // tm,),
        in_specs=[spec(d, ODD_MIX) for d in dilations] + [spec(d, LANES) for d in dilations],
        out_specs=pl.BlockSpec((tm, ODD_MIX), lambda i: (i, 0)),
        scratch_shapes=[pltpu.VMEM((2, tm, LANES), F32),
                        pltpu.VMEM((4, tm // 4, LANES), F32),
                        pltpu.VMEM((4, tm // 4, DIL_HEAD_DIM), F32),
                        pltpu.VMEM((tm, DIL_HEAD_DIM), F32)],
        compiler_params=_params(("parallel",)),
        name="merge_patterns",
    )(*outs, *lses)


def _rope_tables():
    pos = jnp.arange(SEQ, dtype=F32)[:, None]
    half = MLA_ROPE // 2
    inv = ROPE_THETA ** (-jnp.arange(half, dtype=F32) * 2.0 / MLA_ROPE)
    ang = pos * inv[None, :]
    c, s = jnp.cos(ang), jnp.sin(ang)
    z = jnp.zeros_like(c)
    small = jnp.concatenate([c, c, z, z, -s, z, z, z, z, s, z, z], axis=-1)
    half = DIL_HEAD_DIM // 2
    inv = ROPE_THETA ** (-jnp.arange(half, dtype=F32) * 2.0 / DIL_HEAD_DIM)
    ang = pos * inv[None, :]
    c, s = jnp.cos(ang), jnp.sin(ang)
    full = jnp.concatenate([c, c, -s, s], axis=-1)
    return small, full


def _even_weights(w_in, w_uq, w_ukv, b_i, b_f):
    cuts = np.cumsum((MLA_Q_LORA, MLA_KV_LORA, MLA_ROPE, 2 * MLSTM_HEADS * MLSTM_DK,
                      MLSTM_HEADS * MLSTM_DV, MLSTM_HEADS, MLSTM_HEADS))
    c_q, c_kv, k_r, m_qk, m_v, m_i, m_f, m_o = jnp.split(w_in, cuts.tolist(), axis=1)
    pad = jnp.zeros((D_MODEL, EV_PROJ - EV_TAIL - MLA_ROPE - 2 * MLSTM_HEADS), w_in.dtype)
    w_in_r = jnp.concatenate([c_q, c_kv, m_qk, m_v, m_o, k_r, m_i, m_f, pad], axis=1).astype(BF16)
    uq = w_uq.reshape(MLA_Q_LORA, MLA_HEADS, MLA_QK)
    uq = jnp.pad(uq, ((0, 0), (0, 0), (0, 2 * LANES - MLA_QK)))
    w_uq_r = uq.reshape(MLA_Q_LORA, MLA_HEADS * 2 * LANES).astype(BF16)
    gate_bias = jnp.concatenate([jnp.zeros((TAIL_I,), F32), b_i.astype(F32), b_f.astype(F32),
                                 jnp.zeros((LANES - TAIL_F - MLSTM_HEADS,), F32)]).reshape(1, LANES)
    return w_in_r, w_uq_r, w_ukv.astype(BF16), gate_bias


def _even_layer(x, xn, w_in, q_norm, w_uq, kv_norm, w_ukv, conv_w, conv_b, b_i, b_f, w_out, small_tab, casts):
    w_in_r, w_uq_r, w_ukv_r, gate_bias = _even_weights(w_in, w_uq, w_ukv, b_i, b_f)
    proj = matmul_weight_stationary(xn, w_in_r, tm=512, tn=EV_PROJ // 3, out_dtype=F32)
    scale = MLA_QK ** -0.5 * LOG2E
    q = rms_matmul(proj, EV_CQ // MLA_Q_LORA, MLA_Q_LORA, q_norm, w_uq_r, tm=512, tn=2048,
                   out_dtype=BF16, mode="mlaq", tab=small_tab * scale, scale=scale)
    kv = rms_matmul(proj, EV_CKV // MLA_KV_LORA, MLA_KV_LORA, kv_norm, w_ukv_r, tm=512, tn=2048,
                    out_dtype=BF16)
    kr = krope(proj, small_tab, tm=1024)
    a_out, cast_weights = mla_attention(q, kv, kr, casts, tq=512, heads=2)
    hm = mlstm(proj, conv_w, conv_b, gate_bias)
    w_out_b = w_out.astype(BF16)
    n_a = MLA_HEADS * MLA_V
    return matmul_residual(x, [(a_out, w_out_b[:n_a]), (hm, w_out_b[n_a:])], tm=512), cast_weights


def _odd_layer(x, xn, w_qkv_b, w_out_b, full_tab):
    scale = DIL_HEAD_DIM ** -0.5 * LOG2E
    identity = jnp.concatenate([jnp.ones((SEQ, LANES), F32), jnp.zeros((SEQ, LANES), F32)], axis=-1)
    tab = jnp.stack([full_tab * scale, full_tab, identity])
    dilations = [d for _, d in DIL_PATTERNS]
    qkv_perms = qkv_projection(xn, w_qkv_b, tab, dilations, tm=512, tn=1024)
    outs, lses = [], []
    for (window, dilation), qkv_perm in zip(DIL_PATTERNS, qkv_perms):
        o_g, lse_g = dilated_pattern(qkv_perm, window, dilation, hw=1024, units=4)
        outs.append(o_g)
        lses.append(lse_g)
    o = merge_patterns(outs, lses, dilations, tm=512)
    return matmul_residual(x, [(o, w_out_b)], tm=512)


def kernel(x, norm_mix, norm_mlp, ev_w_in, mla_q_norm, mla_w_uq, mla_kv_norm, mla_w_ukv,
           mlstm_conv_w, mlstm_conv_b, mlstm_b_i, mlstm_b_f, ev_w_out, od_w_qkv, od_w_out,
           mlp_w1, mlp_w2, norm_final):
    assert x.shape == (BATCH, SEQ, D_MODEL) and x.dtype == F32
    small_tab, full_tab = _rope_tables()
    xt = x.reshape(TOKENS, D_MODEL)
    xn = rms_norm_bf16(xt, norm_mix[0], tm=512)
    w1b = w2b = w_qkv_b = w_od_out_b = None
    for layer in range(DEPTH):
        i = layer // 2
        if layer % 2 == 0:
            casts = [(mlp_w1, layer, min(2, DEPTH - layer)), (mlp_w2, layer, min(2, DEPTH - layer))]
            if layer + 1 < DEPTH:
                casts += [(od_w_qkv, i, 1), (od_w_out, i, 1)]
            xt, cast_weights = _even_layer(xt, xn, ev_w_in[i], mla_q_norm[i], mla_w_uq[i], mla_kv_norm[i],
                                           mla_w_ukv[i], mlstm_conv_w[i], mlstm_conv_b[i], mlstm_b_i[i],
                                           mlstm_b_f[i], ev_w_out[i], small_tab, casts)
            w1b, w2b = cast_weights[0], cast_weights[1]
            if layer + 1 < DEPTH:
                w_qkv_b, w_od_out_b = cast_weights[2][0], cast_weights[3][0]
        else:
            xt = _odd_layer(xt, xn, w_qkv_b, w_od_out_b, full_tab)
        if layer == DEPTH - 1:
            (xt,) = mlp_block(xt, norm_mlp[layer], w1b, w2b, layer % 2, norm_final, tm=512, tf=1024,
                              tail="final")
        else:
            xt, xn = mlp_block(xt, norm_mlp[layer], w1b, w2b, layer % 2, norm_mix[layer + 1], tm=512,
                               tf=1024, tail="next")
    return xt.reshape(BATCH, SEQ, D_MODEL)
```

```python
import functools

import jax
import jax.numpy as jnp
import numpy as np
from jax import lax
from jax.experimental import pallas as pl
from jax.experimental.pallas import tpu as pltpu

D_MODEL = 2048
BATCH = 4
SEQ = 2048
DEPTH = 4
MLA_HEADS = 8
MLA_Q_LORA = 512
MLA_KV_LORA = 512
MLA_NOPE = 128
MLA_ROPE = 64
MLA_V = 128
MLA_QK = MLA_NOPE + MLA_ROPE
MLSTM_HEADS = 4
MLSTM_DK = 128
MLSTM_DV = 256
MLSTM_CHUNK = 128
CONV_WIDTH = 4
DIL_HEADS = 16
DIL_HEAD_DIM = 128
DIL_PATTERNS = ((128, 1), (512, 4), (2048, 16))
DIL_BLOCK = 128
D_FF = 4 * D_MODEL
ROPE_THETA = 10000.0
NORM_EPS = 1e-6
ODD_MIX = DIL_HEADS * DIL_HEAD_DIM
TOKENS = BATCH * SEQ

LANES = 128
VMEM_LIMIT_BYTES = 56 * 1024 * 1024

EV_CQ, EV_CKV, EV_MQK, EV_MV, EV_MO, EV_TAIL = 0, 512, 1024, 2048, 3072, 4096
EV_PROJ = 4224
TAIL_I = MLA_ROPE
TAIL_F = MLA_ROPE + MLSTM_HEADS

F32 = jnp.float32
BF16 = jnp.bfloat16
NEG_INF = float("-inf")
LOG2E = float(np.log2(np.e))
LN2 = float(np.log(2.0))


def _params(semantics):
    return pltpu.CompilerParams(dimension_semantics=semantics, vmem_limit_bytes=VMEM_LIMIT_BYTES)


def _rms_normalize(x, g):
    ms = jnp.mean(x * x, axis=-1, keepdims=True)
    return x * lax.rsqrt(ms + NORM_EPS) * g


def _dot(a, b):
    return jnp.dot(a, b, preferred_element_type=F32)


def _dot_nt(a, b):
    return lax.dot_general(a, b, (((1,), (1,)), ((), ())), preferred_element_type=F32)


def _dot_tn(a, b):
    return lax.dot_general(a, b, (((0,), (0,)), ((), ())), preferred_element_type=F32)


def _sigmoid(x):
    return 1.0 / (1.0 + jnp.exp(-x))


def _log_sigmoid(x):
    return jnp.minimum(x, 0.0) - jnp.log1p(jnp.exp(-jnp.abs(x)))


def _rope_small(y, tab):
    return (y * tab[:, 0:LANES] + pltpu.roll(y, 96, 1) * tab[:, LANES:2 * LANES]
            + pltpu.roll(y, 32, 1) * tab[:, 2 * LANES:3 * LANES])


def _rope_full(y, tab):
    return y * tab[:, 0:LANES] + pltpu.roll(y, 64, 1) * tab[:, LANES:2 * LANES]


def _norm_to_scratch(x_ref, g_ref, xn_ref):
    @pl.when(pl.program_id(1) == 0)
    def _():
        xn_ref[...] = _rms_normalize(x_ref[...], g_ref[...]).astype(BF16)


def _rms_mm_plain_kernel(x_ref, g_ref, w_ref, o_ref, xn_ref):
    _norm_to_scratch(x_ref, g_ref, xn_ref)
    o_ref[...] = _dot(xn_ref[...], w_ref[...]).astype(o_ref.dtype)


def _rms_mm_mlaq_kernel(x_ref, g_ref, w_ref, tab_ref, o_ref, xn_ref, *, scale):
    _norm_to_scratch(x_ref, g_ref, xn_ref)
    y = _dot(xn_ref[...], w_ref[...])
    tab = tab_ref[...]
    for grp in range(y.shape[1] // LANES):
        yg = y[:, grp * LANES:(grp + 1) * LANES]
        og = yg * scale if grp % 2 == 0 else _rope_small(yg, tab)
        o_ref[:, grp * LANES:(grp + 1) * LANES] = og.astype(o_ref.dtype)


def _qkv_proj_kernel(xn_ref, w_ref, tab_ref, *rest, dilations, tm, sub_cols):
    out_refs, (scr_ref, scr4_ref) = rest[:len(dilations)], rest[len(dilations):]
    out_by_d = dict(zip(dilations, out_refs))
    tab = tab_ref[...]
    for sub in range(w_ref.shape[1] // sub_cols):
        y = _dot(xn_ref[...], w_ref[:, sub * sub_cols:(sub + 1) * sub_cols])
        for part in range(sub_cols // LANES):
            grp = sub * (sub_cols // LANES) + part
            cols = slice(grp * LANES, (grp + 1) * LANES)
            yg = _rope_full(y[:, part * LANES:(part + 1) * LANES], tab)
            out_by_d[1][0, :, cols] = yg.astype(BF16)
            scr_ref[grp] = yg
            for r4 in range(4):
                v = scr_ref[grp, pl.ds(r4, tm // 4, stride=4), :]
                out_by_d[4][r4, :, cols] = v.astype(BF16)
                scr4_ref[r4] = v
            for r4 in range(4):
                for a in range(4):
                    w = scr4_ref[r4, pl.ds(a, tm // 16, stride=4), :]
                    out_by_d[16][r4 + 4 * a, :, cols] = w.astype(BF16)


def qkv_projection(xn, w, tab, dilations, *, tm, tn):
    assert tuple(dilations) == (1, 4, 16), "the de-interleave is written as two stride-4 passes"
    t, k_dim = xn.shape
    n = w.shape[1]
    tiles = SEQ // tm
    q_tiles = ODD_MIX // tn
    return pl.pallas_call(
        functools.partial(_qkv_proj_kernel, dilations=tuple(dilations), tm=tm, sub_cols=2 * LANES),
        out_shape=tuple(jax.ShapeDtypeStruct((BATCH, d, SEQ // d, n), BF16) for d in dilations),
        grid=(n // tn, t // tm),
        in_specs=[
            pl.BlockSpec((tm, k_dim), lambda j, i: (i, 0)),
            pl.BlockSpec((k_dim, tn), lambda j, i: (0, j)),
            pl.BlockSpec((None, tm, 2 * LANES), lambda j, i: (j // q_tiles, i % tiles, 0)),
        ],
        out_specs=tuple(pl.BlockSpec((None, d, tm // d, tn), lambda j, i: (i // tiles, 0, i % tiles, j))
                        for d in dilations),
        scratch_shapes=[pltpu.VMEM((tn // LANES, tm, LANES), F32), pltpu.VMEM((4, tm // 4, LANES), F32)],
        compiler_params=_params(("parallel", "arbitrary")),
        name="qkv_projection",
    )(xn, w, tab)


def _norm_kernel(x_ref, g_ref, o_ref):
    o_ref[...] = _rms_normalize(x_ref[...], g_ref[...]).astype(o_ref.dtype)


def rms_norm_bf16(x, g, *, tm):
    t, d = x.shape
    return pl.pallas_call(
        _norm_kernel,
        out_shape=jax.ShapeDtypeStruct((t, d), BF16),
        grid=(t // tm,),
        in_specs=[pl.BlockSpec((tm, d), lambda i: (i, 0)), pl.BlockSpec((1, d), lambda i: (0, 0))],
        out_specs=pl.BlockSpec((tm, d), lambda i: (i, 0)),
        compiler_params=_params(("parallel",)),
        name="rms_norm_bf16",
    )(x, g.reshape(1, d))


def _mm_kernel(a_ref, w_ref, o_ref):
    o_ref[...] = _dot(a_ref[...], w_ref[...]).astype(o_ref.dtype)


def matmul_weight_stationary(a, w, *, tm, tn, out_dtype):
    t, k = a.shape
    n = w.shape[1]
    return pl.pallas_call(
        _mm_kernel,
        out_shape=jax.ShapeDtypeStruct((t, n), out_dtype),
        grid=(n // tn, t // tm),
        in_specs=[pl.BlockSpec((tm, k), lambda j, i: (i, 0)), pl.BlockSpec((k, tn), lambda j, i: (0, j))],
        out_specs=pl.BlockSpec((tm, tn), lambda j, i: (i, j)),
        compiler_params=_params(("parallel", "arbitrary")),
        name="matmul_weight_stationary",
    )(a, w)


def rms_matmul(x, x_col_block, k_dim, g, w, *, tm, tn, out_dtype, mode="plain", tab=None, scale=None):
    t = x.shape[0]
    n = w.shape[1]
    assert t % tm == 0 and n % tn == 0 and w.shape[0] == k_dim
    grid = (t // tm, n // tn)
    pos_tiles = SEQ // tm
    in_specs = [
        pl.BlockSpec((tm, k_dim), lambda i, j: (i, x_col_block)),
        pl.BlockSpec((1, k_dim), lambda i, j: (0, 0)),
        pl.BlockSpec((k_dim, tn), lambda i, j: (0, j)),
    ]
    args = [x, g.reshape(1, k_dim), w]
    if mode == "plain":
        body = _rms_mm_plain_kernel
    elif mode == "mlaq":
        body = functools.partial(_rms_mm_mlaq_kernel, scale=scale)
        in_specs.append(pl.BlockSpec((tm, 3 * LANES), lambda i, j: (i % pos_tiles, 0)))
        args.append(tab)
    else:
        raise ValueError(mode)
    return pl.pallas_call(
        body,
        out_shape=jax.ShapeDtypeStruct((t, n), out_dtype),
        grid=grid,
        in_specs=in_specs,
        out_specs=pl.BlockSpec((tm, tn), lambda i, j: (i, j)),
        scratch_shapes=[pltpu.VMEM((tm, k_dim), BF16)],
        compiler_params=_params(("parallel", "arbitrary")),
        name="rms_matmul_" + mode,
    )(*args)


def _krope_kernel(p_ref, tab_ref, o_ref):
    o_ref[...] = _rope_small(p_ref[...], tab_ref[...]).astype(o_ref.dtype)


def krope(proj, tab, *, tm):
    pos_tiles = SEQ // tm
    return pl.pallas_call(
        _krope_kernel,
        out_shape=jax.ShapeDtypeStruct((TOKENS, LANES), BF16),
        grid=(TOKENS // tm,),
        in_specs=[pl.BlockSpec((tm, LANES), lambda i: (i, EV_TAIL // LANES)),
                  pl.BlockSpec((tm, 3 * LANES), lambda i: (i % pos_tiles, 0))],
        out_specs=pl.BlockSpec((tm, LANES), lambda i: (i, 0)),
        compiler_params=_params(("parallel",)),
        name="krope",
    )(proj, tab)


def _mm_res_kernel(*refs):
    r_ref, o_ref = refs[-2], refs[-1]
    acc = r_ref[...]
    for k in range((len(refs) - 2) // 2):
        acc = acc + _dot(refs[2 * k][...], refs[2 * k + 1][...])
    o_ref[...] = acc


def matmul_residual(res, pairs, *, tm):
    t, n = res.shape
    in_specs, args = [], []
    for a, w, row_block in pairs:
        k = a.shape[1]
        in_specs += [pl.BlockSpec((tm, k), lambda i: (i, 0)),
                     pl.BlockSpec((k, n), lambda i, row_block=row_block: (row_block, 0))]
        args += [a, w]
    in_specs.append(pl.BlockSpec((tm, n), lambda i: (i, 0)))
    args.append(res)
    return pl.pallas_call(
        _mm_res_kernel,
        out_shape=jax.ShapeDtypeStruct((t, n), F32),
        grid=(t // tm,),
        in_specs=in_specs,
        out_specs=pl.BlockSpec((tm, n), lambda i: (i, 0)),
        compiler_params=_params(("parallel",)),
        name="matmul_residual",
    )(*args)


def _side_cast_specs(casts, steps, step_of):
    in_specs, out_specs, out_shapes = [], [], []
    for w, first, count in casts:
        _, rows, cols = w.shape
        slabs = steps // count
        assert steps % count == 0 and rows % slabs == 0 and (rows // slabs) % 16 == 0
        block = (None, rows // slabs, cols)
        in_specs.append(pl.BlockSpec(
            block, lambda *idx, first=first, slabs=slabs: (first + step_of(*idx) // slabs,
                                                           step_of(*idx) % slabs, 0)))
        out_specs.append(pl.BlockSpec(
            block, lambda *idx, slabs=slabs: (step_of(*idx) // slabs, step_of(*idx) % slabs, 0)))
        out_shapes.append(jax.ShapeDtypeStruct((count, rows, cols), BF16))
    return in_specs, out_specs, out_shapes


def _mlp_kernel(x_ref, g_ref, w1_ref, w2_ref, gn_ref, *rest, tail, ncast):
    cast_in, o_ref, xn_ref = rest[:ncast], rest[ncast], rest[-1]
    cast_out = rest[len(rest) - 1 - ncast:len(rest) - 1]
    f = pl.program_id(1)

    @pl.when(f == 0)
    def _():
        x = x_ref[...]
        xn_ref[...] = _rms_normalize(x, g_ref[...]).astype(BF16)
        o_ref[...] = x

    for src, dst in zip(cast_in, cast_out):
        dst[...] = src[...].astype(dst.dtype)
    h = jnp.maximum(_dot(xn_ref[...], w1_ref[...]), 0.0)
    o_ref[...] += _dot((h * h).astype(BF16), w2_ref[...])

    @pl.when(f == pl.num_programs(1) - 1)
    def _():
        normed = _rms_normalize(o_ref[...], gn_ref[...])
        if tail == "final":
            o_ref[...] = normed
        else:
            rest[ncast + 1][...] = normed.astype(BF16)


def mlp_block(x, g, w1b, w2b, layer, gn, casts, *, tm, tf, tail):
    t, d = x.shape
    dff = w1b.shape[2]
    nf = dff // tf
    cast_in_specs, cast_out_specs, cast_shapes = _side_cast_specs(casts, (t // tm) * nf, lambda i, f: i * nf + f)
    out_shape = [jax.ShapeDtypeStruct((t, d), F32)]
    out_specs = [pl.BlockSpec((tm, d), lambda i, f: (i, 0))]
    if tail == "next":
        out_shape.append(jax.ShapeDtypeStruct((t, d), BF16))
        out_specs.append(pl.BlockSpec((tm, d), lambda i, f: (i, 0)))
    n_main = len(out_shape)
    outs = pl.pallas_call(
        functools.partial(_mlp_kernel, tail=tail, ncast=len(casts)),
        out_shape=(*out_shape, *cast_shapes),
        grid=(t // tm, nf),
        in_specs=[
            pl.BlockSpec((tm, d), lambda i, f: (i, 0)),
            pl.BlockSpec((1, d), lambda i, f: (0, 0)),
            pl.BlockSpec((None, d, tf), lambda i, f: (layer, 0, f)),
            pl.BlockSpec((None, tf, d), lambda i, f: (layer, f, 0)),
            pl.BlockSpec((1, d), lambda i, f: (0, 0)),
            *cast_in_specs,
        ],
        out_specs=(*out_specs, *cast_out_specs),
        scratch_shapes=[pltpu.VMEM((tm, d), BF16)],
        compiler_params=_params(("parallel", "arbitrary")),
        name="mlp_block",
    )(x, g.reshape(1, d), w1b, w2b, gn.reshape(1, d), *[w for w, _, _ in casts])
    return outs[:n_main], outs[n_main:]


def _mla_attn_kernel(q_ref, kv_ref, kr_ref, *rest, tq, heads, ncast):
    cast_in, o_ref, cast_out = rest[:ncast], rest[ncast], rest[ncast + 1:2 * ncast + 1]
    m_ref, l_ref, acc_ref = rest[2 * ncast + 1:]
    qi = pl.program_id(2)
    hw = 2 * LANES
    m_ref[...] = jnp.full(m_ref.shape, NEG_INF, F32)
    l_ref[...] = jnp.zeros(l_ref.shape, F32)
    acc_ref[...] = jnp.zeros(acc_ref.shape, F32)

    def block(ki, masked):
        start = pl.multiple_of(ki * tq, tq)
        kr = kr_ref[pl.ds(start, tq), :]
        for h in range(heads):
            q = q_ref[:, h * hw:(h + 1) * hw]
            kn = kv_ref[pl.ds(start, tq), h * hw:h * hw + MLA_NOPE]
            v = kv_ref[pl.ds(start, tq), h * hw + MLA_NOPE:(h + 1) * hw]
            s = _dot_nt(q, jnp.concatenate([kn, kr], axis=-1))
            if masked:
                row = lax.broadcasted_iota(jnp.int32, s.shape, 0)
                col = lax.broadcasted_iota(jnp.int32, s.shape, 1)
                s = jnp.where(col <= row, s, NEG_INF)
            m_prev = m_ref[h]
            m_new = jnp.maximum(m_prev, jnp.max(s, axis=-1, keepdims=True))
            p = jnp.exp2(s - jnp.tile(m_new, (1, tq // LANES)))
            alpha = jnp.exp2(m_prev - m_new)
            l_ref[h] = alpha * l_ref[h] + jnp.sum(p, axis=-1, keepdims=True)
            acc_ref[h] = alpha * acc_ref[h] + _dot(p.astype(BF16), v)
            m_ref[h] = m_new

    def body(ki, carry):
        block(ki, False)
        return carry

    lax.fori_loop(0, qi, body, 0)
    for src, dst in zip(cast_in, cast_out):
        dst[...] = src[...].astype(dst.dtype)
    block(qi, True)
    for h in range(heads):
        o_ref[:, h * MLA_V:(h + 1) * MLA_V] = (acc_ref[h] / l_ref[h]).astype(o_ref.dtype)


def mla_attention(q, kv, kr, casts, *, tq, heads):
    nq = SEQ // tq
    hw = 2 * LANES * heads
    groups = MLA_HEADS // heads
    steps = BATCH * groups * nq

    cast_in_specs, cast_out_specs, cast_shapes = _side_cast_specs(
        casts, steps, lambda b, h, i: (b * groups + h) * nq + i)

    outs = pl.pallas_call(
        functools.partial(_mla_attn_kernel, tq=tq, heads=heads, ncast=len(casts)),
        out_shape=(jax.ShapeDtypeStruct((TOKENS, MLA_HEADS * MLA_V), BF16), *cast_shapes),
        grid=(BATCH, groups, nq),
        in_specs=[
            pl.BlockSpec((tq, hw), lambda b, h, i: (b * nq + i, h)),
            pl.BlockSpec((SEQ, hw), lambda b, h, i: (b, h)),
            pl.BlockSpec((SEQ, LANES), lambda b, h, i: (b, 0)),
            *cast_in_specs,
        ],
        out_specs=(pl.BlockSpec((tq, heads * MLA_V), lambda b, h, i: (b * nq + i, h)), *cast_out_specs),
        scratch_shapes=[pltpu.VMEM((heads, tq, LANES), F32), pltpu.VMEM((heads, tq, LANES), F32),
                        pltpu.VMEM((heads, tq, MLA_V), F32)],
        compiler_params=_params(("parallel", "parallel", "arbitrary")),
        name="mla_attention",
    )(q, kv, kr, *[w for w, _, _ in casts])
    return outs[0], outs[1:]


def _mlstm_kernel(qk_ref, v_ref, og_ref, gate_ref, cw_ref, cb_ref, gb_ref, out_ref,
                  hist_ref, c_ref, n_ref, m_ref):
    chunk = MLSTM_CHUNK
    c = pl.program_id(1)

    @pl.when(c == 0)
    def _():
        hist_ref[0:8, :] = jnp.zeros((8, hist_ref.shape[1]), F32)
        c_ref[...] = jnp.zeros(c_ref.shape, F32)
        n_ref[...] = jnp.zeros(n_ref.shape, F32)
        m_ref[...] = jnp.zeros(m_ref.shape, F32)

    hist_ref[8:8 + chunk, :] = qk_ref[...]
    y = jnp.broadcast_to(cb_ref[...], (chunk, hist_ref.shape[1]))
    for j in range(CONV_WIDTH):
        off = 8 - (CONV_WIDTH - 1) + j
        y = y + cw_ref[j:j + 1, :] * hist_ref[off:off + chunk, :]
    qk = y * _sigmoid(y)
    hist_ref[0:8, :] = hist_ref[chunk:chunk + 8, :]

    gates = gate_ref[...] + gb_ref[...]
    logf = _log_sigmoid(gates)
    row = lax.broadcasted_iota(jnp.int32, (chunk, chunk), 0)
    col = lax.broadcasted_iota(jnp.int32, (chunk, chunk), 1)
    tril = col <= row
    bcum = jnp.dot(tril.astype(F32), logf, preferred_element_type=F32,
                   precision=lax.Precision.HIGHEST)
    gates_t = gates.T
    bcum_t = bcum.T

    dk, dv = MLSTM_DK, MLSTM_DV
    for h in range(MLSTM_HEADS):
        q = qk[:, h * dk:(h + 1) * dk]
        k = qk[:, (MLSTM_HEADS + h) * dk:(MLSTM_HEADS + h + 1) * dk] * (dk ** -0.5)
        v = v_ref[:, h * dv:(h + 1) * dv].astype(BF16)
        b_c = bcum[:, TAIL_F + h:TAIL_F + h + 1]
        b_r = bcum_t[TAIL_F + h:TAIL_F + h + 1, :]
        i_c = gates[:, TAIL_I + h:TAIL_I + h + 1]
        i_r = gates_t[TAIL_I + h:TAIL_I + h + 1, :]
        b_last = bcum[chunk - 1:chunk, TAIL_F + h:TAIL_F + h + 1]
        m_prev = m_ref[h:h + 1, 0:1]
        n_prev = n_ref[h:h + 1, :]
        c_prev = c_ref[h]

        dmat = jnp.where(tril, b_c - b_r + i_r, NEG_INF)
        m_inter = b_c + m_prev
        m_row = jnp.maximum(m_inter, jnp.max(dmat, axis=-1, keepdims=True))
        q_b = q.astype(BF16)
        wmat = jnp.exp(dmat - m_row) * _dot_nt(q_b, k.astype(BF16))
        inter = jnp.exp(m_inter - m_row)
        num = _dot(wmat.astype(BF16), v) + inter * _dot(q_b, c_prev.astype(BF16))
        den = jnp.sum(wmat, axis=-1, keepdims=True) + inter * jnp.sum(q * n_prev, axis=-1, keepdims=True)
        hcell = num / jnp.maximum(jnp.abs(den), jnp.exp(-m_row))
        gate_o = _sigmoid(og_ref[:, h * dv:(h + 1) * dv])
        out_ref[:, h * dv:(h + 1) * dv] = (gate_o * hcell).astype(out_ref.dtype)

        g_c = b_last - b_c + i_c
        m_new = jnp.maximum(b_last + m_prev, jnp.max(g_c, axis=0, keepdims=True))
        wk = jnp.exp(g_c - m_new)
        decay = jnp.exp(b_last + m_prev - m_new)
        kw = k * wk
        c_ref[h] = decay * c_prev + _dot_tn(kw.astype(BF16), v)
        n_ref[h:h + 1, :] = decay * n_prev + jnp.sum(kw, axis=0, keepdims=True)
        m_ref[h:h + 1, :] = jnp.broadcast_to(m_new, (1, LANES))


def mlstm(proj, conv_w, conv_b, gate_bias):
    nc = SEQ // MLSTM_CHUNK
    chunk = MLSTM_CHUNK
    wqk = 2 * MLSTM_HEADS * MLSTM_DK
    wv = MLSTM_HEADS * MLSTM_DV
    row = lambda b, c: b * nc + c
    return pl.pallas_call(
        _mlstm_kernel,
        out_shape=jax.ShapeDtypeStruct((TOKENS, wv), BF16),
        grid=(BATCH, nc),
        in_specs=[
            pl.BlockSpec((chunk, wqk), lambda b, c: (row(b, c), EV_MQK // wqk)),
            pl.BlockSpec((chunk, wv), lambda b, c: (row(b, c), EV_MV // wv)),
            pl.BlockSpec((chunk, wv), lambda b, c: (row(b, c), EV_MO // wv)),
            pl.BlockSpec((chunk, LANES), lambda b, c: (row(b, c), EV_TAIL // LANES)),
            pl.BlockSpec((CONV_WIDTH, wqk), lambda b, c: (0, 0)),
            pl.BlockSpec((1, wqk), lambda b, c: (0, 0)),
            pl.BlockSpec((1, LANES), lambda b, c: (0, 0)),
        ],
        out_specs=pl.BlockSpec((chunk, wv), lambda b, c: (row(b, c), 0)),
        scratch_shapes=[
            pltpu.VMEM((chunk + 8, wqk), F32),
            pltpu.VMEM((MLSTM_HEADS, MLSTM_DK, MLSTM_DV), F32),
            pltpu.VMEM((8, MLSTM_DK), F32),
            pltpu.VMEM((8, LANES), F32),
        ],
        compiler_params=_params(("parallel", "arbitrary")),
        name="mlstm",
    )(proj, proj, proj, proj, conv_w, conv_b.reshape(1, wqk), gate_bias)


def _dilated_kernel(*refs, nres, nblk, has_prev, span, heads):
    if has_prev:
        q_ref, kp_ref, kc_ref, vp_ref, vc_ref, o_ref, lse_ref = refs
    else:
        q_ref, kc_ref, vc_ref, o_ref, lse_ref = refs
        kp_ref = vp_ref = None
    n = pl.program_id(2)
    hg = pl.program_id(3)
    blk_sz = DIL_BLOCK
    dh = DIL_HEAD_DIM
    nkeys = 2 * blk_sz if has_prev else blk_sz
    qi = lax.broadcasted_iota(jnp.int32, (blk_sz, nkeys), 0)
    kj = lax.broadcasted_iota(jnp.int32, (blk_sz, nkeys), 1)
    dist = (nkeys - blk_sz) + qi - kj
    bias_full = jnp.where(dist >= 0, jnp.where(dist <= span, 0.0, NEG_INF), NEG_INF)
    bias_first = jnp.where(kj >= nkeys - blk_sz, bias_full, NEG_INF)
    lane = lax.broadcasted_iota(jnp.int32, (blk_sz, LANES), 1)

    @pl.when(hg == 0)
    def _():
        lse_ref[...] = jnp.zeros(lse_ref.shape, F32)

    def stack(ref, res, rows):
        return jnp.stack([ref[res, rows, h * dh:(h + 1) * dh] for h in range(heads)])

    for res in range(nres):
        for blk in range(nblk):
            rows = slice(blk * blk_sz, (blk + 1) * blk_sz)
            q3 = stack(q_ref, res, rows)
            k3 = stack(kc_ref, res, rows)
            v3 = stack(vc_ref, res, rows)
            bias = bias_full
            if has_prev:
                if blk == 0:
                    first = slice(0, blk_sz)
                    kp3, vp3 = stack(kp_ref, res, first), stack(vp_ref, res, first)
                    bias = jnp.where(n > 0, bias_full, bias_first)
                else:
                    prows = slice((blk - 1) * blk_sz, blk * blk_sz)
                    kp3, vp3 = stack(kc_ref, res, prows), stack(vc_ref, res, prows)
                k3 = jnp.concatenate([kp3, k3], axis=1)
                v3 = jnp.concatenate([vp3, v3], axis=1)
            s = jnp.einsum("hqd,hkd->hqk", q3, k3, preferred_element_type=F32) + bias[None]
            m = jnp.max(s, axis=-1, keepdims=True)
            p = jnp.exp2(s - m)
            den = jnp.sum(p, axis=-1, keepdims=True)
            o = jnp.einsum("hqk,hkd->hqd", p.astype(BF16), v3, preferred_element_type=F32) / den
            lse = m * LN2 + jnp.log(den)
            lse_tile = lse_ref[res, rows, :]
            for h in range(heads):
                o_ref[res, rows, h * dh:(h + 1) * dh] = o[h].astype(o_ref.dtype)
                lse_tile = jnp.where(lane == hg * heads + h, lse[h], lse_tile)
            lse_ref[res, rows, :] = lse_tile


def dilated_pattern(qkv_perm, window, dilation, *, hw, units):
    span = window // dilation
    length = SEQ // dilation
    nblk = min(length // DIL_BLOCK, units)
    nres = min(dilation, units // nblk)
    tl = nblk * DIL_BLOCK
    ntile = length // tl
    has_prev = length > DIL_BLOCK
    heads = hw // DIL_HEAD_DIM
    ngroups = ODD_MIX // hw

    def cur(which):
        return pl.BlockSpec((None, nres, tl, hw), lambda b, r, n, g: (b, r, n, which * ngroups + g))

    def prev(which):
        return pl.BlockSpec((None, nres, DIL_BLOCK, hw),
                            lambda b, r, n, g: (b, r, jnp.maximum(n * nblk - 1, 0), which * ngroups + g))

    if has_prev:
        in_specs = [cur(0), prev(1), cur(1), prev(2), cur(2)]
    else:
        in_specs = [cur(0), cur(1), cur(2)]
    return pl.pallas_call(
        functools.partial(_dilated_kernel, nres=nres, nblk=nblk, has_prev=has_prev, span=span, heads=heads),
        out_shape=(jax.ShapeDtypeStruct((BATCH, dilation, length, ODD_MIX), BF16),
                   jax.ShapeDtypeStruct((BATCH, dilation, length, LANES), F32)),
        grid=(BATCH, dilation // nres, ntile, ngroups),
        in_specs=in_specs,
        out_specs=(pl.BlockSpec((None, nres, tl, hw), lambda b, r, n, g: (b, r, n, g)),
                   pl.BlockSpec((None, nres, tl, LANES), lambda b, r, n, g: (b, r, n, 0))),
        compiler_params=_params(("parallel", "parallel", "arbitrary", "arbitrary")),
        name="dilated_d%d" % dilation,
    )(*([qkv_perm] * len(in_specs)))


def _merge_kernel(o1_ref, o4_ref, o16_ref, l1_ref, l4_ref, l16_ref, out_ref,
                  nat_scr, p4_scr, o4_scr, o_scr, *, tm):
    dh = DIL_HEAD_DIM
    q4, q16 = tm // 4, tm // 16

    for r4 in range(4):
        for a in range(4):
            p4_scr[r4, pl.ds(a, q16, stride=4), :] = l16_ref[r4 + 4 * a]
    for r4 in range(4):
        nat_scr[0, pl.ds(r4, q4, stride=4), :] = l4_ref[r4]
        nat_scr[1, pl.ds(r4, q4, stride=4), :] = p4_scr[r4]
    l1, l4, l16 = l1_ref[0], nat_scr[0], nat_scr[1]
    mx = jnp.maximum(jnp.maximum(l1, l4), l16)
    e1, e4, e16 = jnp.exp(l1 - mx), jnp.exp(l4 - mx), jnp.exp(l16 - mx)
    tot = e1 + e4 + e16
    w1 = e1 / tot
    nat_scr[0] = e4 / tot
    nat_scr[1] = e16 / tot
    w4 = jnp.concatenate([nat_scr[0, pl.ds(r4, q4, stride=4), :] for r4 in range(4)], axis=0)
    for r4 in range(4):
        p4_scr[r4] = nat_scr[1, pl.ds(r4, q4, stride=4), :]
    w16_parts = [None] * 16
    for r4 in range(4):
        for a in range(4):
            w16_parts[r4 + 4 * a] = p4_scr[r4, pl.ds(a, q16, stride=4), :]
    w16 = jnp.concatenate(w16_parts, axis=0)

    head_of_col = lax.broadcasted_iota(jnp.int32, (2 * LANES, ODD_MIX), 1) // dh
    lane_of_row = lax.broadcasted_iota(jnp.int32, (2 * LANES, ODD_MIX), 0) % LANES
    expand = jnp.where(head_of_col == lane_of_row, 1.0, 0.0).astype(BF16)

    def two_terms(w):
        hi = w.astype(BF16)
        return jnp.concatenate([hi, (w - hi.astype(F32)).astype(BF16)], axis=1)

    w1, w4, w16 = two_terms(w1), two_terms(w4), two_terms(w16)

    pair = 2
    for hp in range(DIL_HEADS // pair):
        spread = expand[:, hp * pair * dh:(hp + 1) * pair * dh]
        w1p, w4p, w16p = _dot(w1, spread), _dot(w4, spread), _dot(w16, spread)
        for hh in range(pair):
            cols = slice((hp * pair + hh) * dh, (hp * pair + hh + 1) * dh)
            part = slice(hh * dh, (hh + 1) * dh)
            for r4 in range(4):
                for a in range(4):
                    r = r4 + 4 * a
                    o4_scr[r4, pl.ds(a, q16, stride=4), :] = (w16p[r * q16:(r + 1) * q16, part]
                                                             * o16_ref[r, :, cols])
            for r4 in range(4):
                acc4 = o4_scr[r4] + w4p[r4 * q4:(r4 + 1) * q4, part] * o4_ref[r4, :, cols]
                o_scr[pl.ds(r4, q4, stride=4), :] = acc4
            out_ref[:, cols] = (o_scr[...] + w1p[:, part] * o1_ref[0, :, cols]).astype(out_ref.dtype)


def merge_patterns(outs, lses, dilations, *, tm):
    assert tuple(dilations) == (1, 4, 16), "the re-interleave is written as two stride-4 passes"
    tiles = SEQ // tm

    def spec(d, width):
        return pl.BlockSpec((None, d, tm // d, width), lambda i: (i // tiles, 0, i % tiles, 0))

    return pl.pallas_call(
        functools.partial(_merge_kernel, tm=tm),
        out_shape=jax.ShapeDtypeStruct((TOKENS, ODD_MIX), BF16),
        grid=(TOKENS // tm,),
        in_specs=[spec(d, ODD_MIX) for d in dilations] + [spec(d, LANES) for d in dilations],
        out_specs=pl.BlockSpec((tm, ODD_MIX), lambda i: (i, 0)),
        scratch_shapes=[pltpu.VMEM((2, tm, LANES), F32),
                        pltpu.VMEM((4, tm // 4, LANES), F32),
                        pltpu.VMEM((4, tm // 4, DIL_HEAD_DIM), F32),
                        pltpu.VMEM((tm, DIL_HEAD_DIM), F32)],
        compiler_params=_params(("parallel",)),
        name="merge_patterns",
    )(*outs, *lses)


def _rope_tables():
    pos = jnp.arange(SEQ, dtype=F32)[:, None]
    half = MLA_ROPE // 2
    inv = ROPE_THETA ** (-jnp.arange(half, dtype=F32) * 2.0 / MLA_ROPE)
    ang = pos * inv[None, :]
    c, s = jnp.cos(ang), jnp.sin(ang)
    z = jnp.zeros_like(c)
    small = jnp.concatenate([c, c, z, z, -s, z, z, z, z, s, z, z], axis=-1)
    half = DIL_HEAD_DIM // 2
    inv = ROPE_THETA ** (-jnp.arange(half, dtype=F32) * 2.0 / DIL_HEAD_DIM)
    ang = pos * inv[None, :]
    c, s = jnp.cos(ang), jnp.sin(ang)
    full = jnp.concatenate([c, c, -s, s], axis=-1)
    return small, full


def _even_weights(w_in, w_uq, w_ukv, b_i, b_f):
    cuts = np.cumsum((MLA_Q_LORA, MLA_KV_LORA, MLA_ROPE, 2 * MLSTM_HEADS * MLSTM_DK,
                      MLSTM_HEADS * MLSTM_DV, MLSTM_HEADS, MLSTM_HEADS))
    c_q, c_kv, k_r, m_qk, m_v, m_i, m_f, m_o = jnp.split(w_in, cuts.tolist(), axis=1)
    pad = jnp.zeros((D_MODEL, EV_PROJ - EV_TAIL - MLA_ROPE - 2 * MLSTM_HEADS), w_in.dtype)
    w_in_r = jnp.concatenate([c_q, c_kv, m_qk, m_v, m_o, k_r, m_i, m_f, pad], axis=1).astype(BF16)
    uq = w_uq.reshape(MLA_Q_LORA, MLA_HEADS, MLA_QK)
    uq = jnp.pad(uq, ((0, 0), (0, 0), (0, 2 * LANES - MLA_QK)))
    w_uq_r = uq.reshape(MLA_Q_LORA, MLA_HEADS * 2 * LANES).astype(BF16)
    gate_bias = jnp.concatenate([jnp.zeros((TAIL_I,), F32), b_i.astype(F32), b_f.astype(F32),
                                 jnp.zeros((LANES - TAIL_F - MLSTM_HEADS,), F32)]).reshape(1, LANES)
    return w_in_r, w_uq_r, w_ukv.astype(BF16), gate_bias


def _even_layer(x, xn, w_in, q_norm, w_uq, kv_norm, w_ukv, conv_w, conv_b, b_i, b_f, small_tab, casts):
    w_in_r, w_uq_r, w_ukv_r, gate_bias = _even_weights(w_in, w_uq, w_ukv, b_i, b_f)
    proj = matmul_weight_stationary(xn, w_in_r, tm=512, tn=EV_PROJ // 3, out_dtype=F32)
    scale = MLA_QK ** -0.5 * LOG2E
    q = rms_matmul(proj, EV_CQ // MLA_Q_LORA, MLA_Q_LORA, q_norm, w_uq_r, tm=512, tn=2048,
                   out_dtype=BF16, mode="mlaq", tab=small_tab * scale, scale=scale)
    kv = rms_matmul(proj, EV_CKV // MLA_KV_LORA, MLA_KV_LORA, kv_norm, w_ukv_r, tm=512, tn=2048,
                    out_dtype=BF16)
    kr = krope(proj, small_tab, tm=1024)
    a_out, cast_weights = mla_attention(q, kv, kr, casts, tq=512, heads=4)
    hm = mlstm(proj, conv_w, conv_b, gate_bias)
    w_out_b = cast_weights[0][0]
    assert a_out.shape[1] == hm.shape[1]
    return matmul_residual(x, [(a_out, w_out_b, 0), (hm, w_out_b, 1)], tm=512), cast_weights


def _odd_layer(x, xn, w_qkv_b, w_out_b, full_tab):
    scale = DIL_HEAD_DIM ** -0.5 * LOG2E
    identity = jnp.concatenate([jnp.ones((SEQ, LANES), F32), jnp.zeros((SEQ, LANES), F32)], axis=-1)
    tab = jnp.stack([full_tab * scale, full_tab, identity])
    dilations = [d for _, d in DIL_PATTERNS]
    qkv_perms = qkv_projection(xn, w_qkv_b, tab, dilations, tm=512, tn=1024)
    outs, lses = [], []
    for (window, dilation), qkv_perm in zip(DIL_PATTERNS, qkv_perms):
        o_g, lse_g = dilated_pattern(qkv_perm, window, dilation, hw=1024, units=4)
        outs.append(o_g)
        lses.append(lse_g)
    o = merge_patterns(outs, lses, dilations, tm=512)
    return matmul_residual(x, [(o, w_out_b, 0)], tm=512)


def kernel(x, norm_mix, norm_mlp, ev_w_in, mla_q_norm, mla_w_uq, mla_kv_norm, mla_w_ukv,
           mlstm_conv_w, mlstm_conv_b, mlstm_b_i, mlstm_b_f, ev_w_out, od_w_qkv, od_w_out,
           mlp_w1, mlp_w2, norm_final):
    assert x.shape == (BATCH, SEQ, D_MODEL) and x.dtype == F32
    small_tab, full_tab = _rope_tables()
    xt = x.reshape(TOKENS, D_MODEL)
    xn = rms_norm_bf16(xt, norm_mix[0], tm=512)
    w1b = w2b = w_qkv_b = w_od_out_b = None
    for layer in range(DEPTH):
        i = layer // 2
        if layer % 2 == 0:
            casts = [(ev_w_out, i, 1)]
            if layer + 1 < DEPTH:
                casts += [(od_w_qkv, i, 1), (od_w_out, i, 1)]
            if layer == 0:
                casts += [(mlp_w1, 0, 1), (mlp_w2, 0, 1)]
            xt, cast_weights = _even_layer(xt, xn, ev_w_in[i], mla_q_norm[i], mla_w_uq[i], mla_kv_norm[i],
                                           mla_w_ukv[i], mlstm_conv_w[i], mlstm_conv_b[i], mlstm_b_i[i],
                                           mlstm_b_f[i], small_tab, casts)
            if layer + 1 < DEPTH:
                w_qkv_b, w_od_out_b = cast_weights[1][0], cast_weights[2][0]
            if layer == 0:
                w1b, w2b = cast_weights[-2], cast_weights[-1]
        else:
            xt = _odd_layer(xt, xn, w_qkv_b, w_od_out_b, full_tab)
        if layer == DEPTH - 1:
            (xt,), _ = mlp_block(xt, norm_mlp[layer], w1b, w2b, 0, norm_final, [], tm=512, tf=1024,
                                 tail="final")
        else:
            casts = [(mlp_w1, layer + 1, 1), (mlp_w2, layer + 1, 1)]
            (xt, xn), (w1b, w2b) = mlp_block(xt, norm_mlp[layer], w1b, w2b, 0, norm_mix[layer + 1], casts,
                                             tm=512, tf=1024, tail="next")
    return xt.reshape(BATCH, SEQ, D_MODEL)
```

```python
import functools

import jax
import jax.numpy as jnp
import numpy as np
from jax import lax
from jax.experimental import pallas as pl
from jax.experimental.pallas import tpu as pltpu

D_MODEL = 2048
BATCH = 4
SEQ = 2048
DEPTH = 4
MLA_HEADS = 8
MLA_Q_LORA = 512
MLA_KV_LORA = 512
MLA_NOPE = 128
MLA_ROPE = 64
MLA_V = 128
MLA_QK = MLA_NOPE + MLA_ROPE
MLSTM_HEADS = 4
MLSTM_DK = 128
MLSTM_DV = 256
MLSTM_CHUNK = 128
CONV_WIDTH = 4
DIL_HEADS = 16
DIL_HEAD_DIM = 128
DIL_PATTERNS = ((128, 1), (512, 4), (2048, 16))
DIL_BLOCK = 128
D_FF = 4 * D_MODEL
ROPE_THETA = 10000.0
NORM_EPS = 1e-6
ODD_MIX = DIL_HEADS * DIL_HEAD_DIM
TOKENS = BATCH * SEQ

LANES = 128
VMEM_LIMIT_BYTES = 56 * 1024 * 1024

EV_CQ, EV_CKV, EV_MQK, EV_MV, EV_MO, EV_TAIL = 0, 512, 1024, 2048, 3072, 4096
EV_PROJ = 4224
TAIL_I = MLA_ROPE
TAIL_F = MLA_ROPE + MLSTM_HEADS

F32 = jnp.float32
BF16 = jnp.bfloat16
NEG_INF = float("-inf")
LOG2E = float(np.log2(np.e))
LN2 = float(np.log(2.0))


def _params(semantics):
    return pltpu.CompilerParams(dimension_semantics=semantics, vmem_limit_bytes=VMEM_LIMIT_BYTES)


def _rms_normalize(x, g):
    ms = jnp.mean(x * x, axis=-1, keepdims=True)
    return x * lax.rsqrt(ms + NORM_EPS) * g


def _dot(a, b):
    return jnp.dot(a, b, preferred_element_type=F32)


def _dot_nt(a, b):
    return lax.dot_general(a, b, (((1,), (1,)), ((), ())), preferred_element_type=F32)


def _dot_tn(a, b):
    return lax.dot_general(a, b, (((0,), (0,)), ((), ())), preferred_element_type=F32)


def _sigmoid(x):
    return 1.0 / (1.0 + jnp.exp(-x))


def _log_sigmoid(x):
    return jnp.minimum(x, 0.0) - jnp.log1p(jnp.exp(-jnp.abs(x)))


def _rope_small(y, tab):
    return (y * tab[:, 0:LANES] + pltpu.roll(y, 96, 1) * tab[:, LANES:2 * LANES]
            + pltpu.roll(y, 32, 1) * tab[:, 2 * LANES:3 * LANES])


def _rope_full(y, tab):
    return y * tab[:, 0:LANES] + pltpu.roll(y, 64, 1) * tab[:, LANES:2 * LANES]


def _norm_to_scratch(x_ref, g_ref, xn_ref):
    @pl.when(pl.program_id(1) == 0)
    def _():
        xn_ref[...] = _rms_normalize(x_ref[...], g_ref[...]).astype(BF16)


def _rms_mm_plain_kernel(x_ref, g_ref, w_ref, o_ref, xn_ref):
    _norm_to_scratch(x_ref, g_ref, xn_ref)
    o_ref[...] = _dot(xn_ref[...], w_ref[...]).astype(o_ref.dtype)


def _rms_mm_mlaq_kernel(x_ref, g_ref, w_ref, tab_ref, o_ref, xn_ref, *, scale):
    _norm_to_scratch(x_ref, g_ref, xn_ref)
    y = _dot(xn_ref[...], w_ref[...])
    tab = tab_ref[...]
    for grp in range(y.shape[1] // LANES):
        yg = y[:, grp * LANES:(grp + 1) * LANES]
        og = yg * scale if grp % 2 == 0 else _rope_small(yg, tab)
        o_ref[:, grp * LANES:(grp + 1) * LANES] = og.astype(o_ref.dtype)


def _qkv_proj_kernel(xn_ref, w_ref, tab_ref, *rest, dilations, tm, sub_cols):
    out_refs, (scr_ref, scr4_ref) = rest[:len(dilations)], rest[len(dilations):]
    out_by_d = dict(zip(dilations, out_refs))
    tab = tab_ref[...]
    for sub in range(w_ref.shape[1] // sub_cols):
        y = _dot(xn_ref[...], w_ref[:, sub * sub_cols:(sub + 1) * sub_cols])
        for part in range(sub_cols // LANES):
            grp = sub * (sub_cols // LANES) + part
            cols = slice(grp * LANES, (grp + 1) * LANES)
            yg = _rope_full(y[:, part * LANES:(part + 1) * LANES], tab)
            out_by_d[1][0, :, cols] = yg.astype(BF16)
            scr_ref[grp] = yg
            for r4 in range(4):
                v = scr_ref[grp, pl.ds(r4, tm // 4, stride=4), :]
                out_by_d[4][r4, :, cols] = v.astype(BF16)
                scr4_ref[r4] = v
            for r4 in range(4):
                for a in range(4):
                    w = scr4_ref[r4, pl.ds(a, tm // 16, stride=4), :]
                    out_by_d[16][r4 + 4 * a, :, cols] = w.astype(BF16)


def qkv_projection(xn, w, tab, dilations, *, tm, tn):
    assert tuple(dilations) == (1, 4, 16), "the de-interleave is written as two stride-4 passes"
    t, k_dim = xn.shape
    n = w.shape[1]
    tiles = SEQ // tm
    q_tiles = ODD_MIX // tn
    return pl.pallas_call(
        functools.partial(_qkv_proj_kernel, dilations=tuple(dilations), tm=tm, sub_cols=2 * LANES),
        out_shape=tuple(jax.ShapeDtypeStruct((BATCH, d, SEQ // d, n), BF16) for d in dilations),
        grid=(n // tn, t // tm),
        in_specs=[
            pl.BlockSpec((tm, k_dim), lambda j, i: (i, 0)),
            pl.BlockSpec((k_dim, tn), lambda j, i: (0, j)),
            pl.BlockSpec((None, tm, 2 * LANES), lambda j, i: (j // q_tiles, i % tiles, 0)),
        ],
        out_specs=tuple(pl.BlockSpec((None, d, tm // d, tn), lambda j, i: (i // tiles, 0, i % tiles, j))
                        for d in dilations),
        scratch_shapes=[pltpu.VMEM((tn // LANES, tm, LANES), F32), pltpu.VMEM((4, tm // 4, LANES), F32)],
        compiler_params=_params(("parallel", "arbitrary")),
        name="qkv_projection",
    )(xn, w, tab)


def _norm_kernel(x_ref, g_ref, o_ref):
    o_ref[...] = _rms_normalize(x_ref[...], g_ref[...]).astype(o_ref.dtype)


def rms_norm_bf16(x, g, *, tm):
    t, d = x.shape
    return pl.pallas_call(
        _norm_kernel,
        out_shape=jax.ShapeDtypeStruct((t, d), BF16),
        grid=(t // tm,),
        in_specs=[pl.BlockSpec((tm, d), lambda i: (i, 0)), pl.BlockSpec((1, d), lambda i: (0, 0))],
        out_specs=pl.BlockSpec((tm, d), lambda i: (i, 0)),
        compiler_params=_params(("parallel",)),
        name="rms_norm_bf16",
    )(x, g.reshape(1, d))


def _mm_kernel(a_ref, w_ref, o_ref):
    o_ref[...] = _dot(a_ref[...], w_ref[...]).astype(o_ref.dtype)


def matmul_weight_stationary(a, w, *, tm, tn, out_dtype):
    t, k = a.shape
    n = w.shape[1]
    return pl.pallas_call(
        _mm_kernel,
        out_shape=jax.ShapeDtypeStruct((t, n), out_dtype),
        grid=(n // tn, t // tm),
        in_specs=[pl.BlockSpec((tm, k), lambda j, i: (i, 0)), pl.BlockSpec((k, tn), lambda j, i: (0, j))],
        out_specs=pl.BlockSpec((tm, tn), lambda j, i: (i, j)),
        compiler_params=_params(("parallel", "arbitrary")),
        name="matmul_weight_stationary",
    )(a, w)


def rms_matmul(x, x_col_block, k_dim, g, w, *, tm, tn, out_dtype, mode="plain", tab=None, scale=None):
    t = x.shape[0]
    n = w.shape[1]
    assert t % tm == 0 and n % tn == 0 and w.shape[0] == k_dim
    grid = (t // tm, n // tn)
    pos_tiles = SEQ // tm
    in_specs = [
        pl.BlockSpec((tm, k_dim), lambda i, j: (i, x_col_block)),
        pl.BlockSpec((1, k_dim), lambda i, j: (0, 0)),
        pl.BlockSpec((k_dim, tn), lambda i, j: (0, j)),
    ]
    args = [x, g.reshape(1, k_dim), w]
    if mode == "plain":
        body = _rms_mm_plain_kernel
    elif mode == "mlaq":
        body = functools.partial(_rms_mm_mlaq_kernel, scale=scale)
        in_specs.append(pl.BlockSpec((tm, 3 * LANES), lambda i, j: (i % pos_tiles, 0)))
        args.append(tab)
    else:
        raise ValueError(mode)
    return pl.pallas_call(
        body,
        out_shape=jax.ShapeDtypeStruct((t, n), out_dtype),
        grid=grid,
        in_specs=in_specs,
        out_specs=pl.BlockSpec((tm, tn), lambda i, j: (i, j)),
        scratch_shapes=[pltpu.VMEM((tm, k_dim), BF16)],
        compiler_params=_params(("parallel", "arbitrary")),
        name="rms_matmul_" + mode,
    )(*args)


def _krope_kernel(p_ref, tab_ref, o_ref):
    o_ref[...] = _rope_small(p_ref[...], tab_ref[...]).astype(o_ref.dtype)


def krope(proj, tab, *, tm):
    pos_tiles = SEQ // tm
    return pl.pallas_call(
        _krope_kernel,
        out_shape=jax.ShapeDtypeStruct((TOKENS, LANES), BF16),
        grid=(TOKENS // tm,),
        in_specs=[pl.BlockSpec((tm, LANES), lambda i: (i, EV_TAIL // LANES)),
                  pl.BlockSpec((tm, 3 * LANES), lambda i: (i % pos_tiles, 0))],
        out_specs=pl.BlockSpec((tm, LANES), lambda i: (i, 0)),
        compiler_params=_params(("parallel",)),
        name="krope",
    )(proj, tab)


def _mm_res_kernel(*refs):
    r_ref, o_ref = refs[-2], refs[-1]
    acc = r_ref[...]
    for k in range((len(refs) - 2) // 2):
        acc = acc + _dot(refs[2 * k][...], refs[2 * k + 1][...])
    o_ref[...] = acc


def matmul_residual(res, pairs, *, tm):
    t, n = res.shape
    in_specs, args = [], []
    for a, w, row_block in pairs:
        k = a.shape[1]
        in_specs += [pl.BlockSpec((tm, k), lambda i: (i, 0)),
                     pl.BlockSpec((k, n), lambda i, row_block=row_block: (row_block, 0))]
        args += [a, w]
    in_specs.append(pl.BlockSpec((tm, n), lambda i: (i, 0)))
    args.append(res)
    return pl.pallas_call(
        _mm_res_kernel,
        out_shape=jax.ShapeDtypeStruct((t, n), F32),
        grid=(t // tm,),
        in_specs=in_specs,
        out_specs=pl.BlockSpec((tm, n), lambda i: (i, 0)),
        compiler_params=_params(("parallel",)),
        name="matmul_residual",
    )(*args)


def _side_cast_specs(casts, steps, step_of):
    in_specs, out_specs, out_shapes = [], [], []
    for w, first, count in casts:
        _, rows, cols = w.shape
        slabs = steps // count
        assert steps % count == 0 and rows % slabs == 0 and (rows // slabs) % 16 == 0
        block = (None, rows // slabs, cols)
        in_specs.append(pl.BlockSpec(
            block, lambda *idx, first=first, slabs=slabs: (first + step_of(*idx) // slabs,
                                                           step_of(*idx) % slabs, 0)))
        out_specs.append(pl.BlockSpec(
            block, lambda *idx, slabs=slabs: (step_of(*idx) // slabs, step_of(*idx) % slabs, 0)))
        out_shapes.append(jax.ShapeDtypeStruct((count, rows, cols), BF16))
    return in_specs, out_specs, out_shapes


def _mlp_kernel(x_ref, g_ref, w1_ref, w2_ref, gn_ref, *rest, tail, ncast):
    cast_in, o_ref, xn_ref = rest[:ncast], rest[ncast], rest[-1]
    cast_out = rest[len(rest) - 1 - ncast:len(rest) - 1]
    f = pl.program_id(1)

    @pl.when(f == 0)
    def _():
        x = x_ref[...]
        xn_ref[...] = _rms_normalize(x, g_ref[...]).astype(BF16)
        o_ref[...] = x

    h = jnp.maximum(_dot(xn_ref[...], w1_ref[...]), 0.0)
    for src, dst in zip(cast_in, cast_out):
        dst[...] = src[...].astype(dst.dtype)
    o_ref[...] += _dot((h * h).astype(BF16), w2_ref[...])

    @pl.when(f == pl.num_programs(1) - 1)
    def _():
        normed = _rms_normalize(o_ref[...], gn_ref[...])
        if tail == "final":
            o_ref[...] = normed
        else:
            rest[ncast + 1][...] = normed.astype(BF16)


def mlp_block(x, g, w1b, w2b, layer, gn, casts, *, tm, tf, tail):
    t, d = x.shape
    dff = w1b.shape[2]
    nf = dff // tf
    cast_in_specs, cast_out_specs, cast_shapes = _side_cast_specs(casts, (t // tm) * nf, lambda i, f: i * nf + f)
    out_shape = [jax.ShapeDtypeStruct((t, d), F32)]
    out_specs = [pl.BlockSpec((tm, d), lambda i, f: (i, 0))]
    if tail == "next":
        out_shape.append(jax.ShapeDtypeStruct((t, d), BF16))
        out_specs.append(pl.BlockSpec((tm, d), lambda i, f: (i, 0)))
    n_main = len(out_shape)
    outs = pl.pallas_call(
        functools.partial(_mlp_kernel, tail=tail, ncast=len(casts)),
        out_shape=(*out_shape, *cast_shapes),
        grid=(t // tm, nf),
        in_specs=[
            pl.BlockSpec((tm, d), lambda i, f: (i, 0)),
            pl.BlockSpec((1, d), lambda i, f: (0, 0)),
            pl.BlockSpec((None, d, tf), lambda i, f: (layer, 0, f)),
            pl.BlockSpec((None, tf, d), lambda i, f: (layer, f, 0)),
            pl.BlockSpec((1, d), lambda i, f: (0, 0)),
            *cast_in_specs,
        ],
        out_specs=(*out_specs, *cast_out_specs),
        scratch_shapes=[pltpu.VMEM((tm, d), BF16)],
        compiler_params=_params(("parallel", "arbitrary")),
        name="mlp_block",
    )(x, g.reshape(1, d), w1b, w2b, gn.reshape(1, d), *[w for w, _, _ in casts])
    return outs[:n_main], outs[n_main:]


def _mla_attn_kernel(q_ref, kv_ref, kr_ref, *rest, tq, heads, ncast):
    cast_in, o_ref, cast_out = rest[:ncast], rest[ncast], rest[ncast + 1:2 * ncast + 1]
    m_ref, l_ref, acc_ref = rest[2 * ncast + 1:]
    qi = pl.program_id(2)
    hw = 2 * LANES
    m_ref[...] = jnp.full(m_ref.shape, NEG_INF, F32)
    l_ref[...] = jnp.zeros(l_ref.shape, F32)
    acc_ref[...] = jnp.zeros(acc_ref.shape, F32)

    def block(ki, masked):
        start = pl.multiple_of(ki * tq, tq)
        kr = kr_ref[pl.ds(start, tq), :]
        for h in range(heads):
            q = q_ref[:, h * hw:(h + 1) * hw]
            kn = kv_ref[pl.ds(start, tq), h * hw:h * hw + MLA_NOPE]
            v = kv_ref[pl.ds(start, tq), h * hw + MLA_NOPE:(h + 1) * hw]
            s = _dot_nt(q, jnp.concatenate([kn, kr], axis=-1))
            if masked:
                row = lax.broadcasted_iota(jnp.int32, s.shape, 0)
                col = lax.broadcasted_iota(jnp.int32, s.shape, 1)
                s = jnp.where(col <= row, s, NEG_INF)
            m_prev = m_ref[h]
            m_new = jnp.maximum(m_prev, jnp.max(s, axis=-1, keepdims=True))
            p = jnp.exp2(s - jnp.tile(m_new, (1, tq // LANES)))
            alpha = jnp.exp2(m_prev - m_new)
            l_ref[h] = alpha * l_ref[h] + jnp.sum(p, axis=-1, keepdims=True)
            acc_ref[h] = alpha * acc_ref[h] + _dot(p.astype(BF16), v)
            m_ref[h] = m_new

    def body(ki, carry):
        block(ki, False)
        return carry

    lax.fori_loop(0, qi, body, 0)
    for src, dst in zip(cast_in, cast_out):
        dst[...] = src[...].astype(dst.dtype)
    block(qi, True)
    for h in range(heads):
        o_ref[:, h * MLA_V:(h + 1) * MLA_V] = (acc_ref[h] / l_ref[h]).astype(o_ref.dtype)


def mla_attention(q, kv, kr, casts, *, tq, heads):
    nq = SEQ // tq
    hw = 2 * LANES * heads
    groups = MLA_HEADS // heads
    steps = BATCH * groups * nq

    cast_in_specs, cast_out_specs, cast_shapes = _side_cast_specs(
        casts, steps, lambda b, h, i: (b * groups + h) * nq + i)

    outs = pl.pallas_call(
        functools.partial(_mla_attn_kernel, tq=tq, heads=heads, ncast=len(casts)),
        out_shape=(jax.ShapeDtypeStruct((TOKENS, MLA_HEADS * MLA_V), BF16), *cast_shapes),
        grid=(BATCH, groups, nq),
        in_specs=[
            pl.BlockSpec((tq, hw), lambda b, h, i: (b * nq + i, h)),
            pl.BlockSpec((SEQ, hw), lambda b, h, i: (b, h)),
            pl.BlockSpec((SEQ, LANES), lambda b, h, i: (b, 0)),
            *cast_in_specs,
        ],
        out_specs=(pl.BlockSpec((tq, heads * MLA_V), lambda b, h, i: (b * nq + i, h)), *cast_out_specs),
        scratch_shapes=[pltpu.VMEM((heads, tq, LANES), F32), pltpu.VMEM((heads, tq, LANES), F32),
                        pltpu.VMEM((heads, tq, MLA_V), F32)],
        compiler_params=_params(("parallel", "parallel", "arbitrary")),
        name="mla_attention",
    )(q, kv, kr, *[w for w, _, _ in casts])
    return outs[0], outs[1:]


def _mlstm_kernel(qk_ref, v_ref, og_ref, gate_ref, cw_ref, cb_ref, gb_ref, out_ref,
                  hist_ref, c_ref, n_ref, m_ref):
    chunk = MLSTM_CHUNK
    c = pl.program_id(1)

    @pl.when(c == 0)
    def _():
        hist_ref[0:8, :] = jnp.zeros((8, hist_ref.shape[1]), F32)
        c_ref[...] = jnp.zeros(c_ref.shape, F32)
        n_ref[...] = jnp.zeros(n_ref.shape, F32)
        m_ref[...] = jnp.zeros(m_ref.shape, F32)

    hist_ref[8:8 + chunk, :] = qk_ref[...]
    y = jnp.broadcast_to(cb_ref[...], (chunk, hist_ref.shape[1]))
    for j in range(CONV_WIDTH):
        off = 8 - (CONV_WIDTH - 1) + j
        y = y + cw_ref[j:j + 1, :] * hist_ref[off:off + chunk, :]
    qk = y * _sigmoid(y)
    hist_ref[0:8, :] = hist_ref[chunk:chunk + 8, :]

    gates = gate_ref[...] + gb_ref[...]
    logf = _log_sigmoid(gates)
    row = lax.broadcasted_iota(jnp.int32, (chunk, chunk), 0)
    col = lax.broadcasted_iota(jnp.int32, (chunk, chunk), 1)
    tril = col <= row
    bcum = jnp.dot(tril.astype(F32), logf, preferred_element_type=F32,
                   precision=lax.Precision.HIGHEST)
    gates_t = gates.T
    bcum_t = bcum.T

    dk, dv = MLSTM_DK, MLSTM_DV
    for h in range(MLSTM_HEADS):
        q = qk[:, h * dk:(h + 1) * dk]
        k = qk[:, (MLSTM_HEADS + h) * dk:(MLSTM_HEADS + h + 1) * dk] * (dk ** -0.5)
        v = v_ref[:, h * dv:(h + 1) * dv].astype(BF16)
        b_c = bcum[:, TAIL_F + h:TAIL_F + h + 1]
        b_r = bcum_t[TAIL_F + h:TAIL_F + h + 1, :]
        i_c = gates[:, TAIL_I + h:TAIL_I + h + 1]
        i_r = gates_t[TAIL_I + h:TAIL_I + h + 1, :]
        b_last = bcum[chunk - 1:chunk, TAIL_F + h:TAIL_F + h + 1]
        m_prev = m_ref[h:h + 1, 0:1]
        n_prev = n_ref[h:h + 1, :]
        c_prev = c_ref[h]

        dmat = jnp.where(tril, b_c - b_r + i_r, NEG_INF)
        m_inter = b_c + m_prev
        m_row = jnp.maximum(m_inter, jnp.max(dmat, axis=-1, keepdims=True))
        q_b = q.astype(BF16)
        wmat = jnp.exp(dmat - m_row) * _dot_nt(q_b, k.astype(BF16))
        inter = jnp.exp(m_inter - m_row)
        num = _dot(wmat.astype(BF16), v) + inter * _dot(q_b, c_prev.astype(BF16))
        den = jnp.sum(wmat, axis=-1, keepdims=True) + inter * jnp.sum(q * n_prev, axis=-1, keepdims=True)
        hcell = num / jnp.maximum(jnp.abs(den), jnp.exp(-m_row))
        gate_o = _sigmoid(og_ref[:, h * dv:(h + 1) * dv])
        out_ref[:, h * dv:(h + 1) * dv] = (gate_o * hcell).astype(out_ref.dtype)

        g_c = b_last - b_c + i_c
        m_new = jnp.maximum(b_last + m_prev, jnp.max(g_c, axis=0, keepdims=True))
        wk = jnp.exp(g_c - m_new)
        decay = jnp.exp(b_last + m_prev - m_new)
        kw = k * wk
        c_ref[h] = decay * c_prev + _dot_tn(kw.astype(BF16), v)
        n_ref[h:h + 1, :] = decay * n_prev + jnp.sum(kw, axis=0, keepdims=True)
        m_ref[h:h + 1, :] = jnp.broadcast_to(m_new, (1, LANES))


def mlstm(proj, conv_w, conv_b, gate_bias):
    nc = SEQ // MLSTM_CHUNK
    chunk = MLSTM_CHUNK
    wqk = 2 * MLSTM_HEADS * MLSTM_DK
    wv = MLSTM_HEADS * MLSTM_DV
    row = lambda b, c: b * nc + c
    return pl.pallas_call(
        _mlstm_kernel,
        out_shape=jax.ShapeDtypeStruct((TOKENS, wv), BF16),
        grid=(BATCH, nc),
        in_specs=[
            pl.BlockSpec((chunk, wqk), lambda b, c: (row(b, c), EV_MQK // wqk)),
            pl.BlockSpec((chunk, wv), lambda b, c: (row(b, c), EV_MV // wv)),
            pl.BlockSpec((chunk, wv), lambda b, c: (row(b, c), EV_MO // wv)),
            pl.BlockSpec((chunk, LANES), lambda b, c: (row(b, c), EV_TAIL // LANES)),
            pl.BlockSpec((CONV_WIDTH, wqk), lambda b, c: (0, 0)),
            pl.BlockSpec((1, wqk), lambda b, c: (0, 0)),
            pl.BlockSpec((1, LANES), lambda b, c: (0, 0)),
        ],
        out_specs=pl.BlockSpec((chunk, wv), lambda b, c: (row(b, c), 0)),
        scratch_shapes=[
            pltpu.VMEM((chunk + 8, wqk), F32),
            pltpu.VMEM((MLSTM_HEADS, MLSTM_DK, MLSTM_DV), F32),
            pltpu.VMEM((8, MLSTM_DK), F32),
            pltpu.VMEM((8, LANES), F32),
        ],
        compiler_params=_params(("parallel", "arbitrary")),
        name="mlstm",
    )(proj, proj, proj, proj, conv_w, conv_b.reshape(1, wqk), gate_bias)


def _dilated_kernel(*refs, nres, nblk, has_prev, span, heads):
    if has_prev:
        q_ref, kp_ref, kc_ref, vp_ref, vc_ref, o_ref, lse_ref = refs
    else:
        q_ref, kc_ref, vc_ref, o_ref, lse_ref = refs
        kp_ref = vp_ref = None
    n = pl.program_id(2)
    hg = pl.program_id(3)
    blk_sz = DIL_BLOCK
    dh = DIL_HEAD_DIM
    nkeys = 2 * blk_sz if has_prev else blk_sz
    qi = lax.broadcasted_iota(jnp.int32, (blk_sz, nkeys), 0)
    kj = lax.broadcasted_iota(jnp.int32, (blk_sz, nkeys), 1)
    dist = (nkeys - blk_sz) + qi - kj
    bias_full = jnp.where(dist >= 0, jnp.where(dist <= span, 0.0, NEG_INF), NEG_INF)
    bias_first = jnp.where(kj >= nkeys - blk_sz, bias_full, NEG_INF)
    lane = lax.broadcasted_iota(jnp.int32, (blk_sz, LANES), 1)

    @pl.when(hg == 0)
    def _():
        lse_ref[...] = jnp.zeros(lse_ref.shape, F32)

    def stack(ref, res, rows):
        return jnp.stack([ref[res, rows, h * dh:(h + 1) * dh] for h in range(heads)])

    for res in range(nres):
        for blk in range(nblk):
            rows = slice(blk * blk_sz, (blk + 1) * blk_sz)
            q3 = stack(q_ref, res, rows)
            k3 = stack(kc_ref, res, rows)
            v3 = stack(vc_ref, res, rows)
            bias = bias_full
            if has_prev:
                if blk == 0:
                    first = slice(0, blk_sz)
                    kp3, vp3 = stack(kp_ref, res, first), stack(vp_ref, res, first)
                    bias = jnp.where(n > 0, bias_full, bias_first)
                else:
                    prows = slice((blk - 1) * blk_sz, blk * blk_sz)
                    kp3, vp3 = stack(kc_ref, res, prows), stack(vc_ref, res, prows)
                k3 = jnp.concatenate([kp3, k3], axis=1)
                v3 = jnp.concatenate([vp3, v3], axis=1)
            s = jnp.einsum("hqd,hkd->hqk", q3, k3, preferred_element_type=F32) + bias[None]
            m = jnp.max(s, axis=-1, keepdims=True)
            p = jnp.exp2(s - m)
            den = jnp.sum(p, axis=-1, keepdims=True)
            o = jnp.einsum("hqk,hkd->hqd", p.astype(BF16), v3, preferred_element_type=F32) / den
            lse = m * LN2 + jnp.log(den)
            lse_tile = lse_ref[res, rows, :]
            for h in range(heads):
                o_ref[res, rows, h * dh:(h + 1) * dh] = o[h].astype(o_ref.dtype)
                lse_tile = jnp.where(lane == hg * heads + h, lse[h], lse_tile)
            lse_ref[res, rows, :] = lse_tile


def dilated_pattern(qkv_perm, window, dilation, *, hw, units):
    span = window // dilation
    length = SEQ // dilation
    nblk = min(length // DIL_BLOCK, units)
    nres = min(dilation, units // nblk)
    tl = nblk * DIL_BLOCK
    ntile = length // tl
    has_prev = length > DIL_BLOCK
    heads = hw // DIL_HEAD_DIM
    ngroups = ODD_MIX // hw

    def cur(which):
        return pl.BlockSpec((None, nres, tl, hw), lambda b, r, n, g: (b, r, n, which * ngroups + g))

    def prev(which):
        return pl.BlockSpec((None, nres, DIL_BLOCK, hw),
                            lambda b, r, n, g: (b, r, jnp.maximum(n * nblk - 1, 0), which * ngroups + g))

    if has_prev:
        in_specs = [cur(0), prev(1), cur(1), prev(2), cur(2)]
    else:
        in_specs = [cur(0), cur(1), cur(2)]
    return pl.pallas_call(
        functools.partial(_dilated_kernel, nres=nres, nblk=nblk, has_prev=has_prev, span=span, heads=heads),
        out_shape=(jax.ShapeDtypeStruct((BATCH, dilation, length, ODD_MIX), BF16),
                   jax.ShapeDtypeStruct((BATCH, dilation, length, LANES), F32)),
        grid=(BATCH, dilation // nres, ntile, ngroups),
        in_specs=in_specs,
        out_specs=(pl.BlockSpec((None, nres, tl, hw), lambda b, r, n, g: (b, r, n, g)),
                   pl.BlockSpec((None, nres, tl, LANES), lambda b, r, n, g: (b, r, n, 0))),
        compiler_params=_params(("parallel", "parallel", "arbitrary", "arbitrary")),
        name="dilated_d%d" % dilation,
    )(*([qkv_perm] * len(in_specs)))


def _merge_kernel(o1_ref, o4_ref, o16_ref, l1_ref, l4_ref, l16_ref, out_ref,
                  nat_scr, p4_scr, o4_scr, o_scr, *, tm):
    dh = DIL_HEAD_DIM
    q4, q16 = tm // 4, tm // 16

    for r4 in range(4):
        for a in range(4):
            p4_scr[r4, pl.ds(a, q16, stride=4), :] = l16_ref[r4 + 4 * a]
    for r4 in range(4):
        nat_scr[0, pl.ds(r4, q4, stride=4), :] = l4_ref[r4]
        nat_scr[1, pl.ds(r4, q4, stride=4), :] = p4_scr[r4]
    l1, l4, l16 = l1_ref[0], nat_scr[0], nat_scr[1]
    mx = jnp.maximum(jnp.maximum(l1, l4), l16)
    e1, e4, e16 = jnp.exp(l1 - mx), jnp.exp(l4 - mx), jnp.exp(l16 - mx)
    tot = e1 + e4 + e16
    w1 = e1 / tot
    nat_scr[0] = e4 / tot
    nat_scr[1] = e16 / tot
    w4 = jnp.concatenate([nat_scr[0, pl.ds(r4, q4, stride=4), :] for r4 in range(4)], axis=0)
    for r4 in range(4):
        p4_scr[r4] = nat_scr[1, pl.ds(r4, q4, stride=4), :]
    w16_parts = [None] * 16
    for r4 in range(4):
        for a in range(4):
            w16_parts[r4 + 4 * a] = p4_scr[r4, pl.ds(a, q16, stride=4), :]
    w16 = jnp.concatenate(w16_parts, axis=0)

    head_of_col = lax.broadcasted_iota(jnp.int32, (2 * LANES, ODD_MIX), 1) // dh
    lane_of_row = lax.broadcasted_iota(jnp.int32, (2 * LANES, ODD_MIX), 0) % LANES
    expand = jnp.where(head_of_col == lane_of_row, 1.0, 0.0).astype(BF16)

    def two_terms(w):
        hi = w.astype(BF16)
        return jnp.concatenate([hi, (w - hi.astype(F32)).astype(BF16)], axis=1)

    w1, w4, w16 = two_terms(w1), two_terms(w4), two_terms(w16)

    pair = 2
    for hp in range(DIL_HEADS // pair):
        spread = expand[:, hp * pair * dh:(hp + 1) * pair * dh]
        w1p, w4p, w16p = _dot(w1, spread), _dot(w4, spread), _dot(w16, spread)
        for hh in range(pair):
            cols = slice((hp * pair + hh) * dh, (hp * pair + hh + 1) * dh)
            part = slice(hh * dh, (hh + 1) * dh)
            for r4 in range(4):
                for a in range(4):
                    r = r4 + 4 * a
                    o4_scr[r4, pl.ds(a, q16, stride=4), :] = (w16p[r * q16:(r + 1) * q16, part]
                                                             * o16_ref[r, :, cols])
            for r4 in range(4):
                acc4 = o4_scr[r4] + w4p[r4 * q4:(r4 + 1) * q4, part] * o4_ref[r4, :, cols]
                o_scr[pl.ds(r4, q4, stride=4), :] = acc4
            out_ref[:, cols] = (o_scr[...] + w1p[:, part] * o1_ref[0, :, cols]).astype(out_ref.dtype)


def merge_patterns(outs, lses, dilations, *, tm):
    assert tuple(dilations) == (1, 4, 16), "the re-interleave is written as two stride-4 passes"
    tiles = SEQ // tm

    def spec(d, width):
        return pl.BlockSpec((None, d, tm // d, width), lambda i: (i // tiles, 0, i % tiles, 0))

    return pl.pallas_call(
        functools.partial(_merge_kernel, tm=tm),
        out_shape=jax.ShapeDtypeStruct((TOKENS, ODD_MIX), BF16),
        grid=(TOKENS // tm,),
        in_specs=[spec(d, ODD_MIX) for d in dilations] + [spec(d, LANES) for d in dilations],
        out_specs=pl.BlockSpec((tm, ODD_MIX), lambda i: (i, 0)),
        scratch_shapes=[pltpu.VMEM((2, tm, LANES), F32),
                        pltpu.VMEM((4, tm // 4, LANES), F32),
                        pltpu.VMEM((4, tm // 4, DIL_HEAD_DIM), F32),
                        pltpu.VMEM((tm, DIL_HEAD_DIM), F32)],
        compiler_params=_params(("parallel",)),
        name="merge_patterns",
    )(*outs, *lses)


def _rope_tables():
    pos = jnp.arange(SEQ, dtype=F32)[:, None]
    half = MLA_ROPE // 2
    inv = ROPE_THETA ** (-jnp.arange(half, dtype=F32) * 2.0 / MLA_ROPE)
    ang = pos * inv[None, :]
    c, s = jnp.cos(ang), jnp.sin(ang)
    z = jnp.zeros_like(c)
    small = jnp.concatenate([c, c, z, z, -s, z, z, z, z, s, z, z], axis=-1)
    half = DIL_HEAD_DIM // 2
    inv = ROPE_THETA ** (-jnp.arange(half, dtype=F32) * 2.0 / DIL_HEAD_DIM)
    ang = pos * inv[None, :]
    c, s = jnp.cos(ang), jnp.sin(ang)
    full = jnp.concatenate([c, c, -s, s], axis=-1)
    return small, full


def _even_weights(w_in, w_uq, w_ukv, b_i, b_f):
    cuts = np.cumsum((MLA_Q_LORA, MLA_KV_LORA, MLA_ROPE, 2 * MLSTM_HEADS * MLSTM_DK,
                      MLSTM_HEADS * MLSTM_DV, MLSTM_HEADS, MLSTM_HEADS))
    c_q, c_kv, k_r, m_qk, m_v, m_i, m_f, m_o = jnp.split(w_in, cuts.tolist(), axis=1)
    pad = jnp.zeros((D_MODEL, EV_PROJ - EV_TAIL - MLA_ROPE - 2 * MLSTM_HEADS), w_in.dtype)
    w_in_r = jnp.concatenate([c_q, c_kv, m_qk, m_v, m_o, k_r, m_i, m_f, pad], axis=1).astype(BF16)
    uq = w_uq.reshape(MLA_Q_LORA, MLA_HEADS, MLA_QK)
    uq = jnp.pad(uq, ((0, 0), (0, 0), (0, 2 * LANES - MLA_QK)))
    w_uq_r = uq.reshape(MLA_Q_LORA, MLA_HEADS * 2 * LANES).astype(BF16)
    gate_bias = jnp.concatenate([jnp.zeros((TAIL_I,), F32), b_i.astype(F32), b_f.astype(F32),
                                 jnp.zeros((LANES - TAIL_F - MLSTM_HEADS,), F32)]).reshape(1, LANES)
    return w_in_r, w_uq_r, w_ukv.astype(BF16), gate_bias


def _even_layer(x, xn, w_in, q_norm, w_uq, kv_norm, w_ukv, conv_w, conv_b, b_i, b_f, small_tab, casts):
    w_in_r, w_uq_r, w_ukv_r, gate_bias = _even_weights(w_in, w_uq, w_ukv, b_i, b_f)
    proj = matmul_weight_stationary(xn, w_in_r, tm=512, tn=EV_PROJ // 3, out_dtype=F32)
    scale = MLA_QK ** -0.5 * LOG2E
    q = rms_matmul(proj, EV_CQ // MLA_Q_LORA, MLA_Q_LORA, q_norm, w_uq_r, tm=512, tn=2048,
                   out_dtype=BF16, mode="mlaq", tab=small_tab * scale, scale=scale)
    kv = rms_matmul(proj, EV_CKV // MLA_KV_LORA, MLA_KV_LORA, kv_norm, w_ukv_r, tm=512, tn=2048,
                    out_dtype=BF16)
    kr = krope(proj, small_tab, tm=1024)
    a_out, cast_weights = mla_attention(q, kv, kr, casts, tq=512, heads=4)
    hm = mlstm(proj, conv_w, conv_b, gate_bias)
    w_out_b = cast_weights[0][0]
    assert a_out.shape[1] == hm.shape[1]
    return matmul_residual(x, [(a_out, w_out_b, 0), (hm, w_out_b, 1)], tm=512), cast_weights


def _odd_layer(x, xn, w_qkv_b, w_out_b, full_tab):
    scale = DIL_HEAD_DIM ** -0.5 * LOG2E
    identity = jnp.concatenate([jnp.ones((SEQ, LANES), F32), jnp.zeros((SEQ, LANES), F32)], axis=-1)
    tab = jnp.stack([full_tab * scale, full_tab, identity])
    dilations = [d for _, d in DIL_PATTERNS]
    qkv_perms = qkv_projection(xn, w_qkv_b, tab, dilations, tm=512, tn=1024)
    outs, lses = [], []
    for (window, dilation), qkv_perm in zip(DIL_PATTERNS, qkv_perms):
        o_g, lse_g = dilated_pattern(qkv_perm, window, dilation, hw=1024, units=8)
        outs.append(o_g)
        lses.append(lse_g)
    o = merge_patterns(outs, lses, dilations, tm=512)
    return matmul_residual(x, [(o, w_out_b, 0)], tm=512)


def kernel(x, norm_mix, norm_mlp, ev_w_in, mla_q_norm, mla_w_uq, mla_kv_norm, mla_w_ukv,
           mlstm_conv_w, mlstm_conv_b, mlstm_b_i, mlstm_b_f, ev_w_out, od_w_qkv, od_w_out,
           mlp_w1, mlp_w2, norm_final):
    assert x.shape == (BATCH, SEQ, D_MODEL) and x.dtype == F32
    small_tab, full_tab = _rope_tables()
    xt = x.reshape(TOKENS, D_MODEL)
    xn = rms_norm_bf16(xt, norm_mix[0], tm=512)
    w1b = w2b = w_qkv_b = w_od_out_b = None
    for layer in range(DEPTH):
        i = layer // 2
        if layer % 2 == 0:
            casts = [(ev_w_out, i, 1)]
            if layer + 1 < DEPTH:
                casts += [(od_w_qkv, i, 1), (od_w_out, i, 1)]
            if layer == 0:
                casts += [(mlp_w1, 0, 1), (mlp_w2, 0, 1)]
            xt, cast_weights = _even_layer(xt, xn, ev_w_in[i], mla_q_norm[i], mla_w_uq[i], mla_kv_norm[i],
                                           mla_w_ukv[i], mlstm_conv_w[i], mlstm_conv_b[i], mlstm_b_i[i],
                                           mlstm_b_f[i], small_tab, casts)
            if layer + 1 < DEPTH:
                w_qkv_b, w_od_out_b = cast_weights[1][0], cast_weights[2][0]
            if layer == 0:
                w1b, w2b = cast_weights[-2], cast_weights[-1]
        else:
            xt = _odd_layer(xt, xn, w_qkv_b, w_od_out_b, full_tab)
        if layer == DEPTH - 1:
            (xt,), _ = mlp_block(xt, norm_mlp[layer], w1b, w2b, 0, norm_final, [], tm=512, tf=1024,
                                 tail="final")
        else:
            casts = [(mlp_w1, layer + 1, 1), (mlp_w2, layer + 1, 1)]
            (xt, xn), (w1b, w2b) = mlp_block(xt, norm_mlp[layer], w1b, w2b, 0, norm_mix[layer + 1], casts,
                                             tm=512, tf=1024, tail="next")
    return xt.reshape(BATCH, SEQ, D_MODEL)
```

```python
import functools

import jax
import jax.numpy as jnp
import numpy as np
from jax import lax
from jax.experimental import pallas as pl
from jax.experimental.pallas import tpu as pltpu

D_MODEL = 2048
BATCH = 4
SEQ = 2048
DEPTH = 4
MLA_HEADS = 8
MLA_Q_LORA = 512
MLA_KV_LORA = 512
MLA_NOPE = 128
MLA_ROPE = 64
MLA_V = 128
MLA_QK = MLA_NOPE + MLA_ROPE
MLSTM_HEADS = 4
MLSTM_DK = 128
MLSTM_DV = 256
MLSTM_CHUNK = 128
CONV_WIDTH = 4
DIL_HEADS = 16
DIL_HEAD_DIM = 128
DIL_PATTERNS = ((128, 1), (512, 4), (2048, 16))
DIL_BLOCK = 128
D_FF = 4 * D_MODEL
ROPE_THETA = 10000.0
NORM_EPS = 1e-6
ODD_MIX = DIL_HEADS * DIL_HEAD_DIM
TOKENS = BATCH * SEQ

LANES = 128
VMEM_LIMIT_BYTES = 56 * 1024 * 1024

EV_CQ, EV_CKV, EV_MQK, EV_MV, EV_MO, EV_TAIL = 0, 512, 1024, 2048, 3072, 4096
EV_PROJ = 4224
TAIL_I = MLA_ROPE
TAIL_F = MLA_ROPE + MLSTM_HEADS

F32 = jnp.float32
BF16 = jnp.bfloat16
NEG_INF = float("-inf")
LOG2E = float(np.log2(np.e))
LN2 = float(np.log(2.0))


def _params(semantics):
    return pltpu.CompilerParams(dimension_semantics=semantics, vmem_limit_bytes=VMEM_LIMIT_BYTES)


def _rms_normalize(x, g):
    ms = jnp.mean(x * x, axis=-1, keepdims=True)
    return x * lax.rsqrt(ms + NORM_EPS) * g


def _dot(a, b):
    return jnp.dot(a, b, preferred_element_type=F32)


def _dot_nt(a, b):
    return lax.dot_general(a, b, (((1,), (1,)), ((), ())), preferred_element_type=F32)


def _dot_tn(a, b):
    return lax.dot_general(a, b, (((0,), (0,)), ((), ())), preferred_element_type=F32)


def _sigmoid(x):
    return 1.0 / (1.0 + jnp.exp(-x))


def _log_sigmoid(x):
    return jnp.minimum(x, 0.0) - jnp.log1p(jnp.exp(-jnp.abs(x)))


def _rope_small(y, tab):
    return (y * tab[:, 0:LANES] + pltpu.roll(y, 96, 1) * tab[:, LANES:2 * LANES]
            + pltpu.roll(y, 32, 1) * tab[:, 2 * LANES:3 * LANES])


def _rope_full(y, tab):
    return y * tab[:, 0:LANES] + pltpu.roll(y, 64, 1) * tab[:, LANES:2 * LANES]


def _norm_to_scratch(x_ref, g_ref, xn_ref):
    @pl.when(pl.program_id(1) == 0)
    def _():
        xn_ref[...] = _rms_normalize(x_ref[...], g_ref[...]).astype(BF16)


def _rms_mm_plain_kernel(x_ref, g_ref, w_ref, o_ref, xn_ref):
    _norm_to_scratch(x_ref, g_ref, xn_ref)
    o_ref[...] = _dot(xn_ref[...], w_ref[...]).astype(o_ref.dtype)


def _rms_mm_mlaq_kernel(x_ref, g_ref, w_ref, tab_ref, o_ref, xn_ref, *, scale):
    _norm_to_scratch(x_ref, g_ref, xn_ref)
    y = _dot(xn_ref[...], w_ref[...])
    tab = tab_ref[...]
    for grp in range(y.shape[1] // LANES):
        yg = y[:, grp * LANES:(grp + 1) * LANES]
        og = yg * scale if grp % 2 == 0 else _rope_small(yg, tab)
        o_ref[:, grp * LANES:(grp + 1) * LANES] = og.astype(o_ref.dtype)


def _qkv_proj_kernel(xn_ref, w_ref, tab_ref, *rest, dilations, tm, sub_cols):
    out_refs, (scr_ref, scr4_ref) = rest[:len(dilations)], rest[len(dilations):]
    out_by_d = dict(zip(dilations, out_refs))
    tab = tab_ref[...]
    for sub in range(w_ref.shape[1] // sub_cols):
        y = _dot(xn_ref[...], w_ref[:, sub * sub_cols:(sub + 1) * sub_cols])
        for part in range(sub_cols // LANES):
            grp = sub * (sub_cols // LANES) + part
            cols = slice(grp * LANES, (grp + 1) * LANES)
            yg = _rope_full(y[:, part * LANES:(part + 1) * LANES], tab)
            out_by_d[1][0, :, cols] = yg.astype(BF16)
            scr_ref[grp] = yg
            for r4 in range(4):
                v = scr_ref[grp, pl.ds(r4, tm // 4, stride=4), :]
                out_by_d[4][r4, :, cols] = v.astype(BF16)
                scr4_ref[r4] = v
            for r4 in range(4):
                for a in range(4):
                    w = scr4_ref[r4, pl.ds(a, tm // 16, stride=4), :]
                    out_by_d[16][r4 + 4 * a, :, cols] = w.astype(BF16)


def qkv_projection(xn, w, tab, dilations, *, tm, tn):
    assert tuple(dilations) == (1, 4, 16), "the de-interleave is written as two stride-4 passes"
    t, k_dim = xn.shape
    n = w.shape[1]
    tiles = SEQ // tm
    q_tiles = ODD_MIX // tn
    return pl.pallas_call(
        functools.partial(_qkv_proj_kernel, dilations=tuple(dilations), tm=tm, sub_cols=2 * LANES),
        out_shape=tuple(jax.ShapeDtypeStruct((BATCH, d, SEQ // d, n), BF16) for d in dilations),
        grid=(n // tn, t // tm),
        in_specs=[
            pl.BlockSpec((tm, k_dim), lambda j, i: (i, 0)),
            pl.BlockSpec((k_dim, tn), lambda j, i: (0, j)),
            pl.BlockSpec((None, tm, 2 * LANES), lambda j, i: (j // q_tiles, i % tiles, 0)),
        ],
        out_specs=tuple(pl.BlockSpec((None, d, tm // d, tn), lambda j, i: (i // tiles, 0, i % tiles, j))
                        for d in dilations),
        scratch_shapes=[pltpu.VMEM((tn // LANES, tm, LANES), F32), pltpu.VMEM((4, tm // 4, LANES), F32)],
        compiler_params=_params(("parallel", "arbitrary")),
        name="qkv_projection",
    )(xn, w, tab)


def _norm_kernel(x_ref, g_ref, o_ref):
    o_ref[...] = _rms_normalize(x_ref[...], g_ref[...]).astype(o_ref.dtype)


def rms_norm_bf16(x, g, *, tm):
    t, d = x.shape
    return pl.pallas_call(
        _norm_kernel,
        out_shape=jax.ShapeDtypeStruct((t, d), BF16),
        grid=(t // tm,),
        in_specs=[pl.BlockSpec((tm, d), lambda i: (i, 0)), pl.BlockSpec((1, d), lambda i: (0, 0))],
        out_specs=pl.BlockSpec((tm, d), lambda i: (i, 0)),
        compiler_params=_params(("parallel",)),
        name="rms_norm_bf16",
    )(x, g.reshape(1, d))


def _mm_kernel(a_ref, w_ref, o_ref):
    o_ref[...] = _dot(a_ref[...], w_ref[...]).astype(o_ref.dtype)


def matmul_weight_stationary(a, w, *, tm, tn, out_dtype):
    t, k = a.shape
    n = w.shape[1]
    return pl.pallas_call(
        _mm_kernel,
        out_shape=jax.ShapeDtypeStruct((t, n), out_dtype),
        grid=(n // tn, t // tm),
        in_specs=[pl.BlockSpec((tm, k), lambda j, i: (i, 0)), pl.BlockSpec((k, tn), lambda j, i: (0, j))],
        out_specs=pl.BlockSpec((tm, tn), lambda j, i: (i, j)),
        compiler_params=_params(("parallel", "arbitrary")),
        name="matmul_weight_stationary",
    )(a, w)


def rms_matmul(x, x_col_block, k_dim, g, w, *, tm, tn, out_dtype, mode="plain", tab=None, scale=None):
    t = x.shape[0]
    n = w.shape[1]
    assert t % tm == 0 and n % tn == 0 and w.shape[0] == k_dim
    grid = (t // tm, n // tn)
    pos_tiles = SEQ // tm
    in_specs = [
        pl.BlockSpec((tm, k_dim), lambda i, j: (i, x_col_block)),
        pl.BlockSpec((1, k_dim), lambda i, j: (0, 0)),
        pl.BlockSpec((k_dim, tn), lambda i, j: (0, j)),
    ]
    args = [x, g.reshape(1, k_dim), w]
    if mode == "plain":
        body = _rms_mm_plain_kernel
    elif mode == "mlaq":
        body = functools.partial(_rms_mm_mlaq_kernel, scale=scale)
        in_specs.append(pl.BlockSpec((tm, 3 * LANES), lambda i, j: (i % pos_tiles, 0)))
        args.append(tab)
    else:
        raise ValueError(mode)
    return pl.pallas_call(
        body,
        out_shape=jax.ShapeDtypeStruct((t, n), out_dtype),
        grid=grid,
        in_specs=in_specs,
        out_specs=pl.BlockSpec((tm, tn), lambda i, j: (i, j)),
        scratch_shapes=[pltpu.VMEM((tm, k_dim), BF16)],
        compiler_params=_params(("parallel", "arbitrary")),
        name="rms_matmul_" + mode,
    )(*args)


def _krope_kernel(p_ref, tab_ref, o_ref):
    o_ref[...] = _rope_small(p_ref[...], tab_ref[...]).astype(o_ref.dtype)


def krope(proj, tab, *, tm):
    pos_tiles = SEQ // tm
    return pl.pallas_call(
        _krope_kernel,
        out_shape=jax.ShapeDtypeStruct((TOKENS, LANES), BF16),
        grid=(TOKENS // tm,),
        in_specs=[pl.BlockSpec((tm, LANES), lambda i: (i, EV_TAIL // LANES)),
                  pl.BlockSpec((tm, 3 * LANES), lambda i: (i % pos_tiles, 0))],
        out_specs=pl.BlockSpec((tm, LANES), lambda i: (i, 0)),
        compiler_params=_params(("parallel",)),
        name="krope",
    )(proj, tab)


def _mm_res_kernel(*refs):
    r_ref, o_ref = refs[-2], refs[-1]
    acc = r_ref[...]
    for k in range((len(refs) - 2) // 2):
        acc = acc + _dot(refs[2 * k][...], refs[2 * k + 1][...])
    o_ref[...] = acc


def matmul_residual(res, pairs, *, tm):
    t, n = res.shape
    in_specs, args = [], []
    for a, w, row_block in pairs:
        k = a.shape[1]
        in_specs += [pl.BlockSpec((tm, k), lambda i: (i, 0)),
                     pl.BlockSpec((k, n), lambda i, row_block=row_block: (row_block, 0))]
        args += [a, w]
    in_specs.append(pl.BlockSpec((tm, n), lambda i: (i, 0)))
    args.append(res)
    return pl.pallas_call(
        _mm_res_kernel,
        out_shape=jax.ShapeDtypeStruct((t, n), F32),
        grid=(t // tm,),
        in_specs=in_specs,
        out_specs=pl.BlockSpec((tm, n), lambda i: (i, 0)),
        compiler_params=_params(("parallel",)),
        name="matmul_residual",
    )(*args)


def _side_cast_specs(casts, steps, step_of):
    in_specs, out_specs, out_shapes = [], [], []
    for w, first, count in casts:
        _, rows, cols = w.shape
        slabs = steps // count
        assert steps % count == 0 and rows % slabs == 0 and (rows // slabs) % 16 == 0
        block = (None, rows // slabs, cols)
        in_specs.append(pl.BlockSpec(
            block, lambda *idx, first=first, slabs=slabs: (first + step_of(*idx) // slabs,
                                                           step_of(*idx) % slabs, 0)))
        out_specs.append(pl.BlockSpec(
            block, lambda *idx, slabs=slabs: (step_of(*idx) // slabs, step_of(*idx) % slabs, 0)))
        out_shapes.append(jax.ShapeDtypeStruct((count, rows, cols), BF16))
    return in_specs, out_specs, out_shapes


def _mlp_kernel(x_ref, g_ref, w1_ref, w2_ref, gn_ref, *rest, tail, ncast):
    cast_in, o_ref, xn_ref = rest[:ncast], rest[ncast], rest[-1]
    cast_out = rest[len(rest) - 1 - ncast:len(rest) - 1]
    f = pl.program_id(1)

    @pl.when(f == 0)
    def _():
        x = x_ref[...]
        xn_ref[...] = _rms_normalize(x, g_ref[...]).astype(BF16)
        o_ref[...] = x

    h = jnp.maximum(_dot(xn_ref[...], w1_ref[...]), 0.0)
    for src, dst in zip(cast_in, cast_out):
        dst[...] = src[...].astype(dst.dtype)
    o_ref[...] += _dot((h * h).astype(BF16), w2_ref[...])

    @pl.when(f == pl.num_programs(1) - 1)
    def _():
        normed = _rms_normalize(o_ref[...], gn_ref[...])
        if tail == "final":
            o_ref[...] = normed
        else:
            rest[ncast + 1][...] = normed.astype(BF16)


def mlp_block(x, g, w1b, w2b, layer, gn, casts, *, tm, tf, tail):
    t, d = x.shape
    dff = w1b.shape[2]
    nf = dff // tf
    cast_in_specs, cast_out_specs, cast_shapes = _side_cast_specs(casts, (t // tm) * nf, lambda i, f: i * nf + f)
    out_shape = [jax.ShapeDtypeStruct((t, d), F32)]
    out_specs = [pl.BlockSpec((tm, d), lambda i, f: (i, 0))]
    if tail == "next":
        out_shape.append(jax.ShapeDtypeStruct((t, d), BF16))
        out_specs.append(pl.BlockSpec((tm, d), lambda i, f: (i, 0)))
    n_main = len(out_shape)
    outs = pl.pallas_call(
        functools.partial(_mlp_kernel, tail=tail, ncast=len(casts)),
        out_shape=(*out_shape, *cast_shapes),
        grid=(t // tm, nf),
        in_specs=[
            pl.BlockSpec((tm, d), lambda i, f: (i, 0)),
            pl.BlockSpec((1, d), lambda i, f: (0, 0)),
            pl.BlockSpec((None, d, tf), lambda i, f: (layer, 0, f)),
            pl.BlockSpec((None, tf, d), lambda i, f: (layer, f, 0)),
            pl.BlockSpec((1, d), lambda i, f: (0, 0)),
            *cast_in_specs,
        ],
        out_specs=(*out_specs, *cast_out_specs),
        scratch_shapes=[pltpu.VMEM((tm, d), BF16)],
        compiler_params=_params(("parallel", "arbitrary")),
        name="mlp_block",
    )(x, g.reshape(1, d), w1b, w2b, gn.reshape(1, d), *[w for w, _, _ in casts])
    return outs[:n_main], outs[n_main:]


def _mla_attn_kernel(q_ref, kv_ref, kr_ref, *rest, tq, heads, ncast):
    cast_in, o_ref, cast_out = rest[:ncast], rest[ncast], rest[ncast + 1:2 * ncast + 1]
    m_ref, l_ref, acc_ref = rest[2 * ncast + 1:]
    qi = pl.program_id(2)
    hw = 2 * LANES
    m_ref[...] = jnp.full(m_ref.shape, NEG_INF, F32)
    l_ref[...] = jnp.zeros(l_ref.shape, F32)
    acc_ref[...] = jnp.zeros(acc_ref.shape, F32)

    def block(key_start, nkeys, rows, first_visible):
        kr = kr_ref[pl.ds(key_start, nkeys), :]
        for h in range(heads):
            q = q_ref[rows, h * hw:(h + 1) * hw]
            kn = kv_ref[pl.ds(key_start, nkeys), h * hw:h * hw + MLA_NOPE]
            v = kv_ref[pl.ds(key_start, nkeys), h * hw + MLA_NOPE:(h + 1) * hw]
            s = _dot_nt(q, jnp.concatenate([kn, kr], axis=-1))
            if first_visible is not None:
                row = lax.broadcasted_iota(jnp.int32, s.shape, 0)
                col = lax.broadcasted_iota(jnp.int32, s.shape, 1)
                s = jnp.where(col <= row + first_visible, s, NEG_INF)
            m_prev = m_ref[h, rows]
            m_new = jnp.maximum(m_prev, jnp.max(s, axis=-1, keepdims=True))
            p = jnp.exp2(s - jnp.tile(m_new, (1, nkeys // LANES)))
            alpha = jnp.exp2(m_prev - m_new)
            l_ref[h, rows] = alpha * l_ref[h, rows] + jnp.sum(p, axis=-1, keepdims=True)
            acc_ref[h, rows] = alpha * acc_ref[h, rows] + _dot(p.astype(BF16), v)
            m_ref[h, rows] = m_new

    def body(ki, carry):
        block(pl.multiple_of(ki * tq, tq), tq, slice(0, tq), None)
        return carry

    lax.fori_loop(0, qi, body, 0)
    for src, dst in zip(cast_in, cast_out):
        dst[...] = src[...].astype(dst.dtype)
    block(pl.multiple_of(qi * tq, tq), tq, slice(0, tq), 0)
    for h in range(heads):
        o_ref[:, h * MLA_V:(h + 1) * MLA_V] = (acc_ref[h] / l_ref[h]).astype(o_ref.dtype)


def mla_attention(q, kv, kr, casts, *, tq, heads):
    nq = SEQ // tq
    hw = 2 * LANES * heads
    groups = MLA_HEADS // heads
    steps = BATCH * groups * nq

    cast_in_specs, cast_out_specs, cast_shapes = _side_cast_specs(
        casts, steps, lambda b, h, i: (b * groups + h) * nq + i)

    outs = pl.pallas_call(
        functools.partial(_mla_attn_kernel, tq=tq, heads=heads, ncast=len(casts)),
        out_shape=(jax.ShapeDtypeStruct((TOKENS, MLA_HEADS * MLA_V), BF16), *cast_shapes),
        grid=(BATCH, groups, nq),
        in_specs=[
            pl.BlockSpec((tq, hw), lambda b, h, i: (b * nq + i, h)),
            pl.BlockSpec((SEQ, hw), lambda b, h, i: (b, h)),
            pl.BlockSpec((SEQ, LANES), lambda b, h, i: (b, 0)),
            *cast_in_specs,
        ],
        out_specs=(pl.BlockSpec((tq, heads * MLA_V), lambda b, h, i: (b * nq + i, h)), *cast_out_specs),
        scratch_shapes=[pltpu.VMEM((heads, tq, LANES), F32), pltpu.VMEM((heads, tq, LANES), F32),
                        pltpu.VMEM((heads, tq, MLA_V), F32)],
        compiler_params=_params(("parallel", "parallel", "arbitrary")),
        name="mla_attention",
    )(q, kv, kr, *[w for w, _, _ in casts])
    return outs[0], outs[1:]


def _mlstm_kernel(qk_ref, v_ref, og_ref, gate_ref, cw_ref, cb_ref, gb_ref, out_ref,
                  hist_ref, c_ref, n_ref, m_ref):
    chunk = MLSTM_CHUNK
    c = pl.program_id(1)

    @pl.when(c == 0)
    def _():
        hist_ref[0:8, :] = jnp.zeros((8, hist_ref.shape[1]), F32)
        c_ref[...] = jnp.zeros(c_ref.shape, F32)
        n_ref[...] = jnp.zeros(n_ref.shape, F32)
        m_ref[...] = jnp.zeros(m_ref.shape, F32)

    hist_ref[8:8 + chunk, :] = qk_ref[...]
    y = jnp.broadcast_to(cb_ref[...], (chunk, hist_ref.shape[1]))
    for j in range(CONV_WIDTH):
        off = 8 - (CONV_WIDTH - 1) + j
        y = y + cw_ref[j:j + 1, :] * hist_ref[off:off + chunk, :]
    qk = y * _sigmoid(y)
    hist_ref[0:8, :] = hist_ref[chunk:chunk + 8, :]

    gates = gate_ref[...] + gb_ref[...]
    logf = _log_sigmoid(gates)
    row = lax.broadcasted_iota(jnp.int32, (chunk, chunk), 0)
    col = lax.broadcasted_iota(jnp.int32, (chunk, chunk), 1)
    tril = col <= row
    bcum = jnp.dot(tril.astype(F32), logf, preferred_element_type=F32,
                   precision=lax.Precision.HIGHEST)
    gates_t = gates.T
    bcum_t = bcum.T

    dk, dv = MLSTM_DK, MLSTM_DV
    for h in range(MLSTM_HEADS):
        q = qk[:, h * dk:(h + 1) * dk]
        k = qk[:, (MLSTM_HEADS + h) * dk:(MLSTM_HEADS + h + 1) * dk] * (dk ** -0.5)
        v = v_ref[:, h * dv:(h + 1) * dv].astype(BF16)
        b_c = bcum[:, TAIL_F + h:TAIL_F + h + 1]
        b_r = bcum_t[TAIL_F + h:TAIL_F + h + 1, :]
        i_c = gates[:, TAIL_I + h:TAIL_I + h + 1]
        i_r = gates_t[TAIL_I + h:TAIL_I + h + 1, :]
        b_last = bcum[chunk - 1:chunk, TAIL_F + h:TAIL_F + h + 1]
        m_prev = m_ref[h:h + 1, 0:1]
        n_prev = n_ref[h:h + 1, :]
        c_prev = c_ref[h]

        dmat = jnp.where(tril, b_c - b_r + i_r, NEG_INF)
        m_inter = b_c + m_prev
        m_row = jnp.maximum(m_inter, jnp.max(dmat, axis=-1, keepdims=True))
        q_b = q.astype(BF16)
        wmat = jnp.exp(dmat - m_row) * _dot_nt(q_b, k.astype(BF16))
        inter = jnp.exp(m_inter - m_row)
        num = _dot(wmat.astype(BF16), v) + inter * _dot(q_b, c_prev.astype(BF16))
        den = jnp.sum(wmat, axis=-1, keepdims=True) + inter * jnp.sum(q * n_prev, axis=-1, keepdims=True)
        hcell = num / jnp.maximum(jnp.abs(den), jnp.exp(-m_row))
        gate_o = _sigmoid(og_ref[:, h * dv:(h + 1) * dv])
        out_ref[:, h * dv:(h + 1) * dv] = (gate_o * hcell).astype(out_ref.dtype)

        g_c = b_last - b_c + i_c
        m_new = jnp.maximum(b_last + m_prev, jnp.max(g_c, axis=0, keepdims=True))
        wk = jnp.exp(g_c - m_new)
        decay = jnp.exp(b_last + m_prev - m_new)
        kw = k * wk
        c_ref[h] = decay * c_prev + _dot_tn(kw.astype(BF16), v)
        n_ref[h:h + 1, :] = decay * n_prev + jnp.sum(kw, axis=0, keepdims=True)
        m_ref[h:h + 1, :] = jnp.broadcast_to(m_new, (1, LANES))


def mlstm(proj, conv_w, conv_b, gate_bias):
    nc = SEQ // MLSTM_CHUNK
    chunk = MLSTM_CHUNK
    wqk = 2 * MLSTM_HEADS * MLSTM_DK
    wv = MLSTM_HEADS * MLSTM_DV
    row = lambda b, c: b * nc + c
    return pl.pallas_call(
        _mlstm_kernel,
        out_shape=jax.ShapeDtypeStruct((TOKENS, wv), BF16),
        grid=(BATCH, nc),
        in_specs=[
            pl.BlockSpec((chunk, wqk), lambda b, c: (row(b, c), EV_MQK // wqk)),
            pl.BlockSpec((chunk, wv), lambda b, c: (row(b, c), EV_MV // wv)),
            pl.BlockSpec((chunk, wv), lambda b, c: (row(b, c), EV_MO // wv)),
            pl.BlockSpec((chunk, LANES), lambda b, c: (row(b, c), EV_TAIL // LANES)),
            pl.BlockSpec((CONV_WIDTH, wqk), lambda b, c: (0, 0)),
            pl.BlockSpec((1, wqk), lambda b, c: (0, 0)),
            pl.BlockSpec((1, LANES), lambda b, c: (0, 0)),
        ],
        out_specs=pl.BlockSpec((chunk, wv), lambda b, c: (row(b, c), 0)),
        scratch_shapes=[
            pltpu.VMEM((chunk + 8, wqk), F32),
            pltpu.VMEM((MLSTM_HEADS, MLSTM_DK, MLSTM_DV), F32),
            pltpu.VMEM((8, MLSTM_DK), F32),
            pltpu.VMEM((8, LANES), F32),
        ],
        compiler_params=_params(("parallel", "arbitrary")),
        name="mlstm",
    )(proj, proj, proj, proj, conv_w, conv_b.reshape(1, wqk), gate_bias)


def _dilated_kernel(*refs, nres, nblk, has_prev, span, heads):
    if has_prev:
        q_ref, kp_ref, kc_ref, vp_ref, vc_ref, o_ref, lse_ref = refs
    else:
        q_ref, kc_ref, vc_ref, o_ref, lse_ref = refs
        kp_ref = vp_ref = None
    n = pl.program_id(2)
    hg = pl.program_id(3)
    blk_sz = DIL_BLOCK
    dh = DIL_HEAD_DIM
    nkeys = 2 * blk_sz if has_prev else blk_sz
    qi = lax.broadcasted_iota(jnp.int32, (blk_sz, nkeys), 0)
    kj = lax.broadcasted_iota(jnp.int32, (blk_sz, nkeys), 1)
    dist = (nkeys - blk_sz) + qi - kj
    bias_full = jnp.where(dist >= 0, jnp.where(dist <= span, 0.0, NEG_INF), NEG_INF)
    bias_first = jnp.where(kj >= nkeys - blk_sz, bias_full, NEG_INF)
    lane = lax.broadcasted_iota(jnp.int32, (blk_sz, LANES), 1)

    @pl.when(hg == 0)
    def _():
        lse_ref[...] = jnp.zeros(lse_ref.shape, F32)

    def stack(ref, res, rows):
        return jnp.stack([ref[res, rows, h * dh:(h + 1) * dh] for h in range(heads)])

    for res in range(nres):
        for blk in range(nblk):
            rows = slice(blk * blk_sz, (blk + 1) * blk_sz)
            q3 = stack(q_ref, res, rows)
            k3 = stack(kc_ref, res, rows)
            v3 = stack(vc_ref, res, rows)
            bias = bias_full
            if has_prev:
                if blk == 0:
                    first = slice(0, blk_sz)
                    kp3, vp3 = stack(kp_ref, res, first), stack(vp_ref, res, first)
                    bias = jnp.where(n > 0, bias_full, bias_first)
                else:
                    prows = slice((blk - 1) * blk_sz, blk * blk_sz)
                    kp3, vp3 = stack(kc_ref, res, prows), stack(vc_ref, res, prows)
                k3 = jnp.concatenate([kp3, k3], axis=1)
                v3 = jnp.concatenate([vp3, v3], axis=1)
            s = jnp.einsum("hqd,hkd->hqk", q3, k3, preferred_element_type=F32) + bias[None]
            m = jnp.max(s, axis=-1, keepdims=True)
            p = jnp.exp2(s - m)
            den = jnp.sum(p, axis=-1, keepdims=True)
            o = jnp.einsum("hqk,hkd->hqd", p.astype(BF16), v3, preferred_element_type=F32) / den
            lse = m * LN2 + jnp.log(den)
            lse_tile = lse_ref[res, rows, :]
            for h in range(heads):
                o_ref[res, rows, h * dh:(h + 1) * dh] = o[h].astype(o_ref.dtype)
                lse_tile = jnp.where(lane == hg * heads + h, lse[h], lse_tile)
            lse_ref[res, rows, :] = lse_tile


def dilated_pattern(qkv_perm, window, dilation, *, hw, units):
    span = window // dilation
    length = SEQ // dilation
    nblk = min(length // DIL_BLOCK, units)
    nres = min(dilation, units // nblk)
    tl = nblk * DIL_BLOCK
    ntile = length // tl
    has_prev = length > DIL_BLOCK
    heads = hw // DIL_HEAD_DIM
    ngroups = ODD_MIX // hw

    def cur(which):
        return pl.BlockSpec((None, nres, tl, hw), lambda b, r, n, g: (b, r, n, which * ngroups + g))

    def prev(which):
        return pl.BlockSpec((None, nres, DIL_BLOCK, hw),
                            lambda b, r, n, g: (b, r, jnp.maximum(n * nblk - 1, 0), which * ngroups + g))

    if has_prev:
        in_specs = [cur(0), prev(1), cur(1), prev(2), cur(2)]
    else:
        in_specs = [cur(0), cur(1), cur(2)]
    return pl.pallas_call(
        functools.partial(_dilated_kernel, nres=nres, nblk=nblk, has_prev=has_prev, span=span, heads=heads),
        out_shape=(jax.ShapeDtypeStruct((BATCH, dilation, length, ODD_MIX), BF16),
                   jax.ShapeDtypeStruct((BATCH, dilation, length, LANES), F32)),
        grid=(BATCH, dilation // nres, ntile, ngroups),
        in_specs=in_specs,
        out_specs=(pl.BlockSpec((None, nres, tl, hw), lambda b, r, n, g: (b, r, n, g)),
                   pl.BlockSpec((None, nres, tl, LANES), lambda b, r, n, g: (b, r, n, 0))),
        compiler_params=_params(("parallel", "parallel", "arbitrary", "arbitrary")),
        name="dilated_d%d" % dilation,
    )(*([qkv_perm] * len(in_specs)))


def _merge_kernel(o1_ref, o4_ref, o16_ref, l1_ref, l4_ref, l16_ref, out_ref,
                  nat_scr, p4_scr, o4_scr, o_scr, *, tm):
    dh = DIL_HEAD_DIM
    q4, q16 = tm // 4, tm // 16

    for r4 in range(4):
        for a in range(4):
            p4_scr[r4, pl.ds(a, q16, stride=4), :] = l16_ref[r4 + 4 * a]
    for r4 in range(4):
        nat_scr[0, pl.ds(r4, q4, stride=4), :] = l4_ref[r4]
        nat_scr[1, pl.ds(r4, q4, stride=4), :] = p4_scr[r4]
    l1, l4, l16 = l1_ref[0], nat_scr[0], nat_scr[1]
    mx = jnp.maximum(jnp.maximum(l1, l4), l16)
    e1, e4, e16 = jnp.exp(l1 - mx), jnp.exp(l4 - mx), jnp.exp(l16 - mx)
    tot = e1 + e4 + e16
    w1 = e1 / tot
    nat_scr[0] = e4 / tot
    nat_scr[1] = e16 / tot
    w4 = jnp.concatenate([nat_scr[0, pl.ds(r4, q4, stride=4), :] for r4 in range(4)], axis=0)
    for r4 in range(4):
        p4_scr[r4] = nat_scr[1, pl.ds(r4, q4, stride=4), :]
    w16_parts = [None] * 16
    for r4 in range(4):
        for a in range(4):
            w16_parts[r4 + 4 * a] = p4_scr[r4, pl.ds(a, q16, stride=4), :]
    w16 = jnp.concatenate(w16_parts, axis=0)

    head_of_col = lax.broadcasted_iota(jnp.int32, (2 * LANES, ODD_MIX), 1) // dh
    lane_of_row = lax.broadcasted_iota(jnp.int32, (2 * LANES, ODD_MIX), 0) % LANES
    expand = jnp.where(head_of_col == lane_of_row, 1.0, 0.0).astype(BF16)

    def two_terms(w):
        hi = w.astype(BF16)
        return jnp.concatenate([hi, (w - hi.astype(F32)).astype(BF16)], axis=1)

    w1, w4, w16 = two_terms(w1), two_terms(w4), two_terms(w16)

    pair = 2
    for hp in range(DIL_HEADS // pair):
        spread = expand[:, hp * pair * dh:(hp + 1) * pair * dh]
        w1p, w4p, w16p = _dot(w1, spread), _dot(w4, spread), _dot(w16, spread)
        for hh in range(pair):
            cols = slice((hp * pair + hh) * dh, (hp * pair + hh + 1) * dh)
            part = slice(hh * dh, (hh + 1) * dh)
            for r4 in range(4):
                for a in range(4):
                    r = r4 + 4 * a
                    o4_scr[r4, pl.ds(a, q16, stride=4), :] = (w16p[r * q16:(r + 1) * q16, part]
                                                             * o16_ref[r, :, cols])
            for r4 in range(4):
                acc4 = o4_scr[r4] + w4p[r4 * q4:(r4 + 1) * q4, part] * o4_ref[r4, :, cols]
                o_scr[pl.ds(r4, q4, stride=4), :] = acc4
            out_ref[:, cols] = (o_scr[...] + w1p[:, part] * o1_ref[0, :, cols]).astype(out_ref.dtype)


def merge_patterns(outs, lses, dilations, *, tm):
    assert tuple(dilations) == (1, 4, 16), "the re-interleave is written as two stride-4 passes"
    tiles = SEQ // tm

    def spec(d, width):
        return pl.BlockSpec((None, d, tm // d, width), lambda i: (i // tiles, 0, i % tiles, 0))

    return pl.pallas_call(
        functools.partial(_merge_kernel, tm=tm),
        out_shape=jax.ShapeDtypeStruct((TOKENS, ODD_MIX), BF16),
        grid=(TOKENS // tm,),
        in_specs=[spec(d, ODD_MIX) for d in dilations] + [spec(d, LANES) for d in dilations],
        out_specs=pl.BlockSpec((tm, ODD_MIX), lambda i: (i, 0)),
        scratch_shapes=[pltpu.VMEM((2, tm, LANES), F32),
                        pltpu.VMEM((4, tm // 4, LANES), F32),
                        pltpu.VMEM((4, tm // 4, DIL_HEAD_DIM), F32),
                        pltpu.VMEM((tm, DIL_HEAD_DIM), F32)],
        compiler_params=_params(("parallel",)),
        name="merge_patterns",
    )(*outs, *lses)


def _rope_tables():
    pos = jnp.arange(SEQ, dtype=F32)[:, None]
    half = MLA_ROPE // 2
    inv = ROPE_THETA ** (-jnp.arange(half, dtype=F32) * 2.0 / MLA_ROPE)
    ang = pos * inv[None, :]
    c, s = jnp.cos(ang), jnp.sin(ang)
    z = jnp.zeros_like(c)
    small = jnp.concatenate([c, c, z, z, -s, z, z, z, z, s, z, z], axis=-1)
    half = DIL_HEAD_DIM // 2
    inv = ROPE_THETA ** (-jnp.arange(half, dtype=F32) * 2.0 / DIL_HEAD_DIM)
    ang = pos * inv[None, :]
    c, s = jnp.cos(ang), jnp.sin(ang)
    full = jnp.concatenate([c, c, -s, s], axis=-1)
    return small, full


def _even_weights(w_in, w_uq, w_ukv, b_i, b_f):
    cuts = np.cumsum((MLA_Q_LORA, MLA_KV_LORA, MLA_ROPE, 2 * MLSTM_HEADS * MLSTM_DK,
                      MLSTM_HEADS * MLSTM_DV, MLSTM_HEADS, MLSTM_HEADS))
    c_q, c_kv, k_r, m_qk, m_v, m_i, m_f, m_o = jnp.split(w_in, cuts.tolist(), axis=1)
    pad = jnp.zeros((D_MODEL, EV_PROJ - EV_TAIL - MLA_ROPE - 2 * MLSTM_HEADS), w_in.dtype)
    w_in_r = jnp.concatenate([c_q, c_kv, m_qk, m_v, m_o, k_r, m_i, m_f, pad], axis=1).astype(BF16)
    uq = w_uq.reshape(MLA_Q_LORA, MLA_HEADS, MLA_QK)
    uq = jnp.pad(uq, ((0, 0), (0, 0), (0, 2 * LANES - MLA_QK)))
    w_uq_r = uq.reshape(MLA_Q_LORA, MLA_HEADS * 2 * LANES).astype(BF16)
    gate_bias = jnp.concatenate([jnp.zeros((TAIL_I,), F32), b_i.astype(F32), b_f.astype(F32),
                                 jnp.zeros((LANES - TAIL_F - MLSTM_HEADS,), F32)]).reshape(1, LANES)
    return w_in_r, w_uq_r, w_ukv.astype(BF16), gate_bias


def _even_layer(x, xn, w_in, q_norm, w_uq, kv_norm, w_ukv, conv_w, conv_b, b_i, b_f, small_tab, casts):
    w_in_r, w_uq_r, w_ukv_r, gate_bias = _even_weights(w_in, w_uq, w_ukv, b_i, b_f)
    proj = matmul_weight_stationary(xn, w_in_r, tm=1024, tn=EV_PROJ // 3, out_dtype=F32)
    scale = MLA_QK ** -0.5 * LOG2E
    q = rms_matmul(proj, EV_CQ // MLA_Q_LORA, MLA_Q_LORA, q_norm, w_uq_r, tm=512, tn=2048,
                   out_dtype=BF16, mode="mlaq", tab=small_tab * scale, scale=scale)
    kv = rms_matmul(proj, EV_CKV // MLA_KV_LORA, MLA_KV_LORA, kv_norm, w_ukv_r, tm=512, tn=2048,
                    out_dtype=BF16)
    kr = krope(proj, small_tab, tm=1024)
    a_out, cast_weights = mla_attention(q, kv, kr, casts, tq=512, heads=4)
    hm = mlstm(proj, conv_w, conv_b, gate_bias)
    w_out_b = cast_weights[0][0]
    assert a_out.shape[1] == hm.shape[1]
    return matmul_residual(x, [(a_out, w_out_b, 0), (hm, w_out_b, 1)], tm=512), cast_weights


def _odd_layer(x, xn, w_qkv_b, w_out_b, full_tab):
    scale = DIL_HEAD_DIM ** -0.5 * LOG2E
    identity = jnp.concatenate([jnp.ones((SEQ, LANES), F32), jnp.zeros((SEQ, LANES), F32)], axis=-1)
    tab = jnp.stack([full_tab * scale, full_tab, identity])
    dilations = [d for _, d in DIL_PATTERNS]
    qkv_perms = qkv_projection(xn, w_qkv_b, tab, dilations, tm=1024, tn=1024)
    outs, lses = [], []
    for (window, dilation), qkv_perm in zip(DIL_PATTERNS, qkv_perms):
        o_g, lse_g = dilated_pattern(qkv_perm, window, dilation, hw=1024, units=8)
        outs.append(o_g)
        lses.append(lse_g)
    o = merge_patterns(outs, lses, dilations, tm=512)
    return matmul_residual(x, [(o, w_out_b, 0)], tm=512)


def kernel(x, norm_mix, norm_mlp, ev_w_in, mla_q_norm, mla_w_uq, mla_kv_norm, mla_w_ukv,
           mlstm_conv_w, mlstm_conv_b, mlstm_b_i, mlstm_b_f, ev_w_out, od_w_qkv, od_w_out,
           mlp_w1, mlp_w2, norm_final):
    assert x.shape == (BATCH, SEQ, D_MODEL) and x.dtype == F32
    small_tab, full_tab = _rope_tables()
    xt = x.reshape(TOKENS, D_MODEL)
    xn = rms_norm_bf16(xt, norm_mix[0], tm=512)
    w1b = w2b = w_qkv_b = w_od_out_b = None
    for layer in range(DEPTH):
        i = layer // 2
        if layer % 2 == 0:
            casts = [(ev_w_out, i, 1)]
            if layer + 1 < DEPTH:
                casts += [(od_w_qkv, i, 1), (od_w_out, i, 1)]
            if layer == 0:
                casts += [(mlp_w1, 0, 1), (mlp_w2, 0, 1)]
            xt, cast_weights = _even_layer(xt, xn, ev_w_in[i], mla_q_norm[i], mla_w_uq[i], mla_kv_norm[i],
                                           mla_w_ukv[i], mlstm_conv_w[i], mlstm_conv_b[i], mlstm_b_i[i],
                                           mlstm_b_f[i], small_tab, casts)
            if layer + 1 < DEPTH:
                w_qkv_b, w_od_out_b = cast_weights[1][0], cast_weights[2][0]
            if layer == 0:
                w1b, w2b = cast_weights[-2], cast_weights[-1]
        else:
            xt = _odd_layer(xt, xn, w_qkv_b, w_od_out_b, full_tab)
        if layer == DEPTH - 1:
            (xt,), _ = mlp_block(xt, norm_mlp[layer], w1b, w2b, 0, norm_final, [], tm=512, tf=1024,
                                 tail="final")
        else:
            casts = [(mlp_w1, layer + 1, 1), (mlp_w2, layer + 1, 1)]
            (xt, xn), (w1b, w2b) = mlp_block(xt, norm_mlp[layer], w1b, w2b, 0, norm_mix[layer + 1], casts,
                                             tm=512, tf=1024, tail="next")
    return xt.reshape(BATCH, SEQ, D_MODEL)
```

```python
import functools

import jax
import jax.numpy as jnp
import numpy as np
from jax import lax
from jax.experimental import pallas as pl
from jax.experimental.pallas import tpu as pltpu

D_MODEL = 2048
BATCH = 4
SEQ = 2048
DEPTH = 4
MLA_HEADS = 8
MLA_Q_LORA = 512
MLA_KV_LORA = 512
MLA_NOPE = 128
MLA_ROPE = 64
MLA_V = 128
MLA_QK = MLA_NOPE + MLA_ROPE
MLSTM_HEADS = 4
MLSTM_DK = 128
MLSTM_DV = 256
MLSTM_CHUNK = 128
CONV_WIDTH = 4
DIL_HEADS = 16
DIL_HEAD_DIM = 128
DIL_PATTERNS = ((128, 1), (512, 4), (2048, 16))
DIL_BLOCK = 128
D_FF = 4 * D_MODEL
ROPE_THETA = 10000.0
NORM_EPS = 1e-6
ODD_MIX = DIL_HEADS * DIL_HEAD_DIM
TOKENS = BATCH * SEQ

LANES = 128
VMEM_LIMIT_BYTES = 56 * 1024 * 1024

EV_CQ, EV_CKV, EV_MQK, EV_MV, EV_MO, EV_TAIL = 0, 512, 1024, 2048, 3072, 4096
EV_PROJ = 4224
TAIL_I = MLA_ROPE
TAIL_F = MLA_ROPE + MLSTM_HEADS

F32 = jnp.float32
BF16 = jnp.bfloat16
NEG_INF = float("-inf")
LOG2E = float(np.log2(np.e))
LN2 = float(np.log(2.0))


def _params(semantics):
    return pltpu.CompilerParams(dimension_semantics=semantics, vmem_limit_bytes=VMEM_LIMIT_BYTES)


def _rms_normalize(x, g):
    ms = jnp.mean(x * x, axis=-1, keepdims=True)
    return x * lax.rsqrt(ms + NORM_EPS) * g


def _dot(a, b):
    return jnp.dot(a, b, preferred_element_type=F32)


def _dot_nt(a, b):
    return lax.dot_general(a, b, (((1,), (1,)), ((), ())), preferred_element_type=F32)


def _dot_tn(a, b):
    return lax.dot_general(a, b, (((0,), (0,)), ((), ())), preferred_element_type=F32)


def _sigmoid(x):
    return 1.0 / (1.0 + jnp.exp(-x))


def _log_sigmoid(x):
    return jnp.minimum(x, 0.0) - jnp.log1p(jnp.exp(-jnp.abs(x)))


def _rope_small(y, tab):
    return (y * tab[:, 0:LANES] + pltpu.roll(y, 96, 1) * tab[:, LANES:2 * LANES]
            + pltpu.roll(y, 32, 1) * tab[:, 2 * LANES:3 * LANES])


def _rope_full(y, tab):
    return y * tab[:, 0:LANES] + pltpu.roll(y, 64, 1) * tab[:, LANES:2 * LANES]


def _mla_prep_kernel(cq_ref, ckv_ref, tail_ref, gq_ref, gkv_ref, wq_ref, wkv_ref, tabq_ref, tabk_ref,
                     q_ref, kv_ref, kr_ref, *, scale):
    cq = _rms_normalize(cq_ref[...], gq_ref[...]).astype(BF16)
    ckv = _rms_normalize(ckv_ref[...], gkv_ref[...]).astype(BF16)
    kv_ref[...] = _dot(ckv, wkv_ref[...]).astype(kv_ref.dtype)
    y = _dot(cq, wq_ref[...])
    tab = tabq_ref[...]
    for grp in range(y.shape[1] // LANES):
        yg = y[:, grp * LANES:(grp + 1) * LANES]
        if grp % 2 == 0:
            og = yg * scale
        else:
            og = yg * tab[:, 0:LANES] + pltpu.roll(yg, 32, 1) * tab[:, LANES:2 * LANES]
        q_ref[:, grp * LANES:(grp + 1) * LANES] = og.astype(q_ref.dtype)
    kr_ref[...] = _rope_small(tail_ref[...], tabk_ref[...]).astype(kr_ref.dtype)


def mla_prep(proj, q_norm, kv_norm, w_uq_r, w_ukv, tab_q, tab_k, *, tm, scale):
    pos_tiles = SEQ // tm
    nq, nkv = w_uq_r.shape[1], w_ukv.shape[1]
    return pl.pallas_call(
        functools.partial(_mla_prep_kernel, scale=scale),
        out_shape=(jax.ShapeDtypeStruct((TOKENS, nq), BF16), jax.ShapeDtypeStruct((TOKENS, nkv), BF16),
                   jax.ShapeDtypeStruct((TOKENS, LANES), BF16)),
        grid=(TOKENS // tm,),
        in_specs=[
            pl.BlockSpec((tm, MLA_Q_LORA), lambda i: (i, EV_CQ // MLA_Q_LORA)),
            pl.BlockSpec((tm, MLA_KV_LORA), lambda i: (i, EV_CKV // MLA_KV_LORA)),
            pl.BlockSpec((tm, LANES), lambda i: (i, EV_TAIL // LANES)),
            pl.BlockSpec((1, MLA_Q_LORA), lambda i: (0, 0)),
            pl.BlockSpec((1, MLA_KV_LORA), lambda i: (0, 0)),
            pl.BlockSpec((MLA_Q_LORA, nq), lambda i: (0, 0)),
            pl.BlockSpec((MLA_KV_LORA, nkv), lambda i: (0, 0)),
            pl.BlockSpec((tm, 2 * LANES), lambda i: (i % pos_tiles, 0)),
            pl.BlockSpec((tm, 3 * LANES), lambda i: (i % pos_tiles, 0)),
        ],
        out_specs=(pl.BlockSpec((tm, nq), lambda i: (i, 0)), pl.BlockSpec((tm, nkv), lambda i: (i, 0)),
                   pl.BlockSpec((tm, LANES), lambda i: (i, 0))),
        compiler_params=_params(("parallel",)),
        name="mla_prep",
    )(proj, proj, proj, q_norm.reshape(1, MLA_Q_LORA), kv_norm.reshape(1, MLA_KV_LORA), w_uq_r, w_ukv,
      tab_q, tab_k)


def _qkv_proj_kernel(xn_ref, w_ref, tab_ref, *rest, dilations, tm, sub_cols):
    out_refs, (scr_ref, scr4_ref) = rest[:len(dilations)], rest[len(dilations):]
    out_by_d = dict(zip(dilations, out_refs))
    tab = tab_ref[...]
    for sub in range(w_ref.shape[1] // sub_cols):
        y = _dot(xn_ref[...], w_ref[:, sub * sub_cols:(sub + 1) * sub_cols])
        for part in range(sub_cols // LANES):
            grp = sub * (sub_cols // LANES) + part
            cols = slice(grp * LANES, (grp + 1) * LANES)
            yg = _rope_full(y[:, part * LANES:(part + 1) * LANES], tab)
            out_by_d[1][0, :, cols] = yg.astype(BF16)
            scr_ref[grp] = yg
            for r4 in range(4):
                v = scr_ref[grp, pl.ds(r4, tm // 4, stride=4), :]
                out_by_d[4][r4, :, cols] = v.astype(BF16)
                scr4_ref[r4] = v
            for r4 in range(4):
                for a in range(4):
                    w = scr4_ref[r4, pl.ds(a, tm // 16, stride=4), :]
                    out_by_d[16][r4 + 4 * a, :, cols] = w.astype(BF16)


def qkv_projection(xn, w, tab, dilations, *, tm, tn):
    assert tuple(dilations) == (1, 4, 16), "the de-interleave is written as two stride-4 passes"
    t, k_dim = xn.shape
    n = w.shape[1]
    tiles = SEQ // tm
    q_tiles = ODD_MIX // tn
    return pl.pallas_call(
        functools.partial(_qkv_proj_kernel, dilations=tuple(dilations), tm=tm, sub_cols=2 * LANES),
        out_shape=tuple(jax.ShapeDtypeStruct((BATCH, d, SEQ // d, n), BF16) for d in dilations),
        grid=(n // tn, t // tm),
        in_specs=[
            pl.BlockSpec((tm, k_dim), lambda j, i: (i, 0)),
            pl.BlockSpec((k_dim, tn), lambda j, i: (0, j)),
            pl.BlockSpec((None, tm, 2 * LANES), lambda j, i: (j // q_tiles, i % tiles, 0)),
        ],
        out_specs=tuple(pl.BlockSpec((None, d, tm // d, tn), lambda j, i: (i // tiles, 0, i % tiles, j))
                        for d in dilations),
        scratch_shapes=[pltpu.VMEM((tn // LANES, tm, LANES), F32), pltpu.VMEM((4, tm // 4, LANES), F32)],
        compiler_params=_params(("parallel", "arbitrary")),
        name="qkv_projection",
    )(xn, w, tab)


def _norm_kernel(x_ref, g_ref, o_ref):
    o_ref[...] = _rms_normalize(x_ref[...], g_ref[...]).astype(o_ref.dtype)


def rms_norm_bf16(x, g, *, tm):
    t, d = x.shape
    return pl.pallas_call(
        _norm_kernel,
        out_shape=jax.ShapeDtypeStruct((t, d), BF16),
        grid=(t // tm,),
        in_specs=[pl.BlockSpec((tm, d), lambda i: (i, 0)), pl.BlockSpec((1, d), lambda i: (0, 0))],
        out_specs=pl.BlockSpec((tm, d), lambda i: (i, 0)),
        compiler_params=_params(("parallel",)),
        name="rms_norm_bf16",
    )(x, g.reshape(1, d))


def _mm_kernel(a_ref, w_ref, o_ref):
    o_ref[...] = _dot(a_ref[...], w_ref[...]).astype(o_ref.dtype)


def matmul_weight_stationary(a, w, *, tm, tn, out_dtype):
    t, k = a.shape
    n = w.shape[1]
    return pl.pallas_call(
        _mm_kernel,
        out_shape=jax.ShapeDtypeStruct((t, n), out_dtype),
        grid=(n // tn, t // tm),
        in_specs=[pl.BlockSpec((tm, k), lambda j, i: (i, 0)), pl.BlockSpec((k, tn), lambda j, i: (0, j))],
        out_specs=pl.BlockSpec((tm, tn), lambda j, i: (i, j)),
        compiler_params=_params(("parallel", "arbitrary")),
        name="matmul_weight_stationary",
    )(a, w)


def _mm_res_kernel(*refs):
    r_ref, o_ref = refs[-2], refs[-1]
    acc = r_ref[...]
    for k in range((len(refs) - 2) // 2):
        acc = acc + _dot(refs[2 * k][...], refs[2 * k + 1][...])
    o_ref[...] = acc


def matmul_residual(res, pairs, *, tm):
    t, n = res.shape
    in_specs, args = [], []
    for a, w, row_block in pairs:
        k = a.shape[1]
        in_specs += [pl.BlockSpec((tm, k), lambda i: (i, 0)),
                     pl.BlockSpec((k, n), lambda i, row_block=row_block: (row_block, 0))]
        args += [a, w]
    in_specs.append(pl.BlockSpec((tm, n), lambda i: (i, 0)))
    args.append(res)
    return pl.pallas_call(
        _mm_res_kernel,
        out_shape=jax.ShapeDtypeStruct((t, n), F32),
        grid=(t // tm,),
        in_specs=in_specs,
        out_specs=pl.BlockSpec((tm, n), lambda i: (i, 0)),
        compiler_params=_params(("parallel",)),
        name="matmul_residual",
    )(*args)


def _side_cast_specs(casts, steps, step_of):
    in_specs, out_specs, out_shapes = [], [], []
    for w, first, count in casts:
        _, rows, cols = w.shape
        slabs = steps // count
        assert steps % count == 0 and rows % slabs == 0 and (rows // slabs) % 16 == 0
        block = (None, rows // slabs, cols)
        in_specs.append(pl.BlockSpec(
            block, lambda *idx, first=first, slabs=slabs: (first + step_of(*idx) // slabs,
                                                           step_of(*idx) % slabs, 0)))
        out_specs.append(pl.BlockSpec(
            block, lambda *idx, slabs=slabs: (step_of(*idx) // slabs, step_of(*idx) % slabs, 0)))
        out_shapes.append(jax.ShapeDtypeStruct((count, rows, cols), BF16))
    return in_specs, out_specs, out_shapes


def _mlp_kernel(x_ref, g_ref, w1_ref, w2_ref, gn_ref, *rest, tail, ncast):
    cast_in, o_ref, xn_ref = rest[:ncast], rest[ncast], rest[-1]
    cast_out = rest[len(rest) - 1 - ncast:len(rest) - 1]
    f = pl.program_id(1)

    @pl.when(f == 0)
    def _():
        x = x_ref[...]
        xn_ref[...] = _rms_normalize(x, g_ref[...]).astype(BF16)
        o_ref[...] = x

    h = jnp.maximum(_dot(xn_ref[...], w1_ref[...]), 0.0)
    for src, dst in zip(cast_in, cast_out):
        dst[...] = src[...].astype(dst.dtype)
    o_ref[...] += _dot((h * h).astype(BF16), w2_ref[...])

    @pl.when(f == pl.num_programs(1) - 1)
    def _():
        normed = _rms_normalize(o_ref[...], gn_ref[...])
        if tail == "final":
            o_ref[...] = normed
        else:
            rest[ncast + 1][...] = normed.astype(BF16)


def mlp_block(x, g, w1b, w2b, layer, gn, casts, *, tm, tf, tail):
    t, d = x.shape
    dff = w1b.shape[2]
    nf = dff // tf
    cast_in_specs, cast_out_specs, cast_shapes = _side_cast_specs(casts, (t // tm) * nf, lambda i, f: i * nf + f)
    out_shape = [jax.ShapeDtypeStruct((t, d), F32)]
    out_specs = [pl.BlockSpec((tm, d), lambda i, f: (i, 0))]
    if tail == "next":
        out_shape.append(jax.ShapeDtypeStruct((t, d), BF16))
        out_specs.append(pl.BlockSpec((tm, d), lambda i, f: (i, 0)))
    n_main = len(out_shape)
    outs = pl.pallas_call(
        functools.partial(_mlp_kernel, tail=tail, ncast=len(casts)),
        out_shape=(*out_shape, *cast_shapes),
        grid=(t // tm, nf),
        in_specs=[
            pl.BlockSpec((tm, d), lambda i, f: (i, 0)),
            pl.BlockSpec((1, d), lambda i, f: (0, 0)),
            pl.BlockSpec((None, d, tf), lambda i, f: (layer, 0, f)),
            pl.BlockSpec((None, tf, d), lambda i, f: (layer, f, 0)),
            pl.BlockSpec((1, d), lambda i, f: (0, 0)),
            *cast_in_specs,
        ],
        out_specs=(*out_specs, *cast_out_specs),
        scratch_shapes=[pltpu.VMEM((tm, d), BF16)],
        compiler_params=_params(("parallel", "arbitrary")),
        name="mlp_block",
    )(x, g.reshape(1, d), w1b, w2b, gn.reshape(1, d), *[w for w, _, _ in casts])
    return outs[:n_main], outs[n_main:]


def _mla_attn_kernel(q_ref, kv_ref, kr_ref, *rest, tq, heads, ncast):
    cast_in, o_ref, cast_out = rest[:ncast], rest[ncast], rest[ncast + 1:2 * ncast + 1]
    m_ref, l_ref, acc_ref = rest[2 * ncast + 1:]
    qi = pl.program_id(2)
    hw = 2 * LANES
    m_ref[...] = jnp.full(m_ref.shape, NEG_INF, F32)
    l_ref[...] = jnp.zeros(l_ref.shape, F32)
    acc_ref[...] = jnp.zeros(acc_ref.shape, F32)

    def block(key_start, nkeys, rows, first_visible):
        kr = kr_ref[pl.ds(key_start, nkeys), :]
        for h in range(heads):
            q = q_ref[rows, h * hw:(h + 1) * hw]
            kn = kv_ref[pl.ds(key_start, nkeys), h * hw:h * hw + MLA_NOPE]
            v = kv_ref[pl.ds(key_start, nkeys), h * hw + MLA_NOPE:(h + 1) * hw]
            s = _dot_nt(q, jnp.concatenate([kn, kr], axis=-1))
            if first_visible is not None:
                row = lax.broadcasted_iota(jnp.int32, s.shape, 0)
                col = lax.broadcasted_iota(jnp.int32, s.shape, 1)
                s = jnp.where(col <= row + first_visible, s, NEG_INF)
            m_prev = m_ref[h, rows]
            m_new = jnp.maximum(m_prev, jnp.max(s, axis=-1, keepdims=True))
            p = jnp.exp2(s - jnp.tile(m_new, (1, nkeys // LANES)))
            alpha = jnp.exp2(m_prev - m_new)
            l_ref[h, rows] = alpha * l_ref[h, rows] + jnp.sum(p, axis=-1, keepdims=True)
            acc_ref[h, rows] = alpha * acc_ref[h, rows] + _dot(p.astype(BF16), v)
            m_ref[h, rows] = m_new

    def body(ki, carry):
        block(pl.multiple_of(ki * tq, tq), tq, slice(0, tq), None)
        return carry

    lax.fori_loop(0, qi, body, 0)
    for src, dst in zip(cast_in, cast_out):
        dst[...] = src[...].astype(dst.dtype)
    block(pl.multiple_of(qi * tq, tq), tq, slice(0, tq), 0)
    for h in range(heads):
        o_ref[:, h * MLA_V:(h + 1) * MLA_V] = (acc_ref[h] / l_ref[h]).astype(o_ref.dtype)


def mla_attention(q, kv, kr, casts, *, tq, heads):
    nq = SEQ // tq
    hw = 2 * LANES * heads
    groups = MLA_HEADS // heads
    steps = BATCH * groups * nq

    cast_in_specs, cast_out_specs, cast_shapes = _side_cast_specs(
        casts, steps, lambda b, h, i: (b * groups + h) * nq + i)

    outs = pl.pallas_call(
        functools.partial(_mla_attn_kernel, tq=tq, heads=heads, ncast=len(casts)),
        out_shape=(jax.ShapeDtypeStruct((TOKENS, MLA_HEADS * MLA_V), BF16), *cast_shapes),
        grid=(BATCH, groups, nq),
        in_specs=[
            pl.BlockSpec((tq, hw), lambda b, h, i: (b * nq + i, h)),
            pl.BlockSpec((SEQ, hw), lambda b, h, i: (b, h)),
            pl.BlockSpec((SEQ, LANES), lambda b, h, i: (b, 0)),
            *cast_in_specs,
        ],
        out_specs=(pl.BlockSpec((tq, heads * MLA_V), lambda b, h, i: (b * nq + i, h)), *cast_out_specs),
        scratch_shapes=[pltpu.VMEM((heads, tq, LANES), F32), pltpu.VMEM((heads, tq, LANES), F32),
                        pltpu.VMEM((heads, tq, MLA_V), F32)],
        compiler_params=_params(("parallel", "parallel", "arbitrary")),
        name="mla_attention",
    )(q, kv, kr, *[w for w, _, _ in casts])
    return outs[0], outs[1:]


def _mlstm_kernel(qk_ref, v_ref, og_ref, gate_ref, cw_ref, cb_ref, gb_ref, out_ref,
                  hist_ref, c_ref, n_ref, m_ref):
    chunk = MLSTM_CHUNK
    c = pl.program_id(1)

    @pl.when(c == 0)
    def _():
        hist_ref[0:8, :] = jnp.zeros((8, hist_ref.shape[1]), F32)
        c_ref[...] = jnp.zeros(c_ref.shape, F32)
        n_ref[...] = jnp.zeros(n_ref.shape, F32)
        m_ref[...] = jnp.zeros(m_ref.shape, F32)

    hist_ref[8:8 + chunk, :] = qk_ref[...]
    y = jnp.broadcast_to(cb_ref[...], (chunk, hist_ref.shape[1]))
    for j in range(CONV_WIDTH):
        off = 8 - (CONV_WIDTH - 1) + j
        y = y + cw_ref[j:j + 1, :] * hist_ref[off:off + chunk, :]
    qk = y * _sigmoid(y)
    hist_ref[0:8, :] = hist_ref[chunk:chunk + 8, :]

    gates = gate_ref[...] + gb_ref[...]
    logf = _log_sigmoid(gates)
    row = lax.broadcasted_iota(jnp.int32, (chunk, chunk), 0)
    col = lax.broadcasted_iota(jnp.int32, (chunk, chunk), 1)
    tril = col <= row
    bcum = jnp.dot(tril.astype(F32), logf, preferred_element_type=F32,
                   precision=lax.Precision.HIGHEST)
    gates_t = gates.T
    bcum_t = bcum.T

    dk, dv = MLSTM_DK, MLSTM_DV
    for h in range(MLSTM_HEADS):
        q = qk[:, h * dk:(h + 1) * dk]
        k = qk[:, (MLSTM_HEADS + h) * dk:(MLSTM_HEADS + h + 1) * dk] * (dk ** -0.5)
        v = v_ref[:, h * dv:(h + 1) * dv].astype(BF16)
        b_c = bcum[:, TAIL_F + h:TAIL_F + h + 1]
        b_r = bcum_t[TAIL_F + h:TAIL_F + h + 1, :]
        i_c = gates[:, TAIL_I + h:TAIL_I + h + 1]
        i_r = gates_t[TAIL_I + h:TAIL_I + h + 1, :]
        b_last = bcum[chunk - 1:chunk, TAIL_F + h:TAIL_F + h + 1]
        m_prev = m_ref[h:h + 1, 0:1]
        n_prev = n_ref[h:h + 1, :]
        c_prev = c_ref[h]

        dmat = jnp.where(tril, b_c - b_r + i_r, NEG_INF)
        m_inter = b_c + m_prev
        m_row = jnp.maximum(m_inter, jnp.max(dmat, axis=-1, keepdims=True))
        q_b = q.astype(BF16)
        wmat = jnp.exp(dmat - m_row) * _dot_nt(q_b, k.astype(BF16))
        inter = jnp.exp(m_inter - m_row)
        num = _dot(wmat.astype(BF16), v) + inter * _dot(q_b, c_prev.astype(BF16))
        den = jnp.sum(wmat, axis=-1, keepdims=True) + inter * jnp.sum(q * n_prev, axis=-1, keepdims=True)
        hcell = num / jnp.maximum(jnp.abs(den), jnp.exp(-m_row))
        gate_o = _sigmoid(og_ref[:, h * dv:(h + 1) * dv])
        out_ref[:, h * dv:(h + 1) * dv] = (gate_o * hcell).astype(out_ref.dtype)

        g_c = b_last - b_c + i_c
        m_new = jnp.maximum(b_last + m_prev, jnp.max(g_c, axis=0, keepdims=True))
        wk = jnp.exp(g_c - m_new)
        decay = jnp.exp(b_last + m_prev - m_new)
        kw = k * wk
        c_ref[h] = decay * c_prev + _dot_tn(kw.astype(BF16), v)
        n_ref[h:h + 1, :] = decay * n_prev + jnp.sum(kw, axis=0, keepdims=True)
        m_ref[h:h + 1, :] = jnp.broadcast_to(m_new, (1, LANES))


def mlstm(proj, conv_w, conv_b, gate_bias):
    nc = SEQ // MLSTM_CHUNK
    chunk = MLSTM_CHUNK
    wqk = 2 * MLSTM_HEADS * MLSTM_DK
    wv = MLSTM_HEADS * MLSTM_DV
    row = lambda b, c: b * nc + c
    return pl.pallas_call(
        _mlstm_kernel,
        out_shape=jax.ShapeDtypeStruct((TOKENS, wv), BF16),
        grid=(BATCH, nc),
        in_specs=[
            pl.BlockSpec((chunk, wqk), lambda b, c: (row(b, c), EV_MQK // wqk)),
            pl.BlockSpec((chunk, wv), lambda b, c: (row(b, c), EV_MV // wv)),
            pl.BlockSpec((chunk, wv), lambda b, c: (row(b, c), EV_MO // wv)),
            pl.BlockSpec((chunk, LANES), lambda b, c: (row(b, c), EV_TAIL // LANES)),
            pl.BlockSpec((CONV_WIDTH, wqk), lambda b, c: (0, 0)),
            pl.BlockSpec((1, wqk), lambda b, c: (0, 0)),
            pl.BlockSpec((1, LANES), lambda b, c: (0, 0)),
        ],
        out_specs=pl.BlockSpec((chunk, wv), lambda b, c: (row(b, c), 0)),
        scratch_shapes=[
            pltpu.VMEM((chunk + 8, wqk), F32),
            pltpu.VMEM((MLSTM_HEADS, MLSTM_DK, MLSTM_DV), F32),
            pltpu.VMEM((8, MLSTM_DK), F32),
            pltpu.VMEM((8, LANES), F32),
        ],
        compiler_params=_params(("parallel", "arbitrary")),
        name="mlstm",
    )(proj, proj, proj, proj, conv_w, conv_b.reshape(1, wqk), gate_bias)


def _dilated_kernel(*refs, nres, nblk, has_prev, span, heads):
    if has_prev:
        q_ref, kp_ref, kc_ref, vp_ref, vc_ref, o_ref, lse_ref = refs
    else:
        q_ref, kc_ref, vc_ref, o_ref, lse_ref = refs
        kp_ref = vp_ref = None
    n = pl.program_id(2)
    hg = pl.program_id(3)
    blk_sz = DIL_BLOCK
    dh = DIL_HEAD_DIM
    nkeys = 2 * blk_sz if has_prev else blk_sz
    qi = lax.broadcasted_iota(jnp.int32, (blk_sz, nkeys), 0)
    kj = lax.broadcasted_iota(jnp.int32, (blk_sz, nkeys), 1)
    dist = (nkeys - blk_sz) + qi - kj
    bias_full = jnp.where(dist >= 0, jnp.where(dist <= span, 0.0, NEG_INF), NEG_INF)
    bias_first = jnp.where(kj >= nkeys - blk_sz, bias_full, NEG_INF)
    lane = lax.broadcasted_iota(jnp.int32, (blk_sz, LANES), 1)

    @pl.when(hg == 0)
    def _():
        lse_ref[...] = jnp.zeros(lse_ref.shape, F32)

    def stack(ref, res, rows):
        return jnp.stack([ref[res, rows, h * dh:(h + 1) * dh] for h in range(heads)])

    for res in range(nres):
        for blk in range(nblk):
            rows = slice(blk * blk_sz, (blk + 1) * blk_sz)
            q3 = stack(q_ref, res, rows)
            k3 = stack(kc_ref, res, rows)
            v3 = stack(vc_ref, res, rows)
            bias = bias_full
            if has_prev:
                if blk == 0:
                    first = slice(0, blk_sz)
                    kp3, vp3 = stack(kp_ref, res, first), stack(vp_ref, res, first)
                    bias = jnp.where(n > 0, bias_full, bias_first)
                else:
                    prows = slice((blk - 1) * blk_sz, blk * blk_sz)
                    kp3, vp3 = stack(kc_ref, res, prows), stack(vc_ref, res, prows)
                k3 = jnp.concatenate([kp3, k3], axis=1)
                v3 = jnp.concatenate([vp3, v3], axis=1)
            s = jnp.einsum("hqd,hkd->hqk", q3, k3, preferred_element_type=F32) + bias[None]
            m = jnp.max(s, axis=-1, keepdims=True)
            p = jnp.exp2(s - m)
            den = jnp.sum(p, axis=-1, keepdims=True)
            o = jnp.einsum("hqk,hkd->hqd", p.astype(BF16), v3, preferred_element_type=F32) / den
            lse = m * LN2 + jnp.log(den)
            lse_tile = lse_ref[res, rows, :]
            for h in range(heads):
                o_ref[res, rows, h * dh:(h + 1) * dh] = o[h].astype(o_ref.dtype)
                lse_tile = jnp.where(lane == hg * heads + h, lse[h], lse_tile)
            lse_ref[res, rows, :] = lse_tile


def dilated_pattern(qkv_perm, window, dilation, *, hw, units):
    span = window // dilation
    length = SEQ // dilation
    nblk = min(length // DIL_BLOCK, units)
    nres = min(dilation, units // nblk)
    tl = nblk * DIL_BLOCK
    ntile = length // tl
    has_prev = length > DIL_BLOCK
    heads = hw // DIL_HEAD_DIM
    ngroups = ODD_MIX // hw

    def cur(which):
        return pl.BlockSpec((None, nres, tl, hw), lambda b, r, n, g: (b, r, n, which * ngroups + g))

    def prev(which):
        return pl.BlockSpec((None, nres, DIL_BLOCK, hw),
                            lambda b, r, n, g: (b, r, jnp.maximum(n * nblk - 1, 0), which * ngroups + g))

    if has_prev:
        in_specs = [cur(0), prev(1), cur(1), prev(2), cur(2)]
    else:
        in_specs = [cur(0), cur(1), cur(2)]
    return pl.pallas_call(
        functools.partial(_dilated_kernel, nres=nres, nblk=nblk, has_prev=has_prev, span=span, heads=heads),
        out_shape=(jax.ShapeDtypeStruct((BATCH, dilation, length, ODD_MIX), BF16),
                   jax.ShapeDtypeStruct((BATCH, dilation, length, LANES), F32)),
        grid=(BATCH, dilation // nres, ntile, ngroups),
        in_specs=in_specs,
        out_specs=(pl.BlockSpec((None, nres, tl, hw), lambda b, r, n, g: (b, r, n, g)),
                   pl.BlockSpec((None, nres, tl, LANES), lambda b, r, n, g: (b, r, n, 0))),
        compiler_params=_params(("parallel", "parallel", "arbitrary", "arbitrary")),
        name="dilated_d%d" % dilation,
    )(*([qkv_perm] * len(in_specs)))


def _merge_kernel(o1_ref, o4_ref, o16_ref, l1_ref, l4_ref, l16_ref, out_ref,
                  nat_scr, p4_scr, o4_scr, o_scr, *, tm):
    dh = DIL_HEAD_DIM
    q4, q16 = tm // 4, tm // 16

    for r4 in range(4):
        for a in range(4):
            p4_scr[r4, pl.ds(a, q16, stride=4), :] = l16_ref[r4 + 4 * a]
    for r4 in range(4):
        nat_scr[0, pl.ds(r4, q4, stride=4), :] = l4_ref[r4]
        nat_scr[1, pl.ds(r4, q4, stride=4), :] = p4_scr[r4]
    l1, l4, l16 = l1_ref[0], nat_scr[0], nat_scr[1]
    mx = jnp.maximum(jnp.maximum(l1, l4), l16)
    e1, e4, e16 = jnp.exp(l1 - mx), jnp.exp(l4 - mx), jnp.exp(l16 - mx)
    tot = e1 + e4 + e16
    w1 = e1 / tot
    nat_scr[0] = e4 / tot
    nat_scr[1] = e16 / tot
    w4 = jnp.concatenate([nat_scr[0, pl.ds(r4, q4, stride=4), :] for r4 in range(4)], axis=0)
    for r4 in range(4):
        p4_scr[r4] = nat_scr[1, pl.ds(r4, q4, stride=4), :]
    w16_parts = [None] * 16
    for r4 in range(4):
        for a in range(4):
            w16_parts[r4 + 4 * a] = p4_scr[r4, pl.ds(a, q16, stride=4), :]
    w16 = jnp.concatenate(w16_parts, axis=0)

    head_of_col = lax.broadcasted_iota(jnp.int32, (2 * LANES, ODD_MIX), 1) // dh
    lane_of_row = lax.broadcasted_iota(jnp.int32, (2 * LANES, ODD_MIX), 0) % LANES
    expand = jnp.where(head_of_col == lane_of_row, 1.0, 0.0).astype(BF16)

    def two_terms(w):
        hi = w.astype(BF16)
        return jnp.concatenate([hi, (w - hi.astype(F32)).astype(BF16)], axis=1)

    w1, w4, w16 = two_terms(w1), two_terms(w4), two_terms(w16)

    pair = 2
    for hp in range(DIL_HEADS // pair):
        spread = expand[:, hp * pair * dh:(hp + 1) * pair * dh]
        w1p, w4p, w16p = _dot(w1, spread), _dot(w4, spread), _dot(w16, spread)
        for hh in range(pair):
            cols = slice((hp * pair + hh) * dh, (hp * pair + hh + 1) * dh)
            part = slice(hh * dh, (hh + 1) * dh)
            for r4 in range(4):
                for a in range(4):
                    r = r4 + 4 * a
                    o4_scr[r4, pl.ds(a, q16, stride=4), :] = (w16p[r * q16:(r + 1) * q16, part]
                                                             * o16_ref[r, :, cols])
            for r4 in range(4):
                acc4 = o4_scr[r4] + w4p[r4 * q4:(r4 + 1) * q4, part] * o4_ref[r4, :, cols]
                o_scr[pl.ds(r4, q4, stride=4), :] = acc4
            out_ref[:, cols] = (o_scr[...] + w1p[:, part] * o1_ref[0, :, cols]).astype(out_ref.dtype)


def merge_patterns(outs, lses, dilations, *, tm):
    assert tuple(dilations) == (1, 4, 16), "the re-interleave is written as two stride-4 passes"
    tiles = SEQ // tm

    def spec(d, width):
        return pl.BlockSpec((None, d, tm // d, width), lambda i: (i // tiles, 0, i % tiles, 0))

    return pl.pallas_call(
        functools.partial(_merge_kernel, tm=tm),
        out_shape=jax.ShapeDtypeStruct((TOKENS, ODD_MIX), BF16),
        grid=(TOKENS // tm,),
        in_specs=[spec(d, ODD_MIX) for d in dilations] + [spec(d, LANES) for d in dilations],
        out_specs=pl.BlockSpec((tm, ODD_MIX), lambda i: (i, 0)),
        scratch_shapes=[pltpu.VMEM((2, tm, LANES), F32),
                        pltpu.VMEM((4, tm // 4, LANES), F32),
                        pltpu.VMEM((4, tm // 4, DIL_HEAD_DIM), F32),
                        pltpu.VMEM((tm, DIL_HEAD_DIM), F32)],
        compiler_params=_params(("parallel",)),
        name="merge_patterns",
    )(*outs, *lses)


def _rope_tables():
    pos = jnp.arange(SEQ, dtype=F32)[:, None]
    half = MLA_ROPE // 2
    inv = ROPE_THETA ** (-jnp.arange(half, dtype=F32) * 2.0 / MLA_ROPE)
    ang = pos * inv[None, :]
    c, s = jnp.cos(ang), jnp.sin(ang)
    z = jnp.zeros_like(c)
    small = jnp.concatenate([c, c, z, z, -s, z, z, z, z, s, z, z], axis=-1)
    small_q = jnp.concatenate([c, c, z, z, -s, s, z, z], axis=-1)
    half = DIL_HEAD_DIM // 2
    inv = ROPE_THETA ** (-jnp.arange(half, dtype=F32) * 2.0 / DIL_HEAD_DIM)
    ang = pos * inv[None, :]
    c, s = jnp.cos(ang), jnp.sin(ang)
    full = jnp.concatenate([c, c, -s, s], axis=-1)
    return small, small_q, full


def _even_weights(w_in, w_uq, w_ukv, b_i, b_f):
    cuts = np.cumsum((MLA_Q_LORA, MLA_KV_LORA, MLA_ROPE, 2 * MLSTM_HEADS * MLSTM_DK,
                      MLSTM_HEADS * MLSTM_DV, MLSTM_HEADS, MLSTM_HEADS))
    c_q, c_kv, k_r, m_qk, m_v, m_i, m_f, m_o = jnp.split(w_in, cuts.tolist(), axis=1)
    pad = jnp.zeros((D_MODEL, EV_PROJ - EV_TAIL - MLA_ROPE - 2 * MLSTM_HEADS), w_in.dtype)
    w_in_r = jnp.concatenate([c_q, c_kv, m_qk, m_v, m_o, k_r, m_i, m_f, pad], axis=1).astype(BF16)
    uq = w_uq.reshape(MLA_Q_LORA, MLA_HEADS, MLA_QK)
    uq = jnp.concatenate([uq, uq[:, :, MLA_NOPE:]], axis=-1)
    w_uq_r = uq.reshape(MLA_Q_LORA, MLA_HEADS * 2 * LANES).astype(BF16)
    gate_bias = jnp.concatenate([jnp.zeros((TAIL_I,), F32), b_i.astype(F32), b_f.astype(F32),
                                 jnp.zeros((LANES - TAIL_F - MLSTM_HEADS,), F32)]).reshape(1, LANES)
    return w_in_r, w_uq_r, w_ukv.astype(BF16), gate_bias


def _even_layer(x, xn, w_in, q_norm, w_uq, kv_norm, w_ukv, conv_w, conv_b, b_i, b_f, tab_q, tab_k, casts):
    w_in_r, w_uq_r, w_ukv_r, gate_bias = _even_weights(w_in, w_uq, w_ukv, b_i, b_f)
    proj = matmul_weight_stationary(xn, w_in_r, tm=1024, tn=EV_PROJ // 3, out_dtype=F32)
    scale = MLA_QK ** -0.5 * LOG2E
    q, kv, kr = mla_prep(proj, q_norm, kv_norm, w_uq_r, w_ukv_r, tab_q * scale, tab_k, tm=512, scale=scale)
    a_out, cast_weights = mla_attention(q, kv, kr, casts, tq=512, heads=4)
    hm = mlstm(proj, conv_w, conv_b, gate_bias)
    w_out_b = cast_weights[0][0]
    assert a_out.shape[1] == hm.shape[1]
    return matmul_residual(x, [(a_out, w_out_b, 0), (hm, w_out_b, 1)], tm=512), cast_weights


def _odd_layer(x, xn, w_qkv_b, w_out_b, full_tab):
    scale = DIL_HEAD_DIM ** -0.5 * LOG2E
    identity = jnp.concatenate([jnp.ones((SEQ, LANES), F32), jnp.zeros((SEQ, LANES), F32)], axis=-1)
    tab = jnp.stack([full_tab * scale, full_tab, identity])
    dilations = [d for _, d in DIL_PATTERNS]
    qkv_perms = qkv_projection(xn, w_qkv_b, tab, dilations, tm=1024, tn=1024)
    outs, lses = [], []
    for (window, dilation), qkv_perm in zip(DIL_PATTERNS, qkv_perms):
        o_g, lse_g = dilated_pattern(qkv_perm, window, dilation, hw=1024, units=8)
        outs.append(o_g)
        lses.append(lse_g)
    o = merge_patterns(outs, lses, dilations, tm=512)
    return matmul_residual(x, [(o, w_out_b, 0)], tm=512)


def kernel(x, norm_mix, norm_mlp, ev_w_in, mla_q_norm, mla_w_uq, mla_kv_norm, mla_w_ukv,
           mlstm_conv_w, mlstm_conv_b, mlstm_b_i, mlstm_b_f, ev_w_out, od_w_qkv, od_w_out,
           mlp_w1, mlp_w2, norm_final):
    assert x.shape == (BATCH, SEQ, D_MODEL) and x.dtype == F32
    tab_k, tab_q, full_tab = _rope_tables()
    xt = x.reshape(TOKENS, D_MODEL)
    xn = rms_norm_bf16(xt, norm_mix[0], tm=512)
    w1b = w2b = w_qkv_b = w_od_out_b = None
    for layer in range(DEPTH):
        i = layer // 2
        if layer % 2 == 0:
            casts = [(ev_w_out, i, 1)]
            if layer + 1 < DEPTH:
                casts += [(od_w_qkv, i, 1), (od_w_out, i, 1)]
            if layer == 0:
                casts += [(mlp_w1, 0, 1), (mlp_w2, 0, 1)]
            xt, cast_weights = _even_layer(xt, xn, ev_w_in[i], mla_q_norm[i], mla_w_uq[i], mla_kv_norm[i],
                                           mla_w_ukv[i], mlstm_conv_w[i], mlstm_conv_b[i], mlstm_b_i[i],
                                           mlstm_b_f[i], tab_q, tab_k, casts)
            if layer + 1 < DEPTH:
                w_qkv_b, w_od_out_b = cast_weights[1][0], cast_weights[2][0]
            if layer == 0:
                w1b, w2b = cast_weights[-2], cast_weights[-1]
        else:
            xt = _odd_layer(xt, xn, w_qkv_b, w_od_out_b, full_tab)
        if layer == DEPTH - 1:
            (xt,), _ = mlp_block(xt, norm_mlp[layer], w1b, w2b, 0, norm_final, [], tm=512, tf=1024,
                                 tail="final")
        else:
            casts = [(mlp_w1, layer + 1, 1), (mlp_w2, layer + 1, 1)]
            (xt, xn), (w1b, w2b) = mlp_block(xt, norm_mlp[layer], w1b, w2b, 0, norm_mix[layer + 1], casts,
                                             tm=512, tf=1024, tail="next")
    return xt.reshape(BATCH, SEQ, D_MODEL)
```

```python
import functools

import jax
import jax.numpy as jnp
import numpy as np
from jax import lax
from jax.experimental import pallas as pl
from jax.experimental.pallas import tpu as pltpu

D_MODEL = 2048
BATCH = 4
SEQ = 2048
DEPTH = 4
MLA_HEADS = 8
MLA_Q_LORA = 512
MLA_KV_LORA = 512
MLA_NOPE = 128
MLA_ROPE = 64
MLA_V = 128
MLA_QK = MLA_NOPE + MLA_ROPE
MLSTM_HEADS = 4
MLSTM_DK = 128
MLSTM_DV = 256
MLSTM_CHUNK = 128
CONV_WIDTH = 4
DIL_HEADS = 16
DIL_HEAD_DIM = 128
DIL_PATTERNS = ((128, 1), (512, 4), (2048, 16))
DIL_BLOCK = 128
D_FF = 4 * D_MODEL
ROPE_THETA = 10000.0
NORM_EPS = 1e-6
ODD_MIX = DIL_HEADS * DIL_HEAD_DIM
TOKENS = BATCH * SEQ

LANES = 128
VMEM_LIMIT_BYTES = 56 * 1024 * 1024

EV_CQ, EV_CKV, EV_MQK, EV_MV, EV_MO, EV_TAIL = 0, 512, 1024, 2048, 3072, 4096
EV_PROJ = 4224
TAIL_I = MLA_ROPE
TAIL_F = MLA_ROPE + MLSTM_HEADS

F32 = jnp.float32
BF16 = jnp.bfloat16
NEG_INF = float("-inf")
LOG2E = float(np.log2(np.e))
LN2 = float(np.log(2.0))


def _params(semantics):
    return pltpu.CompilerParams(dimension_semantics=semantics, vmem_limit_bytes=VMEM_LIMIT_BYTES)


def _rms_normalize(x, g):
    ms = jnp.mean(x * x, axis=-1, keepdims=True)
    return x * lax.rsqrt(ms + NORM_EPS) * g


def _dot(a, b):
    return jnp.dot(a, b, preferred_element_type=F32)


def _dot_nt(a, b):
    return lax.dot_general(a, b, (((1,), (1,)), ((), ())), preferred_element_type=F32)


def _dot_tn(a, b):
    return lax.dot_general(a, b, (((0,), (0,)), ((), ())), preferred_element_type=F32)


def _sigmoid(x):
    return 1.0 / (1.0 + jnp.exp(-x))


def _log_sigmoid(x):
    return jnp.minimum(x, 0.0) - jnp.log1p(jnp.exp(-jnp.abs(x)))


def _rope_small(y, tab):
    return (y * tab[:, 0:LANES] + pltpu.roll(y, 96, 1) * tab[:, LANES:2 * LANES]
            + pltpu.roll(y, 32, 1) * tab[:, 2 * LANES:3 * LANES])


def _rope_full(y, tab):
    return y * tab[:, 0:LANES] + pltpu.roll(y, 64, 1) * tab[:, LANES:2 * LANES]


def _mla_prep_kernel(cq_ref, ckv_ref, tail_ref, gq_ref, gkv_ref, wq_ref, wkv_ref, tabq_ref, tabk_ref,
                     q_ref, kv_ref, kr_ref, *, scale):
    cq = _rms_normalize(cq_ref[...], gq_ref[...]).astype(BF16)
    ckv = _rms_normalize(ckv_ref[...], gkv_ref[...]).astype(BF16)
    kv_ref[...] = _dot(ckv, wkv_ref[...]).astype(kv_ref.dtype)
    y = _dot(cq, wq_ref[...])
    tab = tabq_ref[...]
    for grp in range(y.shape[1] // LANES):
        yg = y[:, grp * LANES:(grp + 1) * LANES]
        if grp % 2 == 0:
            og = yg * scale
        else:
            og = yg * tab[:, 0:LANES] + pltpu.roll(yg, 32, 1) * tab[:, LANES:2 * LANES]
        q_ref[:, grp * LANES:(grp + 1) * LANES] = og.astype(q_ref.dtype)
    kr_ref[...] = _rope_small(tail_ref[...], tabk_ref[...]).astype(kr_ref.dtype)


def mla_prep(proj, q_norm, kv_norm, w_uq_r, w_ukv, tab_q, tab_k, *, tm, scale):
    pos_tiles = SEQ // tm
    nq, nkv = w_uq_r.shape[1], w_ukv.shape[1]
    return pl.pallas_call(
        functools.partial(_mla_prep_kernel, scale=scale),
        out_shape=(jax.ShapeDtypeStruct((TOKENS, nq), BF16), jax.ShapeDtypeStruct((TOKENS, nkv), BF16),
                   jax.ShapeDtypeStruct((TOKENS, LANES), BF16)),
        grid=(TOKENS // tm,),
        in_specs=[
            pl.BlockSpec((tm, MLA_Q_LORA), lambda i: (i, EV_CQ // MLA_Q_LORA)),
            pl.BlockSpec((tm, MLA_KV_LORA), lambda i: (i, EV_CKV // MLA_KV_LORA)),
            pl.BlockSpec((tm, LANES), lambda i: (i, EV_TAIL // LANES)),
            pl.BlockSpec((1, MLA_Q_LORA), lambda i: (0, 0)),
            pl.BlockSpec((1, MLA_KV_LORA), lambda i: (0, 0)),
            pl.BlockSpec((MLA_Q_LORA, nq), lambda i: (0, 0)),
            pl.BlockSpec((MLA_KV_LORA, nkv), lambda i: (0, 0)),
            pl.BlockSpec((tm, 2 * LANES), lambda i: (i % pos_tiles, 0)),
            pl.BlockSpec((tm, 3 * LANES), lambda i: (i % pos_tiles, 0)),
        ],
        out_specs=(pl.BlockSpec((tm, nq), lambda i: (i, 0)), pl.BlockSpec((tm, nkv), lambda i: (i, 0)),
                   pl.BlockSpec((tm, LANES), lambda i: (i, 0))),
        compiler_params=_params(("parallel",)),
        name="mla_prep",
    )(proj, proj, proj, q_norm.reshape(1, MLA_Q_LORA), kv_norm.reshape(1, MLA_KV_LORA), w_uq_r, w_ukv,
      tab_q, tab_k)


def _qkv_proj_kernel(xn_ref, w_ref, tab_ref, *rest, dilations, tm, sub_cols):
    out_refs, (scr_ref, scr4_ref) = rest[:len(dilations)], rest[len(dilations):]
    out_by_d = dict(zip(dilations, out_refs))
    tab = tab_ref[...]
    for sub in range(w_ref.shape[1] // sub_cols):
        y = _dot(xn_ref[...], w_ref[:, sub * sub_cols:(sub + 1) * sub_cols])
        for part in range(sub_cols // LANES):
            grp = sub * (sub_cols // LANES) + part
            cols = slice(grp * LANES, (grp + 1) * LANES)
            yg = _rope_full(y[:, part * LANES:(part + 1) * LANES], tab)
            out_by_d[1][0, :, cols] = yg.astype(BF16)
            scr_ref[grp] = yg
            for r4 in range(4):
                v = scr_ref[grp, pl.ds(r4, tm // 4, stride=4), :]
                out_by_d[4][r4, :, cols] = v.astype(BF16)
                scr4_ref[r4] = v
            for r4 in range(4):
                for a in range(4):
                    w = scr4_ref[r4, pl.ds(a, tm // 16, stride=4), :]
                    out_by_d[16][r4 + 4 * a, :, cols] = w.astype(BF16)


def qkv_projection(xn, w, tab, dilations, *, tm, tn):
    assert tuple(dilations) == (1, 4, 16), "the de-interleave is written as two stride-4 passes"
    t, k_dim = xn.shape
    n = w.shape[1]
    tiles = SEQ // tm
    q_tiles = ODD_MIX // tn
    return pl.pallas_call(
        functools.partial(_qkv_proj_kernel, dilations=tuple(dilations), tm=tm, sub_cols=2 * LANES),
        out_shape=tuple(jax.ShapeDtypeStruct((BATCH, d, SEQ // d, n), BF16) for d in dilations),
        grid=(n // tn, t // tm),
        in_specs=[
            pl.BlockSpec((tm, k_dim), lambda j, i: (i, 0)),
            pl.BlockSpec((k_dim, tn), lambda j, i: (0, j)),
            pl.BlockSpec((None, tm, 2 * LANES), lambda j, i: (j // q_tiles, i % tiles, 0)),
        ],
        out_specs=tuple(pl.BlockSpec((None, d, tm // d, tn), lambda j, i: (i // tiles, 0, i % tiles, j))
                        for d in dilations),
        scratch_shapes=[pltpu.VMEM((tn // LANES, tm, LANES), F32), pltpu.VMEM((4, tm // 4, LANES), F32)],
        compiler_params=_params(("parallel", "arbitrary")),
        name="qkv_projection",
    )(xn, w, tab)


def _norm_kernel(x_ref, g_ref, o_ref):
    o_ref[...] = _rms_normalize(x_ref[...], g_ref[...]).astype(o_ref.dtype)


def rms_norm_bf16(x, g, *, tm):
    t, d = x.shape
    return pl.pallas_call(
        _norm_kernel,
        out_shape=jax.ShapeDtypeStruct((t, d), BF16),
        grid=(t // tm,),
        in_specs=[pl.BlockSpec((tm, d), lambda i: (i, 0)), pl.BlockSpec((1, d), lambda i: (0, 0))],
        out_specs=pl.BlockSpec((tm, d), lambda i: (i, 0)),
        compiler_params=_params(("parallel",)),
        name="rms_norm_bf16",
    )(x, g.reshape(1, d))


def _mm_kernel(a_ref, w_ref, o_ref):
    o_ref[...] = _dot(a_ref[...], w_ref[...]).astype(o_ref.dtype)


def matmul_weight_stationary(a, w, *, tm, tn, out_dtype):
    t, k = a.shape
    n = w.shape[1]
    return pl.pallas_call(
        _mm_kernel,
        out_shape=jax.ShapeDtypeStruct((t, n), out_dtype),
        grid=(n // tn, t // tm),
        in_specs=[pl.BlockSpec((tm, k), lambda j, i: (i, 0)), pl.BlockSpec((k, tn), lambda j, i: (0, j))],
        out_specs=pl.BlockSpec((tm, tn), lambda j, i: (i, j)),
        compiler_params=_params(("parallel", "arbitrary")),
        name="matmul_weight_stationary",
    )(a, w)


def _mm_res_kernel(*refs):
    r_ref, o_ref = refs[-2], refs[-1]
    acc = r_ref[...]
    for k in range((len(refs) - 2) // 2):
        acc = acc + _dot(refs[2 * k][...], refs[2 * k + 1][...])
    o_ref[...] = acc


def matmul_residual(res, pairs, *, tm):
    t, n = res.shape
    in_specs, args = [], []
    for a, w, row_block in pairs:
        k = a.shape[1]
        in_specs += [pl.BlockSpec((tm, k), lambda i: (i, 0)),
                     pl.BlockSpec((k, n), lambda i, row_block=row_block: (row_block, 0))]
        args += [a, w]
    in_specs.append(pl.BlockSpec((tm, n), lambda i: (i, 0)))
    args.append(res)
    return pl.pallas_call(
        _mm_res_kernel,
        out_shape=jax.ShapeDtypeStruct((t, n), F32),
        grid=(t // tm,),
        in_specs=in_specs,
        out_specs=pl.BlockSpec((tm, n), lambda i: (i, 0)),
        compiler_params=_params(("parallel",)),
        name="matmul_residual",
    )(*args)


def _side_cast_specs(casts, steps, step_of):
    in_specs, out_specs, out_shapes = [], [], []
    for w, first, count in casts:
        _, rows, cols = w.shape
        slabs = steps // count
        assert steps % count == 0 and rows % slabs == 0 and (rows // slabs) % 16 == 0
        block = (None, rows // slabs, cols)
        in_specs.append(pl.BlockSpec(
            block, lambda *idx, first=first, slabs=slabs: (first + step_of(*idx) // slabs,
                                                           step_of(*idx) % slabs, 0)))
        out_specs.append(pl.BlockSpec(
            block, lambda *idx, slabs=slabs: (step_of(*idx) // slabs, step_of(*idx) % slabs, 0)))
        out_shapes.append(jax.ShapeDtypeStruct((count, rows, cols), BF16))
    return in_specs, out_specs, out_shapes


def _mlp_kernel(x_ref, g_ref, w1_ref, w2_ref, gn_ref, *rest, tail, ncast):
    cast_in, o_ref, xn_ref = rest[:ncast], rest[ncast], rest[-1]
    cast_out = rest[len(rest) - 1 - ncast:len(rest) - 1]
    f = pl.program_id(1)

    @pl.when(f == 0)
    def _():
        x = x_ref[...]
        xn_ref[...] = _rms_normalize(x, g_ref[...]).astype(BF16)
        o_ref[...] = x

    h = jnp.maximum(_dot(xn_ref[...], w1_ref[...]), 0.0)
    for src, dst in zip(cast_in, cast_out):
        dst[...] = src[...].astype(dst.dtype)
    o_ref[...] += _dot((h * h).astype(BF16), w2_ref[...])

    @pl.when(f == pl.num_programs(1) - 1)
    def _():
        normed = _rms_normalize(o_ref[...], gn_ref[...])
        if tail == "final":
            o_ref[...] = normed
        else:
            rest[ncast + 1][...] = normed.astype(BF16)


def mlp_block(x, g, w1b, w2b, layer, gn, casts, *, tm, tf, tail):
    t, d = x.shape
    dff = w1b.shape[2]
    nf = dff // tf
    cast_in_specs, cast_out_specs, cast_shapes = _side_cast_specs(casts, (t // tm) * nf, lambda i, f: i * nf + f)
    out_shape = [jax.ShapeDtypeStruct((t, d), F32)]
    out_specs = [pl.BlockSpec((tm, d), lambda i, f: (i, 0))]
    if tail == "next":
        out_shape.append(jax.ShapeDtypeStruct((t, d), BF16))
        out_specs.append(pl.BlockSpec((tm, d), lambda i, f: (i, 0)))
    n_main = len(out_shape)
    outs = pl.pallas_call(
        functools.partial(_mlp_kernel, tail=tail, ncast=len(casts)),
        out_shape=(*out_shape, *cast_shapes),
        grid=(t // tm, nf),
        in_specs=[
            pl.BlockSpec((tm, d), lambda i, f: (i, 0)),
            pl.BlockSpec((1, d), lambda i, f: (0, 0)),
            pl.BlockSpec((None, d, tf), lambda i, f: (layer, 0, f)),
            pl.BlockSpec((None, tf, d), lambda i, f: (layer, f, 0)),
            pl.BlockSpec((1, d), lambda i, f: (0, 0)),
            *cast_in_specs,
        ],
        out_specs=(*out_specs, *cast_out_specs),
        scratch_shapes=[pltpu.VMEM((tm, d), BF16)],
        compiler_params=_params(("parallel", "arbitrary")),
        name="mlp_block",
    )(x, g.reshape(1, d), w1b, w2b, gn.reshape(1, d), *[w for w, _, _ in casts])
    return outs[:n_main], outs[n_main:]


def _mla_attn_kernel(q_ref, kv_ref, kr_ref, *rest, tq, heads, ncast):
    cast_in, o_ref, cast_out = rest[:ncast], rest[ncast], rest[ncast + 1:2 * ncast + 1]
    m_ref, l_ref, acc_ref = rest[2 * ncast + 1:]
    qi = pl.program_id(2)
    hw = 2 * LANES
    m_ref[...] = jnp.full(m_ref.shape, NEG_INF, F32)
    l_ref[...] = jnp.zeros(l_ref.shape, F32)
    acc_ref[...] = jnp.zeros(acc_ref.shape, F32)

    def block(key_start, nkeys, rows, first_visible):
        kr = kr_ref[pl.ds(key_start, nkeys), :]
        for h in range(heads):
            q = q_ref[rows, h * hw:(h + 1) * hw]
            kn = kv_ref[pl.ds(key_start, nkeys), h * hw:h * hw + MLA_NOPE]
            v = kv_ref[pl.ds(key_start, nkeys), h * hw + MLA_NOPE:(h + 1) * hw]
            s = _dot_nt(q, jnp.concatenate([kn, kr], axis=-1))
            if first_visible is not None:
                row = lax.broadcasted_iota(jnp.int32, s.shape, 0)
                col = lax.broadcasted_iota(jnp.int32, s.shape, 1)
                s = jnp.where(col <= row + first_visible, s, NEG_INF)
            m_prev = m_ref[h, rows]
            m_new = jnp.maximum(m_prev, jnp.max(s, axis=-1, keepdims=True))
            p = jnp.exp2(s - jnp.tile(m_new, (1, nkeys // LANES)))
            alpha = jnp.exp2(m_prev - m_new)
            l_ref[h, rows] = alpha * l_ref[h, rows] + jnp.sum(p, axis=-1, keepdims=True)
            acc_ref[h, rows] = alpha * acc_ref[h, rows] + _dot(p.astype(BF16), v)
            m_ref[h, rows] = m_new

    def body(ki, carry):
        block(pl.multiple_of(ki * tq, tq), tq, slice(0, tq), None)
        return carry

    lax.fori_loop(0, qi, body, 0)
    for src, dst in zip(cast_in, cast_out):
        dst[...] = src[...].astype(dst.dtype)
    block(pl.multiple_of(qi * tq, tq), tq, slice(0, tq), 0)
    for h in range(heads):
        o_ref[:, h * MLA_V:(h + 1) * MLA_V] = (acc_ref[h] / l_ref[h]).astype(o_ref.dtype)


def mla_attention(q, kv, kr, casts, *, tq, heads):
    nq = SEQ // tq
    hw = 2 * LANES * heads
    groups = MLA_HEADS // heads
    steps = BATCH * groups * nq

    cast_in_specs, cast_out_specs, cast_shapes = _side_cast_specs(
        casts, steps, lambda b, h, i: (b * groups + h) * nq + i)

    outs = pl.pallas_call(
        functools.partial(_mla_attn_kernel, tq=tq, heads=heads, ncast=len(casts)),
        out_shape=(jax.ShapeDtypeStruct((TOKENS, MLA_HEADS * MLA_V), BF16), *cast_shapes),
        grid=(BATCH, groups, nq),
        in_specs=[
            pl.BlockSpec((tq, hw), lambda b, h, i: (b * nq + i, h)),
            pl.BlockSpec((SEQ, hw), lambda b, h, i: (b, h)),
            pl.BlockSpec((SEQ, LANES), lambda b, h, i: (b, 0)),
            *cast_in_specs,
        ],
        out_specs=(pl.BlockSpec((tq, heads * MLA_V), lambda b, h, i: (b * nq + i, h)), *cast_out_specs),
        scratch_shapes=[pltpu.VMEM((heads, tq, LANES), F32), pltpu.VMEM((heads, tq, LANES), F32),
                        pltpu.VMEM((heads, tq, MLA_V), F32)],
        compiler_params=_params(("parallel", "parallel", "arbitrary")),
        name="mla_attention",
    )(q, kv, kr, *[w for w, _, _ in casts])
    return outs[0], outs[1:]


def _mlstm_kernel(qk_ref, v_ref, og_ref, gate_ref, cw_ref, cb_ref, gb_ref, out_ref,
                  hist_ref, c_ref, n_ref, m_ref):
    chunk = MLSTM_CHUNK
    c = pl.program_id(1)

    @pl.when(c == 0)
    def _():
        hist_ref[0:8, :] = jnp.zeros((8, hist_ref.shape[1]), F32)
        c_ref[...] = jnp.zeros(c_ref.shape, F32)
        n_ref[...] = jnp.zeros(n_ref.shape, F32)
        m_ref[...] = jnp.zeros(m_ref.shape, F32)

    hist_ref[8:8 + chunk, :] = qk_ref[...]
    y = jnp.broadcast_to(cb_ref[...], (chunk, hist_ref.shape[1]))
    for j in range(CONV_WIDTH):
        off = 8 - (CONV_WIDTH - 1) + j
        y = y + cw_ref[j:j + 1, :] * hist_ref[off:off + chunk, :]
    qk = y * _sigmoid(y)
    hist_ref[0:8, :] = hist_ref[chunk:chunk + 8, :]

    gates = gate_ref[...] + gb_ref[...]
    logf = _log_sigmoid(gates)
    row = lax.broadcasted_iota(jnp.int32, (chunk, chunk), 0)
    col = lax.broadcasted_iota(jnp.int32, (chunk, chunk), 1)
    tril = col <= row
    bcum = jnp.dot(tril.astype(F32), logf, preferred_element_type=F32,
                   precision=lax.Precision.HIGHEST)
    gates_t = gates.T
    bcum_t = bcum.T

    dk, dv = MLSTM_DK, MLSTM_DV
    for h in range(MLSTM_HEADS):
        q = qk[:, h * dk:(h + 1) * dk]
        k = qk[:, (MLSTM_HEADS + h) * dk:(MLSTM_HEADS + h + 1) * dk] * (dk ** -0.5)
        v = v_ref[:, h * dv:(h + 1) * dv].astype(BF16)
        b_c = bcum[:, TAIL_F + h:TAIL_F + h + 1]
        b_r = bcum_t[TAIL_F + h:TAIL_F + h + 1, :]
        i_c = gates[:, TAIL_I + h:TAIL_I + h + 1]
        i_r = gates_t[TAIL_I + h:TAIL_I + h + 1, :]
        b_last = bcum[chunk - 1:chunk, TAIL_F + h:TAIL_F + h + 1]
        m_prev = m_ref[h:h + 1, 0:1]
        n_prev = n_ref[h:h + 1, :]
        c_prev = c_ref[h]

        dmat = jnp.where(tril, b_c - b_r + i_r, NEG_INF)
        m_inter = b_c + m_prev
        m_row = jnp.maximum(m_inter, jnp.max(dmat, axis=-1, keepdims=True))
        q_b = q.astype(BF16)
        wmat = jnp.exp(dmat - m_row) * _dot_nt(q_b, k.astype(BF16))
        inter = jnp.exp(m_inter - m_row)
        num = _dot(wmat.astype(BF16), v) + inter * _dot(q_b, c_prev.astype(BF16))
        den = jnp.sum(wmat, axis=-1, keepdims=True) + inter * jnp.sum(q * n_prev, axis=-1, keepdims=True)
        hcell = num / jnp.maximum(jnp.abs(den), jnp.exp(-m_row))
        gate_o = _sigmoid(og_ref[:, h * dv:(h + 1) * dv])
        out_ref[:, h * dv:(h + 1) * dv] = (gate_o * hcell).astype(out_ref.dtype)

        g_c = b_last - b_c + i_c
        m_new = jnp.maximum(b_last + m_prev, jnp.max(g_c, axis=0, keepdims=True))
        wk = jnp.exp(g_c - m_new)
        decay = jnp.exp(b_last + m_prev - m_new)
        kw = k * wk
        c_ref[h] = decay * c_prev + _dot_tn(kw.astype(BF16), v)
        n_ref[h:h + 1, :] = decay * n_prev + jnp.sum(kw, axis=0, keepdims=True)
        m_ref[h:h + 1, :] = jnp.broadcast_to(m_new, (1, LANES))


def mlstm(proj, conv_w, conv_b, gate_bias):
    nc = SEQ // MLSTM_CHUNK
    chunk = MLSTM_CHUNK
    wqk = 2 * MLSTM_HEADS * MLSTM_DK
    wv = MLSTM_HEADS * MLSTM_DV
    row = lambda b, c: b * nc + c
    return pl.pallas_call(
        _mlstm_kernel,
        out_shape=jax.ShapeDtypeStruct((TOKENS, wv), BF16),
        grid=(BATCH, nc),
        in_specs=[
            pl.BlockSpec((chunk, wqk), lambda b, c: (row(b, c), EV_MQK // wqk)),
            pl.BlockSpec((chunk, wv), lambda b, c: (row(b, c), EV_MV // wv)),
            pl.BlockSpec((chunk, wv), lambda b, c: (row(b, c), EV_MO // wv)),
            pl.BlockSpec((chunk, LANES), lambda b, c: (row(b, c), EV_TAIL // LANES)),
            pl.BlockSpec((CONV_WIDTH, wqk), lambda b, c: (0, 0)),
            pl.BlockSpec((1, wqk), lambda b, c: (0, 0)),
            pl.BlockSpec((1, LANES), lambda b, c: (0, 0)),
        ],
        out_specs=pl.BlockSpec((chunk, wv), lambda b, c: (row(b, c), 0)),
        scratch_shapes=[
            pltpu.VMEM((chunk + 8, wqk), F32),
            pltpu.VMEM((MLSTM_HEADS, MLSTM_DK, MLSTM_DV), F32),
            pltpu.VMEM((8, MLSTM_DK), F32),
            pltpu.VMEM((8, LANES), F32),
        ],
        compiler_params=_params(("parallel", "arbitrary")),
        name="mlstm",
    )(proj, proj, proj, proj, conv_w, conv_b.reshape(1, wqk), gate_bias)


def _dilated_kernel(*refs, nres, nblk, has_prev, span, heads):
    if has_prev:
        q_ref, kp_ref, kc_ref, vp_ref, vc_ref, o_ref, lse_ref = refs
    else:
        q_ref, kc_ref, vc_ref, o_ref, lse_ref = refs
        kp_ref = vp_ref = None
    n = pl.program_id(2)
    hg = pl.program_id(3)
    blk_sz = DIL_BLOCK
    dh = DIL_HEAD_DIM
    nkeys = 2 * blk_sz if has_prev else blk_sz
    qi = lax.broadcasted_iota(jnp.int32, (blk_sz, nkeys), 0)
    kj = lax.broadcasted_iota(jnp.int32, (blk_sz, nkeys), 1)
    dist = (nkeys - blk_sz) + qi - kj
    bias_full = jnp.where(dist >= 0, jnp.where(dist <= span, 0.0, NEG_INF), NEG_INF)
    bias_first = jnp.where(kj >= nkeys - blk_sz, bias_full, NEG_INF)
    lane = lax.broadcasted_iota(jnp.int32, (blk_sz, LANES), 1)

    @pl.when(hg == 0)
    def _():
        lse_ref[...] = jnp.zeros(lse_ref.shape, F32)

    def stack(ref, res, rows):
        return jnp.stack([ref[res, rows, h * dh:(h + 1) * dh] for h in range(heads)])

    for res in range(nres):
        for blk in range(nblk):
            rows = slice(blk * blk_sz, (blk + 1) * blk_sz)
            q3 = stack(q_ref, res, rows)
            k3 = stack(kc_ref, res, rows)
            v3 = stack(vc_ref, res, rows)
            bias = bias_full
            if has_prev:
                if blk == 0:
                    first = slice(0, blk_sz)
                    kp3, vp3 = stack(kp_ref, res, first), stack(vp_ref, res, first)
                    bias = jnp.where(n > 0, bias_full, bias_first)
                else:
                    prows = slice((blk - 1) * blk_sz, blk * blk_sz)
                    kp3, vp3 = stack(kc_ref, res, prows), stack(vc_ref, res, prows)
                k3 = jnp.concatenate([kp3, k3], axis=1)
                v3 = jnp.concatenate([vp3, v3], axis=1)
            s = jnp.einsum("hqd,hkd->hqk", q3, k3, preferred_element_type=F32) + bias[None]
            m = jnp.max(s, axis=-1, keepdims=True)
            p = jnp.exp2(s - m)
            den = jnp.sum(p, axis=-1, keepdims=True)
            o = jnp.einsum("hqk,hkd->hqd", p.astype(BF16), v3, preferred_element_type=F32) / den
            lse = m * LN2 + jnp.log(den)
            lse_tile = lse_ref[res, rows, :]
            for h in range(heads):
                o_ref[res, rows, h * dh:(h + 1) * dh] = o[h].astype(o_ref.dtype)
                lse_tile = jnp.where(lane == hg * heads + h, lse[h], lse_tile)
            lse_ref[res, rows, :] = lse_tile


def dilated_pattern(qkv_perm, window, dilation, *, hw, units):
    span = window // dilation
    length = SEQ // dilation
    nblk = min(length // DIL_BLOCK, units)
    nres = min(dilation, units // nblk)
    tl = nblk * DIL_BLOCK
    ntile = length // tl
    has_prev = length > DIL_BLOCK
    heads = hw // DIL_HEAD_DIM
    ngroups = ODD_MIX // hw

    def cur(which):
        return pl.BlockSpec((None, nres, tl, hw), lambda b, r, n, g: (b, r, n, which * ngroups + g))

    def prev(which):
        return pl.BlockSpec((None, nres, DIL_BLOCK, hw),
                            lambda b, r, n, g: (b, r, jnp.maximum(n * nblk - 1, 0), which * ngroups + g))

    if has_prev:
        in_specs = [cur(0), prev(1), cur(1), prev(2), cur(2)]
    else:
        in_specs = [cur(0), cur(1), cur(2)]
    return pl.pallas_call(
        functools.partial(_dilated_kernel, nres=nres, nblk=nblk, has_prev=has_prev, span=span, heads=heads),
        out_shape=(jax.ShapeDtypeStruct((BATCH, dilation, length, ODD_MIX), BF16),
                   jax.ShapeDtypeStruct((BATCH, dilation, length, LANES), F32)),
        grid=(BATCH, dilation // nres, ntile, ngroups),
        in_specs=in_specs,
        out_specs=(pl.BlockSpec((None, nres, tl, hw), lambda b, r, n, g: (b, r, n, g)),
                   pl.BlockSpec((None, nres, tl, LANES), lambda b, r, n, g: (b, r, n, 0))),
        compiler_params=_params(("parallel", "parallel", "arbitrary", "arbitrary")),
        name="dilated_d%d" % dilation,
    )(*([qkv_perm] * len(in_specs)))


def _merge_kernel(o1_ref, o4_ref, o16_ref, l1_ref, l4_ref, l16_ref, out_ref,
                  nat_scr, p4_scr, o4_scr, o_scr, *, tm):
    dh = DIL_HEAD_DIM
    q4, q16 = tm // 4, tm // 16

    for r4 in range(4):
        for a in range(4):
            p4_scr[r4, pl.ds(a, q16, stride=4), :] = l16_ref[r4 + 4 * a]
    for r4 in range(4):
        nat_scr[0, pl.ds(r4, q4, stride=4), :] = l4_ref[r4]
        nat_scr[1, pl.ds(r4, q4, stride=4), :] = p4_scr[r4]
    l1, l4, l16 = l1_ref[0], nat_scr[0], nat_scr[1]
    mx = jnp.maximum(jnp.maximum(l1, l4), l16)
    e1, e4, e16 = jnp.exp(l1 - mx), jnp.exp(l4 - mx), jnp.exp(l16 - mx)
    tot = e1 + e4 + e16
    w1 = e1 / tot
    nat_scr[0] = e4 / tot
    nat_scr[1] = e16 / tot
    w4 = jnp.concatenate([nat_scr[0, pl.ds(r4, q4, stride=4), :] for r4 in range(4)], axis=0)
    for r4 in range(4):
        p4_scr[r4] = nat_scr[1, pl.ds(r4, q4, stride=4), :]
    w16_parts = [None] * 16
    for r4 in range(4):
        for a in range(4):
            w16_parts[r4 + 4 * a] = p4_scr[r4, pl.ds(a, q16, stride=4), :]
    w16 = jnp.concatenate(w16_parts, axis=0)

    head_of_col = lax.broadcasted_iota(jnp.int32, (2 * LANES, ODD_MIX), 1) // dh
    lane_of_row = lax.broadcasted_iota(jnp.int32, (2 * LANES, ODD_MIX), 0) % LANES
    expand = jnp.where(head_of_col == lane_of_row, 1.0, 0.0).astype(BF16)

    def two_terms(w):
        hi = w.astype(BF16)
        return jnp.concatenate([hi, (w - hi.astype(F32)).astype(BF16)], axis=1)

    w1, w4, w16 = two_terms(w1), two_terms(w4), two_terms(w16)

    pair = 2
    for hp in range(DIL_HEADS // pair):
        spread = expand[:, hp * pair * dh:(hp + 1) * pair * dh]
        w1p, w4p, w16p = _dot(w1, spread), _dot(w4, spread), _dot(w16, spread)
        for hh in range(pair):
            cols = slice((hp * pair + hh) * dh, (hp * pair + hh + 1) * dh)
            part = slice(hh * dh, (hh + 1) * dh)
            for r4 in range(4):
                for a in range(4):
                    r = r4 + 4 * a
                    o4_scr[r4, pl.ds(a, q16, stride=4), :] = (w16p[r * q16:(r + 1) * q16, part]
                                                             * o16_ref[r, :, cols])
            for r4 in range(4):
                acc4 = o4_scr[r4] + w4p[r4 * q4:(r4 + 1) * q4, part] * o4_ref[r4, :, cols]
                o_scr[pl.ds(r4, q4, stride=4), :] = acc4
            out_ref[:, cols] = (o_scr[...] + w1p[:, part] * o1_ref[0, :, cols]).astype(out_ref.dtype)


def merge_patterns(outs, lses, dilations, *, tm):
    assert tuple(dilations) == (1, 4, 16), "the re-interleave is written as two stride-4 passes"
    tiles = SEQ // tm

    def spec(d, width):
        return pl.BlockSpec((None, d, tm // d, width), lambda i: (i // tiles, 0, i % tiles, 0))

    return pl.pallas_call(
        functools.partial(_merge_kernel, tm=tm),
        out_shape=jax.ShapeDtypeStruct((TOKENS, ODD_MIX), BF16),
        grid=(TOKENS // tm,),
        in_specs=[spec(d, ODD_MIX) for d in dilations] + [spec(d, LANES) for d in dilations],
        out_specs=pl.BlockSpec((tm, ODD_MIX), lambda i: (i, 0)),
        scratch_shapes=[pltpu.VMEM((2, tm, LANES), F32),
                        pltpu.VMEM((4, tm // 4, LANES), F32),
                        pltpu.VMEM((4, tm // 4, DIL_HEAD_DIM), F32),
                        pltpu.VMEM((tm, DIL_HEAD_DIM), F32)],
        compiler_params=_params(("parallel",)),
        name="merge_patterns",
    )(*outs, *lses)


def _rope_tables():
    pos = jnp.arange(SEQ, dtype=F32)[:, None]
    half = MLA_ROPE // 2
    inv = ROPE_THETA ** (-jnp.arange(half, dtype=F32) * 2.0 / MLA_ROPE)
    ang = pos * inv[None, :]
    c, s = jnp.cos(ang), jnp.sin(ang)
    z = jnp.zeros_like(c)
    small = jnp.concatenate([c, c, z, z, -s, z, z, z, z, s, z, z], axis=-1)
    small_q = jnp.concatenate([c, c, z, z, -s, s, z, z], axis=-1)
    half = DIL_HEAD_DIM // 2
    inv = ROPE_THETA ** (-jnp.arange(half, dtype=F32) * 2.0 / DIL_HEAD_DIM)
    ang = pos * inv[None, :]
    c, s = jnp.cos(ang), jnp.sin(ang)
    full = jnp.concatenate([c, c, -s, s], axis=-1)
    return small, small_q, full


def _w_in_relayout_kernel(w_ref, o_ref):
    sizes = (MLA_Q_LORA, MLA_KV_LORA, MLA_ROPE, 2 * MLSTM_HEADS * MLSTM_DK, MLSTM_HEADS * MLSTM_DV,
             MLSTM_HEADS, MLSTM_HEADS, MLSTM_HEADS * MLSTM_DV)
    ends = np.cumsum(sizes)
    c_q, c_kv, k_r, m_qk, m_v, m_i, m_f, m_o = [slice(int(e - n), int(e)) for e, n in zip(ends, sizes)]
    w = w_ref[...]
    pad = jnp.zeros((w.shape[0], o_ref.shape[1] - int(ends[-1])), w.dtype)
    pieces = [w[:, c] for c in (c_q, c_kv, m_qk, m_v, m_o, k_r, m_i, m_f)] + [pad]
    o_ref[...] = jnp.concatenate(pieces, axis=1).astype(o_ref.dtype)


def even_in_weights(ev_w_in, index, *, rows):
    _, d, n = ev_w_in.shape
    return pl.pallas_call(
        _w_in_relayout_kernel,
        out_shape=jax.ShapeDtypeStruct((d, EV_PROJ), BF16),
        grid=(d // rows,),
        in_specs=[pl.BlockSpec((None, rows, n), lambda r: (index, r, 0))],
        out_specs=pl.BlockSpec((rows, EV_PROJ), lambda r: (r, 0)),
        compiler_params=_params(("parallel",)),
        name="even_in_weights",
    )(ev_w_in)


def _even_weights(w_uq, w_ukv, b_i, b_f):
    uq = w_uq.reshape(MLA_Q_LORA, MLA_HEADS, MLA_QK)
    uq = jnp.concatenate([uq, uq[:, :, MLA_NOPE:]], axis=-1)
    w_uq_r = uq.reshape(MLA_Q_LORA, MLA_HEADS * 2 * LANES).astype(BF16)
    gate_bias = jnp.concatenate([jnp.zeros((TAIL_I,), F32), b_i.astype(F32), b_f.astype(F32),
                                 jnp.zeros((LANES - TAIL_F - MLSTM_HEADS,), F32)]).reshape(1, LANES)
    return w_uq_r, w_ukv.astype(BF16), gate_bias


def _even_layer(x, xn, w_in_r, q_norm, w_uq, kv_norm, w_ukv, conv_w, conv_b, b_i, b_f, tab_q, tab_k, casts):
    w_uq_r, w_ukv_r, gate_bias = _even_weights(w_uq, w_ukv, b_i, b_f)
    proj = matmul_weight_stationary(xn, w_in_r, tm=1024, tn=EV_PROJ // 3, out_dtype=F32)
    scale = MLA_QK ** -0.5 * LOG2E
    q, kv, kr = mla_prep(proj, q_norm, kv_norm, w_uq_r, w_ukv_r, tab_q * scale, tab_k, tm=512, scale=scale)
    a_out, cast_weights = mla_attention(q, kv, kr, casts, tq=512, heads=4)
    hm = mlstm(proj, conv_w, conv_b, gate_bias)
    w_out_b = cast_weights[0][0]
    assert a_out.shape[1] == hm.shape[1]
    return matmul_residual(x, [(a_out, w_out_b, 0), (hm, w_out_b, 1)], tm=512), cast_weights


def _odd_layer(x, xn, w_qkv_b, w_out_b, full_tab):
    scale = DIL_HEAD_DIM ** -0.5 * LOG2E
    identity = jnp.concatenate([jnp.ones((SEQ, LANES), F32), jnp.zeros((SEQ, LANES), F32)], axis=-1)
    tab = jnp.stack([full_tab * scale, full_tab, identity])
    dilations = [d for _, d in DIL_PATTERNS]
    qkv_perms = qkv_projection(xn, w_qkv_b, tab, dilations, tm=1024, tn=1024)
    outs, lses = [], []
    for (window, dilation), qkv_perm in zip(DIL_PATTERNS, qkv_perms):
        o_g, lse_g = dilated_pattern(qkv_perm, window, dilation, hw=1024, units=8)
        outs.append(o_g)
        lses.append(lse_g)
    o = merge_patterns(outs, lses, dilations, tm=512)
    return matmul_residual(x, [(o, w_out_b, 0)], tm=512)


def kernel(x, norm_mix, norm_mlp, ev_w_in, mla_q_norm, mla_w_uq, mla_kv_norm, mla_w_ukv,
           mlstm_conv_w, mlstm_conv_b, mlstm_b_i, mlstm_b_f, ev_w_out, od_w_qkv, od_w_out,
           mlp_w1, mlp_w2, norm_final):
    assert x.shape == (BATCH, SEQ, D_MODEL) and x.dtype == F32
    tab_k, tab_q, full_tab = _rope_tables()
    xt = x.reshape(TOKENS, D_MODEL)
    xn = rms_norm_bf16(xt, norm_mix[0], tm=512)
    w1b = w2b = w_qkv_b = w_od_out_b = None
    for layer in range(DEPTH):
        i = layer // 2
        if layer % 2 == 0:
            casts = [(ev_w_out, i, 1)]
            if layer + 1 < DEPTH:
                casts += [(od_w_qkv, i, 1), (od_w_out, i, 1)]
            if layer == 0:
                casts += [(mlp_w1, 0, 1), (mlp_w2, 0, 1)]
            w_in_r = even_in_weights(ev_w_in, i, rows=256)
            xt, cast_weights = _even_layer(xt, xn, w_in_r, mla_q_norm[i], mla_w_uq[i], mla_kv_norm[i],
                                           mla_w_ukv[i], mlstm_conv_w[i], mlstm_conv_b[i], mlstm_b_i[i],
                                           mlstm_b_f[i], tab_q, tab_k, casts)
            if layer + 1 < DEPTH:
                w_qkv_b, w_od_out_b = cast_weights[1][0], cast_weights[2][0]
            if layer == 0:
                w1b, w2b = cast_weights[-2], cast_weights[-1]
        else:
            xt = _odd_layer(xt, xn, w_qkv_b, w_od_out_b, full_tab)
        if layer == DEPTH - 1:
            (xt,), _ = mlp_block(xt, norm_mlp[layer], w1b, w2b, 0, norm_final, [], tm=512, tf=1024,
                                 tail="final")
        else:
            casts = [(mlp_w1, layer + 1, 1), (mlp_w2, layer + 1, 1)]
            (xt, xn), (w1b, w2b) = mlp_block(xt, norm_mlp[layer], w1b, w2b, 0, norm_mix[layer + 1], casts,
                                             tm=512, tf=1024, tail="next")
    return xt.reshape(BATCH, SEQ, D_MODEL)
```

```python
import functools

import jax
import jax.numpy as jnp
import numpy as np
from jax import lax
from jax.experimental import pallas as pl
from jax.experimental.pallas import tpu as pltpu

D_MODEL = 2048
BATCH = 4
SEQ = 2048
DEPTH = 4
MLA_HEADS = 8
MLA_Q_LORA = 512
MLA_KV_LORA = 512
MLA_NOPE = 128
MLA_ROPE = 64
MLA_V = 128
MLA_QK = MLA_NOPE + MLA_ROPE
MLSTM_HEADS = 4
MLSTM_DK = 128
MLSTM_DV = 256
MLSTM_CHUNK = 128
CONV_WIDTH = 4
DIL_HEADS = 16
DIL_HEAD_DIM = 128
DIL_PATTERNS = ((128, 1), (512, 4), (2048, 16))
DIL_BLOCK = 128
D_FF = 4 * D_MODEL
ROPE_THETA = 10000.0
NORM_EPS = 1e-6
ODD_MIX = DIL_HEADS * DIL_HEAD_DIM
TOKENS = BATCH * SEQ

LANES = 128
VMEM_LIMIT_BYTES = 56 * 1024 * 1024

EV_CQ, EV_CKV, EV_MQK, EV_MV, EV_MO, EV_TAIL = 0, 512, 1024, 2048, 3072, 4096
EV_PROJ = 4224
TAIL_I = MLA_ROPE
TAIL_F = MLA_ROPE + MLSTM_HEADS

F32 = jnp.float32
BF16 = jnp.bfloat16
NEG_INF = float("-inf")
LOG2E = float(np.log2(np.e))
LN2 = float(np.log(2.0))


def _params(semantics):
    return pltpu.CompilerParams(dimension_semantics=semantics, vmem_limit_bytes=VMEM_LIMIT_BYTES)


def _rms_normalize(x, g):
    ms = jnp.mean(x * x, axis=-1, keepdims=True)
    return x * lax.rsqrt(ms + NORM_EPS) * g


def _dot(a, b):
    return jnp.dot(a, b, preferred_element_type=F32)


def _dot_nt(a, b):
    return lax.dot_general(a, b, (((1,), (1,)), ((), ())), preferred_element_type=F32)


def _dot_tn(a, b):
    return lax.dot_general(a, b, (((0,), (0,)), ((), ())), preferred_element_type=F32)


def _sigmoid(x):
    return 1.0 / (1.0 + jnp.exp(-x))


def _log_sigmoid(x):
    return jnp.minimum(x, 0.0) - jnp.log1p(jnp.exp(-jnp.abs(x)))


def _rope_small(y, tab):
    return (y * tab[:, 0:LANES] + pltpu.roll(y, 96, 1) * tab[:, LANES:2 * LANES]
            + pltpu.roll(y, 32, 1) * tab[:, 2 * LANES:3 * LANES])


def _rope_full(y, tab):
    return y * tab[:, 0:LANES] + pltpu.roll(y, 64, 1) * tab[:, LANES:2 * LANES]


def _mla_prep_kernel(cq_ref, ckv_ref, tail_ref, gq_ref, gkv_ref, wq_ref, wkv_ref, tabq_ref, tabk_ref,
                     q_ref, kv_ref, kr_ref, *, scale):
    cq = _rms_normalize(cq_ref[...], gq_ref[...]).astype(BF16)
    ckv = _rms_normalize(ckv_ref[...], gkv_ref[...]).astype(BF16)
    kv_ref[...] = _dot(ckv, wkv_ref[...]).astype(kv_ref.dtype)
    y = _dot(cq, wq_ref[...])
    tab = tabq_ref[...]
    for grp in range(y.shape[1] // LANES):
        yg = y[:, grp * LANES:(grp + 1) * LANES]
        if grp % 2 == 0:
            og = yg * scale
        else:
            og = yg * tab[:, 0:LANES] + pltpu.roll(yg, 32, 1) * tab[:, LANES:2 * LANES]
        q_ref[:, grp * LANES:(grp + 1) * LANES] = og.astype(q_ref.dtype)
    kr_ref[...] = _rope_small(tail_ref[...], tabk_ref[...]).astype(kr_ref.dtype)


def mla_prep(proj, q_norm, kv_norm, w_uq_r, w_ukv, tab_q, tab_k, *, tm, scale):
    pos_tiles = SEQ // tm
    nq, nkv = w_uq_r.shape[1], w_ukv.shape[1]
    return pl.pallas_call(
        functools.partial(_mla_prep_kernel, scale=scale),
        out_shape=(jax.ShapeDtypeStruct((TOKENS, nq), BF16), jax.ShapeDtypeStruct((TOKENS, nkv), BF16),
                   jax.ShapeDtypeStruct((TOKENS, LANES), BF16)),
        grid=(TOKENS // tm,),
        in_specs=[
            pl.BlockSpec((tm, MLA_Q_LORA), lambda i: (i, EV_CQ // MLA_Q_LORA)),
            pl.BlockSpec((tm, MLA_KV_LORA), lambda i: (i, EV_CKV // MLA_KV_LORA)),
            pl.BlockSpec((tm, LANES), lambda i: (i, EV_TAIL // LANES)),
            pl.BlockSpec((1, MLA_Q_LORA), lambda i: (0, 0)),
            pl.BlockSpec((1, MLA_KV_LORA), lambda i: (0, 0)),
            pl.BlockSpec((MLA_Q_LORA, nq), lambda i: (0, 0)),
            pl.BlockSpec((MLA_KV_LORA, nkv), lambda i: (0, 0)),
            pl.BlockSpec((tm, 2 * LANES), lambda i: (i % pos_tiles, 0)),
            pl.BlockSpec((tm, 3 * LANES), lambda i: (i % pos_tiles, 0)),
        ],
        out_specs=(pl.BlockSpec((tm, nq), lambda i: (i, 0)), pl.BlockSpec((tm, nkv), lambda i: (i, 0)),
                   pl.BlockSpec((tm, LANES), lambda i: (i, 0))),
        compiler_params=_params(("parallel",)),
        name="mla_prep",
    )(proj, proj, proj, q_norm.reshape(1, MLA_Q_LORA), kv_norm.reshape(1, MLA_KV_LORA), w_uq_r, w_ukv,
      tab_q, tab_k)


def _qkv_proj_kernel(xn_ref, w_ref, tab_ref, *rest, dilations, tm, sub_cols):
    out_refs, (scr_ref, scr4_ref) = rest[:len(dilations)], rest[len(dilations):]
    out_by_d = dict(zip(dilations, out_refs))
    tab = tab_ref[...]
    for sub in range(w_ref.shape[1] // sub_cols):
        y = _dot(xn_ref[...], w_ref[:, sub * sub_cols:(sub + 1) * sub_cols])
        for part in range(sub_cols // LANES):
            grp = sub * (sub_cols // LANES) + part
            cols = slice(grp * LANES, (grp + 1) * LANES)
            yg = _rope_full(y[:, part * LANES:(part + 1) * LANES], tab)
            out_by_d[1][0, :, cols] = yg.astype(BF16)
            scr_ref[grp] = yg
            for r4 in range(4):
                v = scr_ref[grp, pl.ds(r4, tm // 4, stride=4), :]
                out_by_d[4][r4, :, cols] = v.astype(BF16)
                scr4_ref[r4] = v
            for r4 in range(4):
                for a in range(4):
                    w = scr4_ref[r4, pl.ds(a, tm // 16, stride=4), :]
                    out_by_d[16][r4 + 4 * a, :, cols] = w.astype(BF16)


def qkv_projection(xn, w, tab, dilations, *, tm, tn):
    assert tuple(dilations) == (1, 4, 16), "the de-interleave is written as two stride-4 passes"
    t, k_dim = xn.shape
    n = w.shape[1]
    tiles = SEQ // tm
    q_tiles = ODD_MIX // tn
    return pl.pallas_call(
        functools.partial(_qkv_proj_kernel, dilations=tuple(dilations), tm=tm, sub_cols=2 * LANES),
        out_shape=tuple(jax.ShapeDtypeStruct((BATCH, d, SEQ // d, n), BF16) for d in dilations),
        grid=(n // tn, t // tm),
        in_specs=[
            pl.BlockSpec((tm, k_dim), lambda j, i: (i, 0)),
            pl.BlockSpec((k_dim, tn), lambda j, i: (0, j)),
            pl.BlockSpec((None, tm, 2 * LANES), lambda j, i: (j // q_tiles, i % tiles, 0)),
        ],
        out_specs=tuple(pl.BlockSpec((None, d, tm // d, tn), lambda j, i: (i // tiles, 0, i % tiles, j))
                        for d in dilations),
        scratch_shapes=[pltpu.VMEM((tn // LANES, tm, LANES), F32), pltpu.VMEM((4, tm // 4, LANES), F32)],
        compiler_params=_params(("parallel", "arbitrary")),
        name="qkv_projection",
    )(xn, w, tab)


def _norm_kernel(x_ref, g_ref, o_ref):
    o_ref[...] = _rms_normalize(x_ref[...], g_ref[...]).astype(o_ref.dtype)


def rms_norm_bf16(x, g, *, tm):
    t, d = x.shape
    return pl.pallas_call(
        _norm_kernel,
        out_shape=jax.ShapeDtypeStruct((t, d), BF16),
        grid=(t // tm,),
        in_specs=[pl.BlockSpec((tm, d), lambda i: (i, 0)), pl.BlockSpec((1, d), lambda i: (0, 0))],
        out_specs=pl.BlockSpec((tm, d), lambda i: (i, 0)),
        compiler_params=_params(("parallel",)),
        name="rms_norm_bf16",
    )(x, g.reshape(1, d))


def _mm_kernel(a_ref, w_ref, o_ref):
    o_ref[...] = _dot(a_ref[...], w_ref[...]).astype(o_ref.dtype)


def matmul_weight_stationary(a, w, *, tm, tn, out_dtype):
    t, k = a.shape
    n = w.shape[1]
    return pl.pallas_call(
        _mm_kernel,
        out_shape=jax.ShapeDtypeStruct((t, n), out_dtype),
        grid=(n // tn, t // tm),
        in_specs=[pl.BlockSpec((tm, k), lambda j, i: (i, 0)), pl.BlockSpec((k, tn), lambda j, i: (0, j))],
        out_specs=pl.BlockSpec((tm, tn), lambda j, i: (i, j)),
        compiler_params=_params(("parallel", "arbitrary")),
        name="matmul_weight_stationary",
    )(a, w)


def _mm_res_kernel(*refs):
    r_ref, o_ref = refs[-2], refs[-1]
    acc = r_ref[...]
    for k in range((len(refs) - 2) // 2):
        acc = acc + _dot(refs[2 * k][...], refs[2 * k + 1][...])
    o_ref[...] = acc


def matmul_residual(res, pairs, *, tm):
    t, n = res.shape
    in_specs, args = [], []
    for a, w, row_block in pairs:
        k = a.shape[1]
        in_specs += [pl.BlockSpec((tm, k), lambda i: (i, 0)),
                     pl.BlockSpec((k, n), lambda i, row_block=row_block: (row_block, 0))]
        args += [a, w]
    in_specs.append(pl.BlockSpec((tm, n), lambda i: (i, 0)))
    args.append(res)
    return pl.pallas_call(
        _mm_res_kernel,
        out_shape=jax.ShapeDtypeStruct((t, n), F32),
        grid=(t // tm,),
        in_specs=in_specs,
        out_specs=pl.BlockSpec((tm, n), lambda i: (i, 0)),
        compiler_params=_params(("parallel",)),
        name="matmul_residual",
    )(*args)


def _side_cast_specs(casts, steps, step_of):
    in_specs, out_specs, out_shapes = [], [], []
    for w, first, count in casts:
        _, rows, cols = w.shape
        slabs = steps // count
        assert steps % count == 0 and rows % slabs == 0 and (rows // slabs) % 16 == 0
        block = (None, rows // slabs, cols)
        in_specs.append(pl.BlockSpec(
            block, lambda *idx, first=first, slabs=slabs: (first + step_of(*idx) // slabs,
                                                           step_of(*idx) % slabs, 0)))
        out_specs.append(pl.BlockSpec(
            block, lambda *idx, slabs=slabs: (step_of(*idx) // slabs, step_of(*idx) % slabs, 0)))
        out_shapes.append(jax.ShapeDtypeStruct((count, rows, cols), BF16))
    return in_specs, out_specs, out_shapes


def _mlp_kernel(x_ref, g_ref, w1_ref, w2_ref, gn_ref, *rest, tail, ncast):
    cast_in, o_ref, xn_ref = rest[:ncast], rest[ncast], rest[-1]
    cast_out = rest[len(rest) - 1 - ncast:len(rest) - 1]
    f = pl.program_id(1)

    @pl.when(f == 0)
    def _():
        x = x_ref[...]
        xn_ref[...] = _rms_normalize(x, g_ref[...]).astype(BF16)
        o_ref[...] = x

    h = jnp.maximum(_dot(xn_ref[...], w1_ref[...]), 0.0)
    for src, dst in zip(cast_in, cast_out):
        dst[...] = src[...].astype(dst.dtype)
    o_ref[...] += _dot((h * h).astype(BF16), w2_ref[...])

    @pl.when(f == pl.num_programs(1) - 1)
    def _():
        normed = _rms_normalize(o_ref[...], gn_ref[...])
        if tail == "final":
            o_ref[...] = normed
        else:
            rest[ncast + 1][...] = normed.astype(BF16)


def mlp_block(x, g, w1b, w2b, layer, gn, casts, *, tm, tf, tail):
    t, d = x.shape
    dff = w1b.shape[2]
    nf = dff // tf
    cast_in_specs, cast_out_specs, cast_shapes = _side_cast_specs(casts, (t // tm) * nf, lambda i, f: i * nf + f)
    out_shape = [jax.ShapeDtypeStruct((t, d), F32)]
    out_specs = [pl.BlockSpec((tm, d), lambda i, f: (i, 0))]
    if tail == "next":
        out_shape.append(jax.ShapeDtypeStruct((t, d), BF16))
        out_specs.append(pl.BlockSpec((tm, d), lambda i, f: (i, 0)))
    n_main = len(out_shape)
    outs = pl.pallas_call(
        functools.partial(_mlp_kernel, tail=tail, ncast=len(casts)),
        out_shape=(*out_shape, *cast_shapes),
        grid=(t // tm, nf),
        in_specs=[
            pl.BlockSpec((tm, d), lambda i, f: (i, 0)),
            pl.BlockSpec((1, d), lambda i, f: (0, 0)),
            pl.BlockSpec((None, d, tf), lambda i, f: (layer, 0, f)),
            pl.BlockSpec((None, tf, d), lambda i, f: (layer, f, 0)),
            pl.BlockSpec((1, d), lambda i, f: (0, 0)),
            *cast_in_specs,
        ],
        out_specs=(*out_specs, *cast_out_specs),
        scratch_shapes=[pltpu.VMEM((tm, d), BF16)],
        compiler_params=_params(("parallel", "arbitrary")),
        name="mlp_block",
    )(x, g.reshape(1, d), w1b, w2b, gn.reshape(1, d), *[w for w, _, _ in casts])
    return outs[:n_main], outs[n_main:]


def _mla_attn_kernel(q_ref, kv_ref, kr_ref, *rest, tq, heads, ncast):
    cast_in, o_ref, cast_out = rest[:ncast], rest[ncast], rest[ncast + 1:2 * ncast + 1]
    m_ref, l_ref, acc_ref = rest[2 * ncast + 1:]
    qi = pl.program_id(2)
    hw = 2 * LANES
    m_ref[...] = jnp.full(m_ref.shape, NEG_INF, F32)
    l_ref[...] = jnp.zeros(l_ref.shape, F32)
    acc_ref[...] = jnp.zeros(acc_ref.shape, F32)

    def block(key_start, nkeys, rows, first_visible):
        kr = kr_ref[pl.ds(key_start, nkeys), :]
        for h in range(heads):
            q = q_ref[rows, h * hw:(h + 1) * hw]
            kn = kv_ref[pl.ds(key_start, nkeys), h * hw:h * hw + MLA_NOPE]
            v = kv_ref[pl.ds(key_start, nkeys), h * hw + MLA_NOPE:(h + 1) * hw]
            s = _dot_nt(q, jnp.concatenate([kn, kr], axis=-1))
            if first_visible is not None:
                row = lax.broadcasted_iota(jnp.int32, s.shape, 0)
                col = lax.broadcasted_iota(jnp.int32, s.shape, 1)
                s = jnp.where(col <= row + first_visible, s, NEG_INF)
            m_prev = m_ref[h, rows]
            m_new = jnp.maximum(m_prev, jnp.max(s, axis=-1, keepdims=True))
            p = jnp.exp2(s - jnp.tile(m_new, (1, nkeys // LANES)))
            alpha = jnp.exp2(m_prev - m_new)
            l_ref[h, rows] = alpha * l_ref[h, rows] + jnp.sum(p, axis=-1, keepdims=True)
            acc_ref[h, rows] = alpha * acc_ref[h, rows] + _dot(p.astype(BF16), v)
            m_ref[h, rows] = m_new

    def body(ki, carry):
        block(pl.multiple_of(ki * tq, tq), tq, slice(0, tq), None)
        return carry

    lax.fori_loop(0, qi, body, 0)
    for src, dst in zip(cast_in, cast_out):
        dst[...] = src[...].astype(dst.dtype)
    block(pl.multiple_of(qi * tq, tq), tq, slice(0, tq), 0)
    for h in range(heads):
        o_ref[:, h * MLA_V:(h + 1) * MLA_V] = (acc_ref[h] / l_ref[h]).astype(o_ref.dtype)


def mla_attention(q, kv, kr, casts, *, tq, heads):
    nq = SEQ // tq
    hw = 2 * LANES * heads
    groups = MLA_HEADS // heads
    steps = BATCH * groups * nq

    cast_in_specs, cast_out_specs, cast_shapes = _side_cast_specs(
        casts, steps, lambda b, h, i: (b * groups + h) * nq + i)

    outs = pl.pallas_call(
        functools.partial(_mla_attn_kernel, tq=tq, heads=heads, ncast=len(casts)),
        out_shape=(jax.ShapeDtypeStruct((TOKENS, MLA_HEADS * MLA_V), BF16), *cast_shapes),
        grid=(BATCH, groups, nq),
        in_specs=[
            pl.BlockSpec((tq, hw), lambda b, h, i: (b * nq + i, h)),
            pl.BlockSpec((SEQ, hw), lambda b, h, i: (b, h)),
            pl.BlockSpec((SEQ, LANES), lambda b, h, i: (b, 0)),
            *cast_in_specs,
        ],
        out_specs=(pl.BlockSpec((tq, heads * MLA_V), lambda b, h, i: (b * nq + i, h)), *cast_out_specs),
        scratch_shapes=[pltpu.VMEM((heads, tq, LANES), F32), pltpu.VMEM((heads, tq, LANES), F32),
                        pltpu.VMEM((heads, tq, MLA_V), F32)],
        compiler_params=_params(("parallel", "parallel", "arbitrary")),
        name="mla_attention",
    )(q, kv, kr, *[w for w, _, _ in casts])
    return outs[0], outs[1:]


def _mlstm_kernel(qk_ref, v_ref, og_ref, gate_ref, cw_ref, cb_ref, gb_ref, out_ref,
                  hist_ref, c_ref, n_ref, m_ref):
    chunk = MLSTM_CHUNK
    c = pl.program_id(1)

    @pl.when(c == 0)
    def _():
        hist_ref[0:8, :] = jnp.zeros((8, hist_ref.shape[1]), F32)
        c_ref[...] = jnp.zeros(c_ref.shape, F32)
        n_ref[...] = jnp.zeros(n_ref.shape, F32)
        m_ref[...] = jnp.zeros(m_ref.shape, F32)

    hist_ref[8:8 + chunk, :] = qk_ref[...]
    y = jnp.broadcast_to(cb_ref[...], (chunk, hist_ref.shape[1]))
    for j in range(CONV_WIDTH):
        off = 8 - (CONV_WIDTH - 1) + j
        y = y + cw_ref[j:j + 1, :] * hist_ref[off:off + chunk, :]
    qk = y * _sigmoid(y)
    hist_ref[0:8, :] = hist_ref[chunk:chunk + 8, :]

    gates = gate_ref[...] + gb_ref[...]
    logf = _log_sigmoid(gates)
    row = lax.broadcasted_iota(jnp.int32, (chunk, chunk), 0)
    col = lax.broadcasted_iota(jnp.int32, (chunk, chunk), 1)
    tril = col <= row
    bcum = jnp.dot(tril.astype(F32), logf, preferred_element_type=F32,
                   precision=lax.Precision.HIGHEST)
    gates_t = gates.T
    bcum_t = bcum.T

    dk, dv = MLSTM_DK, MLSTM_DV
    for h in range(MLSTM_HEADS):
        q = qk[:, h * dk:(h + 1) * dk]
        k = qk[:, (MLSTM_HEADS + h) * dk:(MLSTM_HEADS + h + 1) * dk] * (dk ** -0.5)
        v = v_ref[:, h * dv:(h + 1) * dv].astype(BF16)
        b_c = bcum[:, TAIL_F + h:TAIL_F + h + 1]
        b_r = bcum_t[TAIL_F + h:TAIL_F + h + 1, :]
        i_c = gates[:, TAIL_I + h:TAIL_I + h + 1]
        i_r = gates_t[TAIL_I + h:TAIL_I + h + 1, :]
        b_last = bcum[chunk - 1:chunk, TAIL_F + h:TAIL_F + h + 1]
        m_prev = m_ref[h:h + 1, 0:1]
        n_prev = n_ref[h:h + 1, :]
        c_prev = c_ref[h]

        dmat = jnp.where(tril, b_c - b_r + i_r, NEG_INF)
        m_inter = b_c + m_prev
        m_row = jnp.maximum(m_inter, jnp.max(dmat, axis=-1, keepdims=True))
        q_b = q.astype(BF16)
        wmat = jnp.exp(dmat - m_row) * _dot_nt(q_b, k.astype(BF16))
        inter = jnp.exp(m_inter - m_row)
        num = _dot(wmat.astype(BF16), v) + inter * _dot(q_b, c_prev.astype(BF16))
        den = jnp.sum(wmat, axis=-1, keepdims=True) + inter * jnp.sum(q * n_prev, axis=-1, keepdims=True)
        hcell = num / jnp.maximum(jnp.abs(den), jnp.exp(-m_row))
        gate_o = _sigmoid(og_ref[:, h * dv:(h + 1) * dv])
        out_ref[:, h * dv:(h + 1) * dv] = (gate_o * hcell).astype(out_ref.dtype)

        g_c = b_last - b_c + i_c
        m_new = jnp.maximum(b_last + m_prev, jnp.max(g_c, axis=0, keepdims=True))
        wk = jnp.exp(g_c - m_new)
        decay = jnp.exp(b_last + m_prev - m_new)
        kw = k * wk
        c_ref[h] = decay * c_prev + _dot_tn(kw.astype(BF16), v)
        n_ref[h:h + 1, :] = decay * n_prev + jnp.sum(kw, axis=0, keepdims=True)
        m_ref[h:h + 1, :] = jnp.broadcast_to(m_new, (1, LANES))


def mlstm(proj, conv_w, conv_b, gate_bias):
    nc = SEQ // MLSTM_CHUNK
    chunk = MLSTM_CHUNK
    wqk = 2 * MLSTM_HEADS * MLSTM_DK
    wv = MLSTM_HEADS * MLSTM_DV
    row = lambda b, c: b * nc + c
    return pl.pallas_call(
        _mlstm_kernel,
        out_shape=jax.ShapeDtypeStruct((TOKENS, wv), BF16),
        grid=(BATCH, nc),
        in_specs=[
            pl.BlockSpec((chunk, wqk), lambda b, c: (row(b, c), EV_MQK // wqk)),
            pl.BlockSpec((chunk, wv), lambda b, c: (row(b, c), EV_MV // wv)),
            pl.BlockSpec((chunk, wv), lambda b, c: (row(b, c), EV_MO // wv)),
            pl.BlockSpec((chunk, LANES), lambda b, c: (row(b, c), EV_TAIL // LANES)),
            pl.BlockSpec((CONV_WIDTH, wqk), lambda b, c: (0, 0)),
            pl.BlockSpec((1, wqk), lambda b, c: (0, 0)),
            pl.BlockSpec((1, LANES), lambda b, c: (0, 0)),
        ],
        out_specs=pl.BlockSpec((chunk, wv), lambda b, c: (row(b, c), 0)),
        scratch_shapes=[
            pltpu.VMEM((chunk + 8, wqk), F32),
            pltpu.VMEM((MLSTM_HEADS, MLSTM_DK, MLSTM_DV), F32),
            pltpu.VMEM((8, MLSTM_DK), F32),
            pltpu.VMEM((8, LANES), F32),
        ],
        compiler_params=_params(("parallel", "arbitrary")),
        name="mlstm",
    )(proj, proj, proj, proj, conv_w, conv_b.reshape(1, wqk), gate_bias)


def _dilated_kernel(*refs, nres, nblk, has_prev, span, heads):
    if has_prev:
        q_ref, kp_ref, kc_ref, vp_ref, vc_ref, o_ref, lse_ref = refs
    else:
        q_ref, kc_ref, vc_ref, o_ref, lse_ref = refs
        kp_ref = vp_ref = None
    n = pl.program_id(2)
    hg = pl.program_id(3)
    blk_sz = DIL_BLOCK
    dh = DIL_HEAD_DIM
    nkeys = 2 * blk_sz if has_prev else blk_sz
    qi = lax.broadcasted_iota(jnp.int32, (blk_sz, nkeys), 0)
    kj = lax.broadcasted_iota(jnp.int32, (blk_sz, nkeys), 1)
    dist = (nkeys - blk_sz) + qi - kj
    bias_full = jnp.where(dist >= 0, jnp.where(dist <= span, 0.0, NEG_INF), NEG_INF)
    bias_first = jnp.where(kj >= nkeys - blk_sz, bias_full, NEG_INF)
    lane = lax.broadcasted_iota(jnp.int32, (blk_sz, LANES), 1)

    @pl.when(hg == 0)
    def _():
        lse_ref[...] = jnp.zeros(lse_ref.shape, F32)

    def stack(ref, res, rows):
        return jnp.stack([ref[res, rows, h * dh:(h + 1) * dh] for h in range(heads)])

    for res in range(nres):
        for blk in range(nblk):
            rows = slice(blk * blk_sz, (blk + 1) * blk_sz)
            q3 = stack(q_ref, res, rows)
            k3 = stack(kc_ref, res, rows)
            v3 = stack(vc_ref, res, rows)
            bias = bias_full
            if has_prev:
                if blk == 0:
                    first = slice(0, blk_sz)
                    kp3, vp3 = stack(kp_ref, res, first), stack(vp_ref, res, first)
                    bias = jnp.where(n > 0, bias_full, bias_first)
                else:
                    prows = slice((blk - 1) * blk_sz, blk * blk_sz)
                    kp3, vp3 = stack(kc_ref, res, prows), stack(vc_ref, res, prows)
                k3 = jnp.concatenate([kp3, k3], axis=1)
                v3 = jnp.concatenate([vp3, v3], axis=1)
            s = jnp.einsum("hqd,hkd->hqk", q3, k3, preferred_element_type=F32) + bias[None]
            m = jnp.max(s, axis=-1, keepdims=True)
            p = jnp.exp2(s - m)
            den = jnp.sum(p, axis=-1, keepdims=True)
            o = jnp.einsum("hqk,hkd->hqd", p.astype(BF16), v3, preferred_element_type=F32) / den
            lse = m * LN2 + jnp.log(den)
            lse_tile = lse_ref[res, rows, :]
            for h in range(heads):
                o_ref[res, rows, h * dh:(h + 1) * dh] = o[h].astype(o_ref.dtype)
                lse_tile = jnp.where(lane == hg * heads + h, lse[h], lse_tile)
            lse_ref[res, rows, :] = lse_tile


def dilated_pattern(qkv_perm, window, dilation, *, hw, units):
    span = window // dilation
    length = SEQ // dilation
    nblk = min(length // DIL_BLOCK, units)
    nres = min(dilation, units // nblk)
    tl = nblk * DIL_BLOCK
    ntile = length // tl
    has_prev = length > DIL_BLOCK
    heads = hw // DIL_HEAD_DIM
    ngroups = ODD_MIX // hw

    def cur(which):
        return pl.BlockSpec((None, nres, tl, hw), lambda b, r, n, g: (b, r, n, which * ngroups + g))

    def prev(which):
        return pl.BlockSpec((None, nres, DIL_BLOCK, hw),
                            lambda b, r, n, g: (b, r, jnp.maximum(n * nblk - 1, 0), which * ngroups + g))

    if has_prev:
        in_specs = [cur(0), prev(1), cur(1), prev(2), cur(2)]
    else:
        in_specs = [cur(0), cur(1), cur(2)]
    return pl.pallas_call(
        functools.partial(_dilated_kernel, nres=nres, nblk=nblk, has_prev=has_prev, span=span, heads=heads),
        out_shape=(jax.ShapeDtypeStruct((BATCH, dilation, length, ODD_MIX), BF16),
                   jax.ShapeDtypeStruct((BATCH, dilation, length, LANES), F32)),
        grid=(BATCH, dilation // nres, ntile, ngroups),
        in_specs=in_specs,
        out_specs=(pl.BlockSpec((None, nres, tl, hw), lambda b, r, n, g: (b, r, n, g)),
                   pl.BlockSpec((None, nres, tl, LANES), lambda b, r, n, g: (b, r, n, 0))),
        compiler_params=_params(("parallel", "parallel", "arbitrary", "arbitrary")),
        name="dilated_d%d" % dilation,
    )(*([qkv_perm] * len(in_specs)))


def _merge_kernel(o1_ref, o4_ref, o16_ref, l1_ref, l4_ref, l16_ref, out_ref,
                  nat_scr, p4_scr, o4_scr, o_scr, *, tm):
    dh = DIL_HEAD_DIM
    q4, q16 = tm // 4, tm // 16

    for r4 in range(4):
        for a in range(4):
            p4_scr[r4, pl.ds(a, q16, stride=4), :] = l16_ref[r4 + 4 * a]
    for r4 in range(4):
        nat_scr[0, pl.ds(r4, q4, stride=4), :] = l4_ref[r4]
        nat_scr[1, pl.ds(r4, q4, stride=4), :] = p4_scr[r4]
    l1, l4, l16 = l1_ref[0], nat_scr[0], nat_scr[1]
    mx = jnp.maximum(jnp.maximum(l1, l4), l16)
    e1, e4, e16 = jnp.exp(l1 - mx), jnp.exp(l4 - mx), jnp.exp(l16 - mx)
    tot = e1 + e4 + e16
    w1 = e1 / tot
    nat_scr[0] = e4 / tot
    nat_scr[1] = e16 / tot
    w4 = jnp.concatenate([nat_scr[0, pl.ds(r4, q4, stride=4), :] for r4 in range(4)], axis=0)
    for r4 in range(4):
        p4_scr[r4] = nat_scr[1, pl.ds(r4, q4, stride=4), :]
    w16_parts = [None] * 16
    for r4 in range(4):
        for a in range(4):
            w16_parts[r4 + 4 * a] = p4_scr[r4, pl.ds(a, q16, stride=4), :]
    w16 = jnp.concatenate(w16_parts, axis=0)

    head_of_col = lax.broadcasted_iota(jnp.int32, (2 * LANES, ODD_MIX), 1) // dh
    lane_of_row = lax.broadcasted_iota(jnp.int32, (2 * LANES, ODD_MIX), 0) % LANES
    expand = jnp.where(head_of_col == lane_of_row, 1.0, 0.0).astype(BF16)

    def two_terms(w):
        hi = w.astype(BF16)
        return jnp.concatenate([hi, (w - hi.astype(F32)).astype(BF16)], axis=1)

    w1, w4, w16 = two_terms(w1), two_terms(w4), two_terms(w16)

    pair = 2
    for hp in range(DIL_HEADS // pair):
        spread = expand[:, hp * pair * dh:(hp + 1) * pair * dh]
        w1p, w4p, w16p = _dot(w1, spread), _dot(w4, spread), _dot(w16, spread)
        for hh in range(pair):
            cols = slice((hp * pair + hh) * dh, (hp * pair + hh + 1) * dh)
            part = slice(hh * dh, (hh + 1) * dh)
            for r4 in range(4):
                for a in range(4):
                    r = r4 + 4 * a
                    o4_scr[r4, pl.ds(a, q16, stride=4), :] = (w16p[r * q16:(r + 1) * q16, part]
                                                             * o16_ref[r, :, cols])
            for r4 in range(4):
                acc4 = o4_scr[r4] + w4p[r4 * q4:(r4 + 1) * q4, part] * o4_ref[r4, :, cols]
                o_scr[pl.ds(r4, q4, stride=4), :] = acc4
            out_ref[:, cols] = (o_scr[...] + w1p[:, part] * o1_ref[0, :, cols]).astype(out_ref.dtype)


def merge_patterns(outs, lses, dilations, *, tm):
    assert tuple(dilations) == (1, 4, 16), "the re-interleave is written as two stride-4 passes"
    tiles = SEQ // tm

    def spec(d, width):
        return pl.BlockSpec((None, d, tm // d, width), lambda i: (i // tiles, 0, i % tiles, 0))

    return pl.pallas_call(
        functools.partial(_merge_kernel, tm=tm),
        out_shape=jax.ShapeDtypeStruct((TOKENS, ODD_MIX), BF16),
        grid=(TOKENS // tm,),
        in_specs=[spec(d, ODD_MIX) for d in dilations] + [spec(d, LANES) for d in dilations],
        out_specs=pl.BlockSpec((tm, ODD_MIX), lambda i: (i, 0)),
        scratch_shapes=[pltpu.VMEM((2, tm, LANES), F32),
                        pltpu.VMEM((4, tm // 4, LANES), F32),
                        pltpu.VMEM((4, tm // 4, DIL_HEAD_DIM), F32),
                        pltpu.VMEM((tm, DIL_HEAD_DIM), F32)],
        compiler_params=_params(("parallel",)),
        name="merge_patterns",
    )(*outs, *lses)


def _rope_tables():
    pos = jnp.arange(SEQ, dtype=F32)[:, None]
    half = MLA_ROPE // 2
    inv = ROPE_THETA ** (-jnp.arange(half, dtype=F32) * 2.0 / MLA_ROPE)
    ang = pos * inv[None, :]
    c, s = jnp.cos(ang), jnp.sin(ang)
    z = jnp.zeros_like(c)
    small = jnp.concatenate([c, c, z, z, -s, z, z, z, z, s, z, z], axis=-1)
    small_q = jnp.concatenate([c, c, z, z, -s, s, z, z], axis=-1)
    half = DIL_HEAD_DIM // 2
    inv = ROPE_THETA ** (-jnp.arange(half, dtype=F32) * 2.0 / DIL_HEAD_DIM)
    ang = pos * inv[None, :]
    c, s = jnp.cos(ang), jnp.sin(ang)
    full = jnp.concatenate([c, c, -s, s], axis=-1)
    return small, small_q, full


def _w_in_relayout_kernel(wt_ref, o_ref):
    sizes = (MLA_Q_LORA, MLA_KV_LORA, MLA_ROPE, 2 * MLSTM_HEADS * MLSTM_DK, MLSTM_HEADS * MLSTM_DV,
             MLSTM_HEADS, MLSTM_HEADS, MLSTM_HEADS * MLSTM_DV)
    ends = np.cumsum(sizes)
    c_q, c_kv, k_r, m_qk, m_v, m_i, m_f, m_o = [slice(int(e - n), int(e)) for e, n in zip(ends, sizes)]
    gates = slice(m_i.start, m_f.stop)
    rows = wt_ref.shape[1]
    pad = jnp.zeros((LANES - MLA_ROPE - (gates.stop - gates.start), rows), F32)
    tail = jnp.concatenate([wt_ref[k_r, :], wt_ref[gates, :], pad], axis=0)
    groups = [wt_ref[c, :].T for c in (c_q, c_kv, m_qk, m_v, m_o)] + [tail.T]
    o_ref[...] = jnp.concatenate(groups, axis=1).astype(o_ref.dtype)


def even_in_weights(ev_w_in, index, *, rows):
    _, d, n = ev_w_in.shape
    return pl.pallas_call(
        _w_in_relayout_kernel,
        out_shape=jax.ShapeDtypeStruct((d, EV_PROJ), BF16),
        grid=(d // rows,),
        in_specs=[pl.BlockSpec((None, n, rows), lambda r: (index, 0, r))],
        out_specs=pl.BlockSpec((rows, EV_PROJ), lambda r: (r, 0)),
        compiler_params=_params(("parallel",)),
        name="even_in_weights",
    )(jnp.swapaxes(ev_w_in, 1, 2))


def _even_weights(w_uq, w_ukv, b_i, b_f):
    uq = w_uq.reshape(MLA_Q_LORA, MLA_HEADS, MLA_QK)
    uq = jnp.concatenate([uq, uq[:, :, MLA_NOPE:]], axis=-1)
    w_uq_r = uq.reshape(MLA_Q_LORA, MLA_HEADS * 2 * LANES).astype(BF16)
    gate_bias = jnp.concatenate([jnp.zeros((TAIL_I,), F32), b_i.astype(F32), b_f.astype(F32),
                                 jnp.zeros((LANES - TAIL_F - MLSTM_HEADS,), F32)]).reshape(1, LANES)
    return w_uq_r, w_ukv.astype(BF16), gate_bias


def _even_layer(x, xn, w_in_r, q_norm, w_uq, kv_norm, w_ukv, conv_w, conv_b, b_i, b_f, tab_q, tab_k, casts):
    w_uq_r, w_ukv_r, gate_bias = _even_weights(w_uq, w_ukv, b_i, b_f)
    proj = matmul_weight_stationary(xn, w_in_r, tm=1024, tn=EV_PROJ // 3, out_dtype=F32)
    scale = MLA_QK ** -0.5 * LOG2E
    q, kv, kr = mla_prep(proj, q_norm, kv_norm, w_uq_r, w_ukv_r, tab_q * scale, tab_k, tm=512, scale=scale)
    a_out, cast_weights = mla_attention(q, kv, kr, casts, tq=512, heads=4)
    hm = mlstm(proj, conv_w, conv_b, gate_bias)
    w_out_b = cast_weights[0][0]
    assert a_out.shape[1] == hm.shape[1]
    return matmul_residual(x, [(a_out, w_out_b, 0), (hm, w_out_b, 1)], tm=512), cast_weights


def _odd_layer(x, xn, w_qkv_b, w_out_b, full_tab):
    scale = DIL_HEAD_DIM ** -0.5 * LOG2E
    identity = jnp.concatenate([jnp.ones((SEQ, LANES), F32), jnp.zeros((SEQ, LANES), F32)], axis=-1)
    tab = jnp.stack([full_tab * scale, full_tab, identity])
    dilations = [d for _, d in DIL_PATTERNS]
    qkv_perms = qkv_projection(xn, w_qkv_b, tab, dilations, tm=1024, tn=1024)
    outs, lses = [], []
    for (window, dilation), qkv_perm in zip(DIL_PATTERNS, qkv_perms):
        o_g, lse_g = dilated_pattern(qkv_perm, window, dilation, hw=1024, units=8)
        outs.append(o_g)
        lses.append(lse_g)
    o = merge_patterns(outs, lses, dilations, tm=512)
    return matmul_residual(x, [(o, w_out_b, 0)], tm=512)


def kernel(x, norm_mix, norm_mlp, ev_w_in, mla_q_norm, mla_w_uq, mla_kv_norm, mla_w_ukv,
           mlstm_conv_w, mlstm_conv_b, mlstm_b_i, mlstm_b_f, ev_w_out, od_w_qkv, od_w_out,
           mlp_w1, mlp_w2, norm_final):
    assert x.shape == (BATCH, SEQ, D_MODEL) and x.dtype == F32
    tab_k, tab_q, full_tab = _rope_tables()
    xt = x.reshape(TOKENS, D_MODEL)
    xn = rms_norm_bf16(xt, norm_mix[0], tm=512)
    w1b = w2b = w_qkv_b = w_od_out_b = None
    for layer in range(DEPTH):
        i = layer // 2
        if layer % 2 == 0:
            casts = [(ev_w_out, i, 1)]
            if layer + 1 < DEPTH:
                casts += [(od_w_qkv, i, 1), (od_w_out, i, 1)]
            if layer == 0:
                casts += [(mlp_w1, 0, 1), (mlp_w2, 0, 1)]
            w_in_r = even_in_weights(ev_w_in, i, rows=256)
            xt, cast_weights = _even_layer(xt, xn, w_in_r, mla_q_norm[i], mla_w_uq[i], mla_kv_norm[i],
                                           mla_w_ukv[i], mlstm_conv_w[i], mlstm_conv_b[i], mlstm_b_i[i],
                                           mlstm_b_f[i], tab_q, tab_k, casts)
            if layer + 1 < DEPTH:
                w_qkv_b, w_od_out_b = cast_weights[1][0], cast_weights[2][0]
            if layer == 0:
                w1b, w2b = cast_weights[-2], cast_weights[-1]
        else:
            xt = _odd_layer(xt, xn, w_qkv_b, w_od_out_b, full_tab)
        if layer == DEPTH - 1:
            (xt,), _ = mlp_block(xt, norm_mlp[layer], w1b, w2b, 0, norm_final, [], tm=512, tf=1024,
                                 tail="final")
        else:
            casts = [(mlp_w1, layer + 1, 1), (mlp_w2, layer + 1, 1)]
            (xt, xn), (w1b, w2b) = mlp_block(xt, norm_mlp[layer], w1b, w2b, 0, norm_mix[layer + 1], casts,
                                             tm=512, tf=1024, tail="next")
    return xt.reshape(BATCH, SEQ, D_MODEL)
```

```python
import functools

import jax
import jax.numpy as jnp
import numpy as np
from jax import lax
from jax.experimental import pallas as pl
from jax.experimental.pallas import tpu as pltpu

D_MODEL = 2048
BATCH = 4
SEQ = 2048
DEPTH = 4
MLA_HEADS = 8
MLA_Q_LORA = 512
MLA_KV_LORA = 512
MLA_NOPE = 128
MLA_ROPE = 64
MLA_V = 128
MLA_QK = MLA_NOPE + MLA_ROPE
MLSTM_HEADS = 4
MLSTM_DK = 128
MLSTM_DV = 256
MLSTM_CHUNK = 128
CONV_WIDTH = 4
DIL_HEADS = 16
DIL_HEAD_DIM = 128
DIL_PATTERNS = ((128, 1), (512, 4), (2048, 16))
DIL_BLOCK = 128
D_FF = 4 * D_MODEL
ROPE_THETA = 10000.0
NORM_EPS = 1e-6
ODD_MIX = DIL_HEADS * DIL_HEAD_DIM
TOKENS = BATCH * SEQ

LANES = 128
VMEM_LIMIT_BYTES = 56 * 1024 * 1024

EV_CQ, EV_CKV, EV_MQK, EV_MV, EV_MO, EV_TAIL = 0, 512, 1024, 2048, 3072, 4096
EV_PROJ = 4224
TAIL_I = MLA_ROPE
TAIL_F = MLA_ROPE + MLSTM_HEADS

F32 = jnp.float32
BF16 = jnp.bfloat16
NEG_INF = float("-inf")
LOG2E = float(np.log2(np.e))
LN2 = float(np.log(2.0))


def _params(semantics):
    return pltpu.CompilerParams(dimension_semantics=semantics, vmem_limit_bytes=VMEM_LIMIT_BYTES)


def _rms_normalize(x, g):
    ms = jnp.mean(x * x, axis=-1, keepdims=True)
    return x * lax.rsqrt(ms + NORM_EPS) * g


def _dot(a, b):
    return jnp.dot(a, b, preferred_element_type=F32)


def _dot_nt(a, b):
    return lax.dot_general(a, b, (((1,), (1,)), ((), ())), preferred_element_type=F32)


def _dot_tn(a, b):
    return lax.dot_general(a, b, (((0,), (0,)), ((), ())), preferred_element_type=F32)


def _sigmoid(x):
    return 1.0 / (1.0 + jnp.exp(-x))


def _log_sigmoid(x):
    return jnp.minimum(x, 0.0) - jnp.log1p(jnp.exp(-jnp.abs(x)))


def _rope_small(y, tab):
    return (y * tab[:, 0:LANES] + pltpu.roll(y, 96, 1) * tab[:, LANES:2 * LANES]
            + pltpu.roll(y, 32, 1) * tab[:, 2 * LANES:3 * LANES])


def _rope_full(y, tab):
    return y * tab[:, 0:LANES] + pltpu.roll(y, 64, 1) * tab[:, LANES:2 * LANES]


def _mla_prep_kernel(cq_ref, ckv_ref, tail_ref, gq_ref, gkv_ref, wq_ref, wkv_ref, tabq_ref, tabk_ref,
                     q_ref, kv_ref, kr_ref, *, scale):
    cq = _rms_normalize(cq_ref[...], gq_ref[...]).astype(BF16)
    ckv = _rms_normalize(ckv_ref[...], gkv_ref[...]).astype(BF16)
    kv_ref[...] = _dot(ckv, wkv_ref[...]).astype(kv_ref.dtype)
    y = _dot(cq, wq_ref[...])
    tab = tabq_ref[...]
    for grp in range(y.shape[1] // LANES):
        yg = y[:, grp * LANES:(grp + 1) * LANES]
        if grp % 2 == 0:
            og = yg * scale
        else:
            og = yg * tab[:, 0:LANES] + pltpu.roll(yg, 32, 1) * tab[:, LANES:2 * LANES]
        q_ref[:, grp * LANES:(grp + 1) * LANES] = og.astype(q_ref.dtype)
    kr_ref[...] = _rope_small(tail_ref[...], tabk_ref[...]).astype(kr_ref.dtype)


def mla_prep(proj, q_norm, kv_norm, w_uq_r, w_ukv, tab_q, tab_k, *, tm, scale):
    pos_tiles = SEQ // tm
    nq, nkv = w_uq_r.shape[1], w_ukv.shape[1]
    return pl.pallas_call(
        functools.partial(_mla_prep_kernel, scale=scale),
        out_shape=(jax.ShapeDtypeStruct((TOKENS, nq), BF16), jax.ShapeDtypeStruct((TOKENS, nkv), BF16),
                   jax.ShapeDtypeStruct((TOKENS, LANES), BF16)),
        grid=(TOKENS // tm,),
        in_specs=[
            pl.BlockSpec((tm, MLA_Q_LORA), lambda i: (i, EV_CQ // MLA_Q_LORA)),
            pl.BlockSpec((tm, MLA_KV_LORA), lambda i: (i, EV_CKV // MLA_KV_LORA)),
            pl.BlockSpec((tm, LANES), lambda i: (i, EV_TAIL // LANES)),
            pl.BlockSpec((1, MLA_Q_LORA), lambda i: (0, 0)),
            pl.BlockSpec((1, MLA_KV_LORA), lambda i: (0, 0)),
            pl.BlockSpec((MLA_Q_LORA, nq), lambda i: (0, 0)),
            pl.BlockSpec((MLA_KV_LORA, nkv), lambda i: (0, 0)),
            pl.BlockSpec((tm, 2 * LANES), lambda i: (i % pos_tiles, 0)),
            pl.BlockSpec((tm, 3 * LANES), lambda i: (i % pos_tiles, 0)),
        ],
        out_specs=(pl.BlockSpec((tm, nq), lambda i: (i, 0)), pl.BlockSpec((tm, nkv), lambda i: (i, 0)),
                   pl.BlockSpec((tm, LANES), lambda i: (i, 0))),
        compiler_params=_params(("parallel",)),
        name="mla_prep",
    )(proj, proj, proj, q_norm.reshape(1, MLA_Q_LORA), kv_norm.reshape(1, MLA_KV_LORA), w_uq_r, w_ukv,
      tab_q, tab_k)


def _qkv_proj_kernel(xn_ref, w_ref, tab_ref, *rest, dilations, tm, sub_cols):
    out_refs, (scr_ref, scr4_ref) = rest[:len(dilations)], rest[len(dilations):]
    out_by_d = dict(zip(dilations, out_refs))
    tab = tab_ref[...]
    for sub in range(w_ref.shape[1] // sub_cols):
        y = _dot(xn_ref[...], w_ref[:, sub * sub_cols:(sub + 1) * sub_cols])
        for part in range(sub_cols // LANES):
            grp = sub * (sub_cols // LANES) + part
            cols = slice(grp * LANES, (grp + 1) * LANES)
            yg = _rope_full(y[:, part * LANES:(part + 1) * LANES], tab)
            out_by_d[1][0, :, cols] = yg.astype(BF16)
            scr_ref[grp] = yg
            for r4 in range(4):
                v = scr_ref[grp, pl.ds(r4, tm // 4, stride=4), :]
                out_by_d[4][r4, :, cols] = v.astype(BF16)
                scr4_ref[r4] = v
            for r4 in range(4):
                for a in range(4):
                    w = scr4_ref[r4, pl.ds(a, tm // 16, stride=4), :]
                    out_by_d[16][r4 + 4 * a, :, cols] = w.astype(BF16)


def qkv_projection(xn, w, tab, dilations, *, tm, tn):
    assert tuple(dilations) == (1, 4, 16), "the de-interleave is written as two stride-4 passes"
    t, k_dim = xn.shape
    n = w.shape[1]
    tiles = SEQ // tm
    q_tiles = ODD_MIX // tn
    return pl.pallas_call(
        functools.partial(_qkv_proj_kernel, dilations=tuple(dilations), tm=tm, sub_cols=2 * LANES),
        out_shape=tuple(jax.ShapeDtypeStruct((BATCH, d, SEQ // d, n), BF16) for d in dilations),
        grid=(n // tn, t // tm),
        in_specs=[
            pl.BlockSpec((tm, k_dim), lambda j, i: (i, 0)),
            pl.BlockSpec((k_dim, tn), lambda j, i: (0, j)),
            pl.BlockSpec((None, tm, 2 * LANES), lambda j, i: (j // q_tiles, i % tiles, 0)),
        ],
        out_specs=tuple(pl.BlockSpec((None, d, tm // d, tn), lambda j, i: (i // tiles, 0, i % tiles, j))
                        for d in dilations),
        scratch_shapes=[pltpu.VMEM((tn // LANES, tm, LANES), F32), pltpu.VMEM((4, tm // 4, LANES), F32)],
        compiler_params=_params(("parallel", "arbitrary")),
        name="qkv_projection",
    )(xn, w, tab)


def _norm_kernel(x_ref, g_ref, o_ref):
    o_ref[...] = _rms_normalize(x_ref[...], g_ref[...]).astype(o_ref.dtype)


def rms_norm_bf16(x, g, *, tm):
    t, d = x.shape
    return pl.pallas_call(
        _norm_kernel,
        out_shape=jax.ShapeDtypeStruct((t, d), BF16),
        grid=(t // tm,),
        in_specs=[pl.BlockSpec((tm, d), lambda i: (i, 0)), pl.BlockSpec((1, d), lambda i: (0, 0))],
        out_specs=pl.BlockSpec((tm, d), lambda i: (i, 0)),
        compiler_params=_params(("parallel",)),
        name="rms_norm_bf16",
    )(x, g.reshape(1, d))


def _mm_kernel(a_ref, w_ref, o_ref):
    o_ref[...] = _dot(a_ref[...], w_ref[...]).astype(o_ref.dtype)


def matmul_weight_stationary(a, w, *, tm, tn, out_dtype):
    t, k = a.shape
    n = w.shape[1]
    return pl.pallas_call(
        _mm_kernel,
        out_shape=jax.ShapeDtypeStruct((t, n), out_dtype),
        grid=(n // tn, t // tm),
        in_specs=[pl.BlockSpec((tm, k), lambda j, i: (i, 0)), pl.BlockSpec((k, tn), lambda j, i: (0, j))],
        out_specs=pl.BlockSpec((tm, tn), lambda j, i: (i, j)),
        compiler_params=_params(("parallel", "arbitrary")),
        name="matmul_weight_stationary",
    )(a, w)


def _mm_res_kernel(*refs):
    r_ref, o_ref = refs[-2], refs[-1]
    acc = r_ref[...]
    for k in range((len(refs) - 2) // 2):
        acc = acc + _dot(refs[2 * k][...], refs[2 * k + 1][...])
    o_ref[...] = acc


def matmul_residual(res, pairs, *, tm):
    t, n = res.shape
    in_specs, args = [], []
    for a, w, row_block in pairs:
        k = a.shape[1]
        in_specs += [pl.BlockSpec((tm, k), lambda i: (i, 0)),
                     pl.BlockSpec((k, n), lambda i, row_block=row_block: (row_block, 0))]
        args += [a, w]
    in_specs.append(pl.BlockSpec((tm, n), lambda i: (i, 0)))
    args.append(res)
    return pl.pallas_call(
        _mm_res_kernel,
        out_shape=jax.ShapeDtypeStruct((t, n), F32),
        grid=(t // tm,),
        in_specs=in_specs,
        out_specs=pl.BlockSpec((tm, n), lambda i: (i, 0)),
        compiler_params=_params(("parallel",)),
        name="matmul_residual",
    )(*args)


def _side_cast_specs(casts, steps, step_of):
    in_specs, out_specs, out_shapes = [], [], []
    for w, first, count in casts:
        _, rows, cols = w.shape
        slabs = steps // count
        assert steps % count == 0 and rows % slabs == 0 and (rows // slabs) % 16 == 0
        block = (None, rows // slabs, cols)
        in_specs.append(pl.BlockSpec(
            block, lambda *idx, first=first, slabs=slabs: (first + step_of(*idx) // slabs,
                                                           step_of(*idx) % slabs, 0)))
        out_specs.append(pl.BlockSpec(
            block, lambda *idx, slabs=slabs: (step_of(*idx) // slabs, step_of(*idx) % slabs, 0)))
        out_shapes.append(jax.ShapeDtypeStruct((count, rows, cols), BF16))
    return in_specs, out_specs, out_shapes


def _mlp_kernel(x_ref, g_ref, w1_ref, w2_ref, gn_ref, *rest, tail, ncast):
    cast_in, o_ref, xn_ref = rest[:ncast], rest[ncast], rest[-1]
    cast_out = rest[len(rest) - 1 - ncast:len(rest) - 1]
    f = pl.program_id(1)

    @pl.when(f == 0)
    def _():
        x = x_ref[...]
        xn_ref[...] = _rms_normalize(x, g_ref[...]).astype(BF16)
        o_ref[...] = x

    h = jnp.maximum(_dot(xn_ref[...], w1_ref[...]), 0.0)
    for src, dst in zip(cast_in, cast_out):
        dst[...] = src[...].astype(dst.dtype)
    o_ref[...] += _dot((h * h).astype(BF16), w2_ref[...])

    @pl.when(f == pl.num_programs(1) - 1)
    def _():
        normed = _rms_normalize(o_ref[...], gn_ref[...])
        if tail == "final":
            o_ref[...] = normed
        else:
            rest[ncast + 1][...] = normed.astype(BF16)


def mlp_block(x, g, w1b, w2b, layer, gn, casts, *, tm, tf, tail):
    t, d = x.shape
    dff = w1b.shape[2]
    nf = dff // tf
    cast_in_specs, cast_out_specs, cast_shapes = _side_cast_specs(casts, (t // tm) * nf, lambda i, f: i * nf + f)
    out_shape = [jax.ShapeDtypeStruct((t, d), F32)]
    out_specs = [pl.BlockSpec((tm, d), lambda i, f: (i, 0))]
    if tail == "next":
        out_shape.append(jax.ShapeDtypeStruct((t, d), BF16))
        out_specs.append(pl.BlockSpec((tm, d), lambda i, f: (i, 0)))
    n_main = len(out_shape)
    outs = pl.pallas_call(
        functools.partial(_mlp_kernel, tail=tail, ncast=len(casts)),
        out_shape=(*out_shape, *cast_shapes),
        grid=(t // tm, nf),
        in_specs=[
            pl.BlockSpec((tm, d), lambda i, f: (i, 0)),
            pl.BlockSpec((1, d), lambda i, f: (0, 0)),
            pl.BlockSpec((None, d, tf), lambda i, f: (layer, 0, f)),
            pl.BlockSpec((None, tf, d), lambda i, f: (layer, f, 0)),
            pl.BlockSpec((1, d), lambda i, f: (0, 0)),
            *cast_in_specs,
        ],
        out_specs=(*out_specs, *cast_out_specs),
        scratch_shapes=[pltpu.VMEM((tm, d), BF16)],
        compiler_params=_params(("parallel", "arbitrary")),
        name="mlp_block",
    )(x, g.reshape(1, d), w1b, w2b, gn.reshape(1, d), *[w for w, _, _ in casts])
    return outs[:n_main], outs[n_main:]


def _mla_attn_kernel(q_ref, kv_ref, kr_ref, *rest, tq, heads, ncast):
    cast_in, o_ref, cast_out = rest[:ncast], rest[ncast], rest[ncast + 1:2 * ncast + 1]
    m_ref, l_ref, acc_ref = rest[2 * ncast + 1:]
    qi = pl.program_id(2)
    hw = 2 * LANES
    m_ref[...] = jnp.full(m_ref.shape, NEG_INF, F32)
    l_ref[...] = jnp.zeros(l_ref.shape, F32)
    acc_ref[...] = jnp.zeros(acc_ref.shape, F32)

    def block(key_start, nkeys, rows, first_visible):
        kr = kr_ref[pl.ds(key_start, nkeys), :]
        for h in range(heads):
            q = q_ref[rows, h * hw:(h + 1) * hw]
            kn = kv_ref[pl.ds(key_start, nkeys), h * hw:h * hw + MLA_NOPE]
            v = kv_ref[pl.ds(key_start, nkeys), h * hw + MLA_NOPE:(h + 1) * hw]
            s = _dot_nt(q, jnp.concatenate([kn, kr], axis=-1))
            if first_visible is not None:
                row = lax.broadcasted_iota(jnp.int32, s.shape, 0)
                col = lax.broadcasted_iota(jnp.int32, s.shape, 1)
                s = jnp.where(col <= row + first_visible, s, NEG_INF)
            m_prev = m_ref[h, rows]
            m_new = jnp.maximum(m_prev, jnp.max(s, axis=-1, keepdims=True))
            p = jnp.exp2(s - jnp.tile(m_new, (1, nkeys // LANES)))
            alpha = jnp.exp2(m_prev - m_new)
            l_ref[h, rows] = alpha * l_ref[h, rows] + jnp.sum(p, axis=-1, keepdims=True)
            acc_ref[h, rows] = alpha * acc_ref[h, rows] + _dot(p.astype(BF16), v)
            m_ref[h, rows] = m_new

    def body(ki, carry):
        block(pl.multiple_of(ki * tq, tq), tq, slice(0, tq), None)
        return carry

    lax.fori_loop(0, qi, body, 0)
    for src, dst in zip(cast_in, cast_out):
        dst[...] = src[...].astype(dst.dtype)
    block(pl.multiple_of(qi * tq, tq), tq, slice(0, tq), 0)
    for h in range(heads):
        o_ref[:, h * MLA_V:(h + 1) * MLA_V] = (acc_ref[h] / l_ref[h]).astype(o_ref.dtype)


def mla_attention(q, kv, kr, casts, *, tq, heads):
    nq = SEQ // tq
    hw = 2 * LANES * heads
    groups = MLA_HEADS // heads
    steps = BATCH * groups * nq

    cast_in_specs, cast_out_specs, cast_shapes = _side_cast_specs(
        casts, steps, lambda b, h, i: (b * groups + h) * nq + i)

    outs = pl.pallas_call(
        functools.partial(_mla_attn_kernel, tq=tq, heads=heads, ncast=len(casts)),
        out_shape=(jax.ShapeDtypeStruct((TOKENS, MLA_HEADS * MLA_V), BF16), *cast_shapes),
        grid=(BATCH, groups, nq),
        in_specs=[
            pl.BlockSpec((tq, hw), lambda b, h, i: (b * nq + i, h)),
            pl.BlockSpec((SEQ, hw), lambda b, h, i: (b, h)),
            pl.BlockSpec((SEQ, LANES), lambda b, h, i: (b, 0)),
            *cast_in_specs,
        ],
        out_specs=(pl.BlockSpec((tq, heads * MLA_V), lambda b, h, i: (b * nq + i, h)), *cast_out_specs),
        scratch_shapes=[pltpu.VMEM((heads, tq, LANES), F32), pltpu.VMEM((heads, tq, LANES), F32),
                        pltpu.VMEM((heads, tq, MLA_V), F32)],
        compiler_params=_params(("parallel", "parallel", "arbitrary")),
        name="mla_attention",
    )(q, kv, kr, *[w for w, _, _ in casts])
    return outs[0], outs[1:]


def _mlstm_kernel(qk_ref, v_ref, og_ref, gate_ref, cw_ref, cb_ref, gb_ref, out_ref,
                  hist_ref, c_ref, n_ref, m_ref):
    chunk = MLSTM_CHUNK
    c = pl.program_id(1)

    @pl.when(c == 0)
    def _():
        hist_ref[0:8, :] = jnp.zeros((8, hist_ref.shape[1]), F32)
        c_ref[...] = jnp.zeros(c_ref.shape, F32)
        n_ref[...] = jnp.zeros(n_ref.shape, F32)
        m_ref[...] = jnp.zeros(m_ref.shape, F32)

    hist_ref[8:8 + chunk, :] = qk_ref[...]
    y = jnp.broadcast_to(cb_ref[...], (chunk, hist_ref.shape[1]))
    for j in range(CONV_WIDTH):
        off = 8 - (CONV_WIDTH - 1) + j
        y = y + cw_ref[j:j + 1, :] * hist_ref[off:off + chunk, :]
    qk = y * _sigmoid(y)
    hist_ref[0:8, :] = hist_ref[chunk:chunk + 8, :]

    gates = gate_ref[...] + gb_ref[...]
    logf = _log_sigmoid(gates)
    row = lax.broadcasted_iota(jnp.int32, (chunk, chunk), 0)
    col = lax.broadcasted_iota(jnp.int32, (chunk, chunk), 1)
    tril = col <= row
    bcum = jnp.dot(tril.astype(F32), logf, preferred_element_type=F32,
                   precision=lax.Precision.HIGHEST)
    gates_t = gates.T
    bcum_t = bcum.T

    dk, dv = MLSTM_DK, MLSTM_DV
    for h in range(MLSTM_HEADS):
        q = qk[:, h * dk:(h + 1) * dk]
        k = qk[:, (MLSTM_HEADS + h) * dk:(MLSTM_HEADS + h + 1) * dk] * (dk ** -0.5)
        v = v_ref[:, h * dv:(h + 1) * dv].astype(BF16)
        b_c = bcum[:, TAIL_F + h:TAIL_F + h + 1]
        b_r = bcum_t[TAIL_F + h:TAIL_F + h + 1, :]
        i_c = gates[:, TAIL_I + h:TAIL_I + h + 1]
        i_r = gates_t[TAIL_I + h:TAIL_I + h + 1, :]
        b_last = bcum[chunk - 1:chunk, TAIL_F + h:TAIL_F + h + 1]
        m_prev = m_ref[h:h + 1, 0:1]
        n_prev = n_ref[h:h + 1, :]
        c_prev = c_ref[h]

        dmat = jnp.where(tril, b_c - b_r + i_r, NEG_INF)
        m_inter = b_c + m_prev
        m_row = jnp.maximum(m_inter, jnp.max(dmat, axis=-1, keepdims=True))
        q_b = q.astype(BF16)
        wmat = jnp.exp(dmat - m_row) * _dot_nt(q_b, k.astype(BF16))
        inter = jnp.exp(m_inter - m_row)
        num = _dot(wmat.astype(BF16), v) + inter * _dot(q_b, c_prev.astype(BF16))
        den = jnp.sum(wmat, axis=-1, keepdims=True) + inter * jnp.sum(q * n_prev, axis=-1, keepdims=True)
        hcell = num / jnp.maximum(jnp.abs(den), jnp.exp(-m_row))
        gate_o = _sigmoid(og_ref[:, h * dv:(h + 1) * dv])
        out_ref[:, h * dv:(h + 1) * dv] = (gate_o * hcell).astype(out_ref.dtype)

        g_c = b_last - b_c + i_c
        m_new = jnp.maximum(b_last + m_prev, jnp.max(g_c, axis=0, keepdims=True))
        wk = jnp.exp(g_c - m_new)
        decay = jnp.exp(b_last + m_prev - m_new)
        kw = k * wk
        c_ref[h] = decay * c_prev + _dot_tn(kw.astype(BF16), v)
        n_ref[h:h + 1, :] = decay * n_prev + jnp.sum(kw, axis=0, keepdims=True)
        m_ref[h:h + 1, :] = jnp.broadcast_to(m_new, (1, LANES))


def mlstm(proj, conv_w, conv_b, gate_bias):
    nc = SEQ // MLSTM_CHUNK
    chunk = MLSTM_CHUNK
    wqk = 2 * MLSTM_HEADS * MLSTM_DK
    wv = MLSTM_HEADS * MLSTM_DV
    row = lambda b, c: b * nc + c
    return pl.pallas_call(
        _mlstm_kernel,
        out_shape=jax.ShapeDtypeStruct((TOKENS, wv), BF16),
        grid=(BATCH, nc),
        in_specs=[
            pl.BlockSpec((chunk, wqk), lambda b, c: (row(b, c), EV_MQK // wqk)),
            pl.BlockSpec((chunk, wv), lambda b, c: (row(b, c), EV_MV // wv)),
            pl.BlockSpec((chunk, wv), lambda b, c: (row(b, c), EV_MO // wv)),
            pl.BlockSpec((chunk, LANES), lambda b, c: (row(b, c), EV_TAIL // LANES)),
            pl.BlockSpec((CONV_WIDTH, wqk), lambda b, c: (0, 0)),
            pl.BlockSpec((1, wqk), lambda b, c: (0, 0)),
            pl.BlockSpec((1, LANES), lambda b, c: (0, 0)),
        ],
        out_specs=pl.BlockSpec((chunk, wv), lambda b, c: (row(b, c), 0)),
        scratch_shapes=[
            pltpu.VMEM((chunk + 8, wqk), F32),
            pltpu.VMEM((MLSTM_HEADS, MLSTM_DK, MLSTM_DV), F32),
            pltpu.VMEM((8, MLSTM_DK), F32),
            pltpu.VMEM((8, LANES), F32),
        ],
        compiler_params=_params(("parallel", "arbitrary")),
        name="mlstm",
    )(proj, proj, proj, proj, conv_w, conv_b.reshape(1, wqk), gate_bias)


def _dilated_kernel(*refs, nres, nblk, has_prev, span, heads):
    if has_prev:
        q_ref, kp_ref, kc_ref, vp_ref, vc_ref, o_ref, lse_ref = refs
    else:
        q_ref, kc_ref, vc_ref, o_ref, lse_ref = refs
        kp_ref = vp_ref = None
    n = pl.program_id(2)
    hg = pl.program_id(3)
    blk_sz = DIL_BLOCK
    dh = DIL_HEAD_DIM
    nkeys = 2 * blk_sz if has_prev else blk_sz
    qi = lax.broadcasted_iota(jnp.int32, (blk_sz, nkeys), 0)
    kj = lax.broadcasted_iota(jnp.int32, (blk_sz, nkeys), 1)
    dist = (nkeys - blk_sz) + qi - kj
    bias_full = jnp.where(dist >= 0, jnp.where(dist <= span, 0.0, NEG_INF), NEG_INF)
    bias_first = jnp.where(kj >= nkeys - blk_sz, bias_full, NEG_INF)
    lane = lax.broadcasted_iota(jnp.int32, (blk_sz, LANES), 1)

    @pl.when(hg == 0)
    def _():
        lse_ref[...] = jnp.zeros(lse_ref.shape, F32)

    def stack(ref, res, rows):
        return jnp.stack([ref[res, rows, h * dh:(h + 1) * dh] for h in range(heads)])

    def operands(res, blk):
        rows = slice(blk * blk_sz, (blk + 1) * blk_sz)
        q3, k3, v3 = stack(q_ref, res, rows), stack(kc_ref, res, rows), stack(vc_ref, res, rows)
        bias = bias_full
        if has_prev:
            if blk == 0:
                first = slice(0, blk_sz)
                kp3, vp3 = stack(kp_ref, res, first), stack(vp_ref, res, first)
                bias = jnp.where(n > 0, bias_full, bias_first)
            else:
                prows = slice((blk - 1) * blk_sz, blk * blk_sz)
                kp3, vp3 = stack(kc_ref, res, prows), stack(vc_ref, res, prows)
            k3 = jnp.concatenate([kp3, k3], axis=1)
            v3 = jnp.concatenate([vp3, v3], axis=1)
        return q3, k3, v3, bias

    items = [(res, blk) for res in range(nres) for blk in range(nblk)]
    per_unit = 1 if has_prev else 2
    for first_item in range(0, len(items), per_unit):
        group = items[first_item:first_item + per_unit]
        ops = [operands(res, blk) for res, blk in group]
        q3, k3, v3 = (jnp.concatenate([op[i] for op in ops], axis=0) for i in range(3))
        bias = ops[0][3]
        s = jnp.einsum("hqd,hkd->hqk", q3, k3, preferred_element_type=F32) + bias[None]
        m = jnp.max(s, axis=-1, keepdims=True)
        p = jnp.exp2(s - m)
        den = jnp.sum(p, axis=-1, keepdims=True)
        o = jnp.einsum("hqk,hkd->hqd", p.astype(BF16), v3, preferred_element_type=F32) / den
        lse = m * LN2 + jnp.log(den)
        for g, (res, blk) in enumerate(group):
            rows = slice(blk * blk_sz, (blk + 1) * blk_sz)
            lse_tile = lse_ref[res, rows, :]
            for h in range(heads):
                o_ref[res, rows, h * dh:(h + 1) * dh] = o[g * heads + h].astype(o_ref.dtype)
                lse_tile = jnp.where(lane == hg * heads + h, lse[g * heads + h], lse_tile)
            lse_ref[res, rows, :] = lse_tile


def dilated_pattern(qkv_perm, window, dilation, *, hw, units):
    span = window // dilation
    length = SEQ // dilation
    nblk = min(length // DIL_BLOCK, units)
    nres = min(dilation, units // nblk)
    tl = nblk * DIL_BLOCK
    ntile = length // tl
    has_prev = length > DIL_BLOCK
    heads = hw // DIL_HEAD_DIM
    ngroups = ODD_MIX // hw

    def cur(which):
        return pl.BlockSpec((None, nres, tl, hw), lambda b, r, n, g: (b, r, n, which * ngroups + g))

    def prev(which):
        return pl.BlockSpec((None, nres, DIL_BLOCK, hw),
                            lambda b, r, n, g: (b, r, jnp.maximum(n * nblk - 1, 0), which * ngroups + g))

    if has_prev:
        in_specs = [cur(0), prev(1), cur(1), prev(2), cur(2)]
    else:
        in_specs = [cur(0), cur(1), cur(2)]
    return pl.pallas_call(
        functools.partial(_dilated_kernel, nres=nres, nblk=nblk, has_prev=has_prev, span=span, heads=heads),
        out_shape=(jax.ShapeDtypeStruct((BATCH, dilation, length, ODD_MIX), BF16),
                   jax.ShapeDtypeStruct((BATCH, dilation, length, LANES), F32)),
        grid=(BATCH, dilation // nres, ntile, ngroups),
        in_specs=in_specs,
        out_specs=(pl.BlockSpec((None, nres, tl, hw), lambda b, r, n, g: (b, r, n, g)),
                   pl.BlockSpec((None, nres, tl, LANES), lambda b, r, n, g: (b, r, n, 0))),
        compiler_params=_params(("parallel", "parallel", "arbitrary", "arbitrary")),
        name="dilated_d%d" % dilation,
    )(*([qkv_perm] * len(in_specs)))


def _merge_kernel(o1_ref, o4_ref, o16_ref, l1_ref, l4_ref, l16_ref, out_ref,
                  nat_scr, p4_scr, o4_scr, o_scr, *, tm):
    dh = DIL_HEAD_DIM
    q4, q16 = tm // 4, tm // 16

    for r4 in range(4):
        for a in range(4):
            p4_scr[r4, pl.ds(a, q16, stride=4), :] = l16_ref[r4 + 4 * a]
    for r4 in range(4):
        nat_scr[0, pl.ds(r4, q4, stride=4), :] = l4_ref[r4]
        nat_scr[1, pl.ds(r4, q4, stride=4), :] = p4_scr[r4]
    l1, l4, l16 = l1_ref[0], nat_scr[0], nat_scr[1]
    mx = jnp.maximum(jnp.maximum(l1, l4), l16)
    e1, e4, e16 = jnp.exp(l1 - mx), jnp.exp(l4 - mx), jnp.exp(l16 - mx)
    tot = e1 + e4 + e16
    w1 = e1 / tot
    nat_scr[0] = e4 / tot
    nat_scr[1] = e16 / tot
    w4 = jnp.concatenate([nat_scr[0, pl.ds(r4, q4, stride=4), :] for r4 in range(4)], axis=0)
    for r4 in range(4):
        p4_scr[r4] = nat_scr[1, pl.ds(r4, q4, stride=4), :]
    w16_parts = [None] * 16
    for r4 in range(4):
        for a in range(4):
            w16_parts[r4 + 4 * a] = p4_scr[r4, pl.ds(a, q16, stride=4), :]
    w16 = jnp.concatenate(w16_parts, axis=0)

    head_of_col = lax.broadcasted_iota(jnp.int32, (2 * LANES, ODD_MIX), 1) // dh
    lane_of_row = lax.broadcasted_iota(jnp.int32, (2 * LANES, ODD_MIX), 0) % LANES
    expand = jnp.where(head_of_col == lane_of_row, 1.0, 0.0).astype(BF16)

    def two_terms(w):
        hi = w.astype(BF16)
        return jnp.concatenate([hi, (w - hi.astype(F32)).astype(BF16)], axis=1)

    w1, w4, w16 = two_terms(w1), two_terms(w4), two_terms(w16)

    pair = 2
    for hp in range(DIL_HEADS // pair):
        spread = expand[:, hp * pair * dh:(hp + 1) * pair * dh]
        w1p, w4p, w16p = _dot(w1, spread), _dot(w4, spread), _dot(w16, spread)
        for hh in range(pair):
            cols = slice((hp * pair + hh) * dh, (hp * pair + hh + 1) * dh)
            part = slice(hh * dh, (hh + 1) * dh)
            for r4 in range(4):
                for a in range(4):
                    r = r4 + 4 * a
                    o4_scr[r4, pl.ds(a, q16, stride=4), :] = (w16p[r * q16:(r + 1) * q16, part]
                                                             * o16_ref[r, :, cols])
            for r4 in range(4):
                acc4 = o4_scr[r4] + w4p[r4 * q4:(r4 + 1) * q4, part] * o4_ref[r4, :, cols]
                o_scr[pl.ds(r4, q4, stride=4), :] = acc4
            out_ref[:, cols] = (o_scr[...] + w1p[:, part] * o1_ref[0, :, cols]).astype(out_ref.dtype)


def merge_patterns(outs, lses, dilations, *, tm):
    assert tuple(dilations) == (1, 4, 16), "the re-interleave is written as two stride-4 passes"
    tiles = SEQ // tm

    def spec(d, width):
        return pl.BlockSpec((None, d, tm // d, width), lambda i: (i // tiles, 0, i % tiles, 0))

    return pl.pallas_call(
        functools.partial(_merge_kernel, tm=tm),
        out_shape=jax.ShapeDtypeStruct((TOKENS, ODD_MIX), BF16),
        grid=(TOKENS // tm,),
        in_specs=[spec(d, ODD_MIX) for d in dilations] + [spec(d, LANES) for d in dilations],
        out_specs=pl.BlockSpec((tm, ODD_MIX), lambda i: (i, 0)),
        scratch_shapes=[pltpu.VMEM((2, tm, LANES), F32),
                        pltpu.VMEM((4, tm // 4, LANES), F32),
                        pltpu.VMEM((4, tm // 4, DIL_HEAD_DIM), F32),
                        pltpu.VMEM((tm, DIL_HEAD_DIM), F32)],
        compiler_params=_params(("parallel",)),
        name="merge_patterns",
    )(*outs, *lses)


def _rope_tables():
    pos = jnp.arange(SEQ, dtype=F32)[:, None]
    half = MLA_ROPE // 2
    inv = ROPE_THETA ** (-jnp.arange(half, dtype=F32) * 2.0 / MLA_ROPE)
    ang = pos * inv[None, :]
    c, s = jnp.cos(ang), jnp.sin(ang)
    z = jnp.zeros_like(c)
    small = jnp.concatenate([c, c, z, z, -s, z, z, z, z, s, z, z], axis=-1)
    small_q = jnp.concatenate([c, c, z, z, -s, s, z, z], axis=-1)
    half = DIL_HEAD_DIM // 2
    inv = ROPE_THETA ** (-jnp.arange(half, dtype=F32) * 2.0 / DIL_HEAD_DIM)
    ang = pos * inv[None, :]
    c, s = jnp.cos(ang), jnp.sin(ang)
    full = jnp.concatenate([c, c, -s, s], axis=-1)
    return small, small_q, full


def _w_in_relayout_kernel(wt_ref, o_ref):
    sizes = (MLA_Q_LORA, MLA_KV_LORA, MLA_ROPE, 2 * MLSTM_HEADS * MLSTM_DK, MLSTM_HEADS * MLSTM_DV,
             MLSTM_HEADS, MLSTM_HEADS, MLSTM_HEADS * MLSTM_DV)
    ends = np.cumsum(sizes)
    c_q, c_kv, k_r, m_qk, m_v, m_i, m_f, m_o = [slice(int(e - n), int(e)) for e, n in zip(ends, sizes)]
    gates = slice(m_i.start, m_f.stop)
    rows = wt_ref.shape[1]
    pad = jnp.zeros((LANES - MLA_ROPE - (gates.stop - gates.start), rows), F32)
    tail = jnp.concatenate([wt_ref[k_r, :], wt_ref[gates, :], pad], axis=0)
    groups = [wt_ref[c, :].T for c in (c_q, c_kv, m_qk, m_v, m_o)] + [tail.T]
    o_ref[...] = jnp.concatenate(groups, axis=1).astype(o_ref.dtype)


def even_in_weights(ev_w_in, index, *, rows):
    _, d, n = ev_w_in.shape
    return pl.pallas_call(
        _w_in_relayout_kernel,
        out_shape=jax.ShapeDtypeStruct((d, EV_PROJ), BF16),
        grid=(d // rows,),
        in_specs=[pl.BlockSpec((None, n, rows), lambda r: (index, 0, r))],
        out_specs=pl.BlockSpec((rows, EV_PROJ), lambda r: (r, 0)),
        compiler_params=_params(("parallel",)),
        name="even_in_weights",
    )(jnp.swapaxes(ev_w_in, 1, 2))


def _even_weights(w_uq, w_ukv, b_i, b_f):
    uq = w_uq.reshape(MLA_Q_LORA, MLA_HEADS, MLA_QK)
    uq = jnp.concatenate([uq, uq[:, :, MLA_NOPE:]], axis=-1)
    w_uq_r = uq.reshape(MLA_Q_LORA, MLA_HEADS * 2 * LANES).astype(BF16)
    gate_bias = jnp.concatenate([jnp.zeros((TAIL_I,), F32), b_i.astype(F32), b_f.astype(F32),
                                 jnp.zeros((LANES - TAIL_F - MLSTM_HEADS,), F32)]).reshape(1, LANES)
    return w_uq_r, w_ukv.astype(BF16), gate_bias


def _even_layer(x, xn, w_in_r, q_norm, w_uq, kv_norm, w_ukv, conv_w, conv_b, b_i, b_f, tab_q, tab_k, casts):
    w_uq_r, w_ukv_r, gate_bias = _even_weights(w_uq, w_ukv, b_i, b_f)
    proj = matmul_weight_stationary(xn, w_in_r, tm=1024, tn=EV_PROJ // 3, out_dtype=F32)
    scale = MLA_QK ** -0.5 * LOG2E
    q, kv, kr = mla_prep(proj, q_norm, kv_norm, w_uq_r, w_ukv_r, tab_q * scale, tab_k, tm=512, scale=scale)
    a_out, cast_weights = mla_attention(q, kv, kr, casts, tq=512, heads=4)
    hm = mlstm(proj, conv_w, conv_b, gate_bias)
    w_out_b = cast_weights[0][0]
    assert a_out.shape[1] == hm.shape[1]
    return matmul_residual(x, [(a_out, w_out_b, 0), (hm, w_out_b, 1)], tm=512), cast_weights


def _odd_layer(x, xn, w_qkv_b, w_out_b, full_tab):
    scale = DIL_HEAD_DIM ** -0.5 * LOG2E
    identity = jnp.concatenate([jnp.ones((SEQ, LANES), F32), jnp.zeros((SEQ, LANES), F32)], axis=-1)
    tab = jnp.stack([full_tab * scale, full_tab, identity])
    dilations = [d for _, d in DIL_PATTERNS]
    qkv_perms = qkv_projection(xn, w_qkv_b, tab, dilations, tm=1024, tn=1024)
    outs, lses = [], []
    for (window, dilation), qkv_perm in zip(DIL_PATTERNS, qkv_perms):
        o_g, lse_g = dilated_pattern(qkv_perm, window, dilation, hw=1024, units=8)
        outs.append(o_g)
        lses.append(lse_g)
    o = merge_patterns(outs, lses, dilations, tm=512)
    return matmul_residual(x, [(o, w_out_b, 0)], tm=512)


def kernel(x, norm_mix, norm_mlp, ev_w_in, mla_q_norm, mla_w_uq, mla_kv_norm, mla_w_ukv,
           mlstm_conv_w, mlstm_conv_b, mlstm_b_i, mlstm_b_f, ev_w_out, od_w_qkv, od_w_out,
           mlp_w1, mlp_w2, norm_final):
    assert x.shape == (BATCH, SEQ, D_MODEL) and x.dtype == F32
    tab_k, tab_q, full_tab = _rope_tables()
    xt = x.reshape(TOKENS, D_MODEL)
    xn = rms_norm_bf16(xt, norm_mix[0], tm=512)
    w1b = w2b = w_qkv_b = w_od_out_b = None
    for layer in range(DEPTH):
        i = layer // 2
        if layer % 2 == 0:
            casts = [(ev_w_out, i, 1)]
            if layer + 1 < DEPTH:
                casts += [(od_w_qkv, i, 1), (od_w_out, i, 1)]
            if layer == 0:
                casts += [(mlp_w1, 0, 1), (mlp_w2, 0, 1)]
            w_in_r = even_in_weights(ev_w_in, i, rows=256)
            xt, cast_weights = _even_layer(xt, xn, w_in_r, mla_q_norm[i], mla_w_uq[i], mla_kv_norm[i],
                                           mla_w_ukv[i], mlstm_conv_w[i], mlstm_conv_b[i], mlstm_b_i[i],
                                           mlstm_b_f[i], tab_q, tab_k, casts)
            if layer + 1 < DEPTH:
                w_qkv_b, w_od_out_b = cast_weights[1][0], cast_weights[2][0]
            if layer == 0:
                w1b, w2b = cast_weights[-2], cast_weights[-1]
        else:
            xt = _odd_layer(xt, xn, w_qkv_b, w_od_out_b, full_tab)
        if layer == DEPTH - 1:
            (xt,), _ = mlp_block(xt, norm_mlp[layer], w1b, w2b, 0, norm_final, [], tm=512, tf=1024,
                                 tail="final")
        else:
            casts = [(mlp_w1, layer + 1, 1), (mlp_w2, layer + 1, 1)]
            (xt, xn), (w1b, w2b) = mlp_block(xt, norm_mlp[layer], w1b, w2b, 0, norm_mix[layer + 1], casts,
                                             tm=512, tf=1024, tail="next")
    return xt.reshape(BATCH, SEQ, D_MODEL)
```
